```python
import math
import jax, jax.numpy as jnp
from jax import lax
import numpy as np

D_MODEL = 2048
BATCH = 4
SEQ = 4096
DEPTH = 1

GRID_W = 64
CTX_LEN = 256
MIX_WIDTH = D_MODEL
FOURIER_WIDTH = MIX_WIDTH // 2
FOURIER_GROUPS = 8
FOURIER_GROUP_DIM = FOURIER_WIDTH // FOURIER_GROUPS
GDN_WIDTH = MIX_WIDTH - FOURIER_WIDTH
GDN_HEADS = 8
GDN_HEAD_DIM = GDN_WIDTH // GDN_HEADS
CONV_K = 5
CHUNK = 64
N_DIR = 2
IN_COLS = FOURIER_WIDTH + 4 * GDN_WIDTH + 2 * N_DIR * GDN_HEADS
N_EXPERTS = 16
CAPACITY_FACTOR = 2
EXPERT_FF = 3 * D_MODEL // 4
EPS = 1e-6

kernel_name = 'hybrid_fourier_gdn_ecmoe_dit'


def rmsnorm(x, g):
    xf = x.astype(jnp.float32)
    y = xf * lax.rsqrt(jnp.mean(xf * xf, axis=-1, keepdims=True) + EPS)
    return y.astype(x.dtype) * g


def l2norm(t):
    tf = t.astype(jnp.float32)
    return tf * lax.rsqrt(jnp.sum(tf * tf, axis=-1, keepdims=True) + EPS)


def modulate(x, g, shift, scale):
    return rmsnorm(x, g) * (1 + scale) + shift


def short_conv(x, w):
    half = CONV_K // 2
    L = x.shape[2]
    xp = jnp.pad(x, ((0, 0), (0, 0), (half, half), (0, 0)))
    return sum(xp[:, :, j:j + L] * w[j] for j in range(CONV_K))


def fourier_mix(f):
    B, N, _ = f.shape
    fg = f.reshape(B, N, FOURIER_GROUPS, FOURIER_GROUP_DIM).astype(jnp.float32)
    y = jnp.fft.fft2(fg, axes=(1, 3), norm='ortho').real
    return y.reshape(B, N, FOURIER_WIDTH).astype(f.dtype)


def gated_delta_chunked(q, k, v, g, beta, s0):
    B, T, H, K = q.shape
    V = v.shape[-1]
    n = T // CHUNK

    def blocks(t):
        t = t.astype(jnp.float32).reshape((B, n, CHUNK) + t.shape[2:])
        return jnp.moveaxis(t, 2, 3)

    q, k, v, g, beta = (blocks(t) for t in (q, k, v, g, beta))
    gc = jnp.cumsum(g, axis=-1)
    lower = jnp.tril(jnp.ones((CHUNK, CHUNK), bool))
    strict = jnp.tril(jnp.ones((CHUNK, CHUNK), bool), -1)
    diff = gc[..., :, None] - gc[..., None, :]
    decay = jnp.where(lower, jnp.exp(jnp.where(lower, diff, 0.0)), 0.0)
    kb = k * beta[..., None]
    m = jnp.where(strict, jnp.einsum('bnhik,bnhjk->bnhij', kb, k) * decay, 0.0)
    eye = jnp.eye(CHUNK, dtype=jnp.float32)
    t_inv = lax.linalg.triangular_solve(m + eye, jnp.broadcast_to(eye, m.shape),
                                        left_side=True, lower=True, unit_diagonal=True)
    u = jnp.einsum('bnhij,bnhjv->bnhiv', t_inv, v * beta[..., None])
    w = jnp.einsum('bnhij,bnhjk->bnhik', t_inv, kb * jnp.exp(gc)[..., None])
    qk = jnp.einsum('bnhik,bnhjk->bnhij', q, k) * decay
    q_dec = q * jnp.exp(gc)[..., None]
    k_dec = k * jnp.exp(gc[..., -1:] - gc)[..., None]
    g_tot = jnp.exp(gc[..., -1])

    def step(s, xs):
        u_c, w_c, qk_c, qd_c, kd_c, gt_c = xs
        v_new = u_c - jnp.einsum('bhck,bhkv->bhcv', w_c, s)
        o_c = jnp.einsum('bhck,bhkv->bhcv', qd_c, s) + jnp.einsum('bhij,bhjv->bhiv', qk_c, v_new)
        s = s * gt_c[..., None, None] + jnp.einsum('bhck,bhcv->bhkv', kd_c, v_new)
        return s, o_c

    xs = tuple(jnp.moveaxis(t, 1, 0) for t in (u, w, qk, q_dec, k_dec, g_tot))
    s_final, o = lax.scan(step, s0.astype(jnp.float32), xs)
    o = jnp.transpose(o, (1, 0, 3, 2, 4)).reshape(B, T, H, V)
    return o, s_final


def bidirectional_gated_delta(q, k, v, a, b, a_log, dt_bias, s0_fwd, s0_bwd):
    g = -jnp.exp(a_log.astype(jnp.float32)) * jax.nn.softplus(a.astype(jnp.float32) + dt_bias.astype(jnp.float32))
    beta = jax.nn.sigmoid(b.astype(jnp.float32))
    o_f, s_f = gated_delta_chunked(q, k, v, g[:, :, 0], beta[:, :, 0], s0_fwd)
    rev = lambda t: jnp.flip(t, axis=1)
    o_b, s_b = gated_delta_chunked(rev(q), rev(k), rev(v), rev(g[:, :, 1]), rev(beta[:, :, 1]), s0_bwd)
    return o_f + rev(o_b), s_f, s_b


def token_mixer(h, w_in, conv_w, a_log, dt_bias, gdn_norm_w, w_out, s0_fwd, s0_bwd, n_rows, with_output):
    B, N, _ = h.shape
    p = h @ w_in
    cuts = [FOURIER_WIDTH, FOURIER_WIDTH + 3 * GDN_WIDTH, FOURIER_WIDTH + 4 * GDN_WIDTH,
            FOURIER_WIDTH + 4 * GDN_WIDTH + N_DIR * GDN_HEADS]
    f_in, qkv, z, a, b = jnp.split(p, cuts, axis=-1)
    qkv = jax.nn.silu(short_conv(qkv.reshape(B, n_rows, N // n_rows, 3 * GDN_WIDTH), conv_w))
    qkv = qkv.reshape(B, N, 3, GDN_HEADS, GDN_HEAD_DIM)
    q = l2norm(qkv[:, :, 0]) * GDN_HEAD_DIM ** -0.5
    k = l2norm(qkv[:, :, 1])
    v = qkv[:, :, 2]
    o, s_f, s_b = bidirectional_gated_delta(q, k, v, a.reshape(B, N, N_DIR, GDN_HEADS),
                                            b.reshape(B, N, N_DIR, GDN_HEADS), a_log, dt_bias, s0_fwd, s0_bwd)
    if not with_output:
        return None, s_f, s_b
    zg = jax.nn.silu(z.astype(jnp.float32).reshape(B, N, GDN_HEADS, GDN_HEAD_DIM))
    o = (rmsnorm(o, gdn_norm_w.astype(jnp.float32)) * zg).astype(h.dtype)
    out = jnp.concatenate([fourier_mix(f_in), o.reshape(B, N, GDN_WIDTH)], axis=-1) @ w_out
    return out, s_f, s_b


def expert_choice_ffn(h, w_router, w_gate, w_up, w_down):
    B, N, D = h.shape
    cap = CAPACITY_FACTOR * N // N_EXPERTS
    aff = jax.nn.softmax((h @ w_router).astype(jnp.float32), axis=-1)
    vals, idx = lax.top_k(jnp.swapaxes(aff, 1, 2), cap)
    xg = jax.vmap(lambda hb, ib: hb[ib])(h, idx)
    hid = jax.nn.silu(jnp.einsum('becd,edf->becf', xg, w_gate)) * jnp.einsum('becd,edf->becf', xg, w_up)
    y = jnp.einsum('becf,efd->becd', hid, w_down) * vals[..., None].astype(h.dtype)
    return jax.vmap(lambda yb, ib: jnp.zeros((N, D), yb.dtype).at[ib.reshape(-1)].add(yb.reshape(-1, D)))(y, idx)


def setup_inputs(seed: int = 0) -> dict:
    key = jax.random.key(seed)
    ks = jax.random.split(key, 20)
    f32 = jnp.float32
    D = D_MODEL

    def nrm(k, shape, scale):
        return jax.random.normal(k, shape, f32) * scale

    dt = jnp.exp(jax.random.uniform(ks[10], (DEPTH, N_DIR, GDN_HEADS), f32, math.log(1e-3), math.log(1e-1)))
    return {
        'x': nrm(ks[0], (BATCH, SEQ, D), 1.0),
        'c': nrm(ks[1], (BATCH, D), 1.0),
        'ctx': nrm(ks[2], (BATCH, CTX_LEN, D), 1.0),
        'c_ctx': nrm(ks[3], (D,), 1.0),
        'w_mod': nrm(ks[4], (DEPTH, D, 6 * D), D ** -0.5),
        'b_mod': nrm(ks[5], (DEPTH, 6 * D), 0.01),
        'norm1_g': 1.0 + nrm(ks[6], (DEPTH, D), 0.02),
        'norm2_g': 1.0 + nrm(ks[7], (DEPTH, D), 0.02),
        'w_in': nrm(ks[8], (DEPTH, D, IN_COLS), D ** -0.5),
        'conv_w': nrm(ks[9], (DEPTH, CONV_K, 3 * GDN_WIDTH), CONV_K ** -0.5),
        'a_log': jnp.log(jax.random.uniform(ks[11], (DEPTH, N_DIR, GDN_HEADS), f32, 1.0, 16.0)),
        'dt_bias': dt + jnp.log(-jnp.expm1(-dt)),
        'gdn_norm_w': 1.0 + nrm(ks[12], (DEPTH, GDN_HEAD_DIM), 0.02),
        'w_out': nrm(ks[13], (DEPTH, MIX_WIDTH, D), MIX_WIDTH ** -0.5),
        'w_router': nrm(ks[14], (DEPTH, D, N_EXPERTS), D ** -0.5),
        'w_gate': nrm(ks[15], (DEPTH, N_EXPERTS, D, EXPERT_FF), D ** -0.5),
        'w_up': nrm(ks[16], (DEPTH, N_EXPERTS, D, EXPERT_FF), D ** -0.5),
        'w_down': nrm(ks[17], (DEPTH, N_EXPERTS, EXPERT_FF, D), EXPERT_FF ** -0.5),
        'norm_f': 1.0 + nrm(ks[18], (D,), 0.02),
    }


def reference(x, c, ctx, c_ctx, w_mod, b_mod, norm1_g, norm2_g, w_in, conv_w, a_log, dt_bias,
              gdn_norm_w, w_out, w_router, w_gate, w_up, w_down, norm_f):
    B, N, _ = x.shape
    rows = N // GRID_W
    zero_state = jnp.zeros((B, GDN_HEADS, GDN_HEAD_DIM, GDN_HEAD_DIM), jnp.float32)
    for i in range(DEPTH):
        last = i == DEPTH - 1
        sh1_x, sc1_x, gt1_x, sh2_x, sc2_x, gt2_x = jnp.split(
            (jax.nn.silu(c) @ w_mod[i] + b_mod[i])[:, None, :], 6, axis=-1)
        sh1_c, sc1_c, gt1_c, sh2_c, sc2_c, gt2_c = jnp.split(
            jax.nn.silu(c_ctx) @ w_mod[i] + b_mod[i], 6, axis=-1)
        mix_w = (w_in[i], conv_w[i], a_log[i], dt_bias[i], gdn_norm_w[i], w_out[i])
        hc = modulate(ctx, norm1_g[i], sh1_c, sc1_c)
        ctx_mix, s_f, s_b = token_mixer(hc, *mix_w, zero_state, zero_state, 1, not last)
        hx = modulate(x, norm1_g[i], sh1_x, sc1_x)
        x_mix, _, _ = token_mixer(hx, *mix_w, s_f, s_b, rows, True)
        x = x + gt1_x * x_mix
        hx = modulate(x, norm2_g[i], sh2_x, sc2_x)
        x = x + gt2_x * expert_choice_ffn(hx, w_router[i], w_gate[i], w_up[i], w_down[i])
        if not last:
            ctx = ctx + gt1_c * ctx_mix
            hc = modulate(ctx, norm2_g[i], sh2_c, sc2_c)
            ctx = ctx + gt2_c * expert_choice_ffn(hc, w_router[i], w_gate[i], w_up[i], w_down[i])
    return rmsnorm(x, norm_f)
```

```python
import functools
import math

import numpy as np
import jax
import jax.numpy as jnp
from jax import lax
from jax.experimental import pallas as pl
from jax.experimental.pallas import tpu as pltpu

F32 = jnp.float32
BF16 = jnp.bfloat16
I32 = jnp.int32

D = 2048
SEQ_N = 4096
GRID_W = 64
FW = 1024
FG = 8
FGD = 128
GW = 1024
NH = 8
HD = 128
CONV_K = 5
N_DIR = 2
NE = 16
EFF = 1536
EPS = 1e-6
GC = 128
NU = N_DIR * NH
LANES = 128
VMEM_LIMIT = 56 * 1024 * 1024


def _sigmoid(x):
    return 1.0 / (1.0 + jnp.exp(-x))


def _silu(x):
    return x * _sigmoid(x)


def _softplus(x):
    return jnp.maximum(x, 0.0) + jnp.log(1.0 + jnp.exp(-jnp.abs(x)))


def _dot(a, b):
    return jnp.dot(a, b, preferred_element_type=F32)


def _dot_nt(a, b):
    return lax.dot_general(a, b, (((1,), (1,)), ((), ())), preferred_element_type=F32)


def _dot_split(a, b):
    ah = a.astype(BF16)
    bh = b.astype(BF16)
    al = (a - ah.astype(F32)).astype(BF16)
    bl = (b - bh.astype(F32)).astype(BF16)
    return _dot(ah, bh) + (_dot(ah, bl) + _dot(al, bh))


def _cparams(sem, vmem=VMEM_LIMIT):
    return pltpu.CompilerParams(dimension_semantics=sem, vmem_limit_bytes=vmem)


def _mod_kernel(c_ref, w_ref, b_ref, o_ref):
    s = _silu(c_ref[...]).astype(BF16)
    o_ref[...] = _dot(s, w_ref[...].astype(BF16)) + b_ref[...]


def _mod_call(cc, w_mod, b_mod):
    tn = 1024
    n = w_mod.shape[1]
    return pl.pallas_call(
        _mod_kernel,
        grid=(n // tn,),
        in_specs=[pl.BlockSpec((8, D), lambda j: (0, 0)),
                  pl.BlockSpec((D, tn), lambda j: (0, j)),
                  pl.BlockSpec((1, tn), lambda j: (0, j))],
        out_specs=pl.BlockSpec((8, tn), lambda j: (0, j)),
        out_shape=jax.ShapeDtypeStruct((8, n), F32),
        compiler_params=_cparams(("parallel",)),
        name="mod",
    )(cc, w_mod, b_mod)


def _inproj_kernel(x_ref, g_ref, sh_ref, sc_ref, w_ref, wab_ref, o_ref, ab_ref, h_scr):
    @pl.when(pl.program_id(2) == 0)
    def _():
        x = x_ref[0]
        ms = jnp.mean(x * x, axis=-1, keepdims=True)
        y = x * lax.rsqrt(ms + EPS) * g_ref[...]
        h = (y * (1.0 + sc_ref[0]) + sh_ref[0]).astype(BF16)
        h_scr[...] = h
        ab_ref[0] = _dot(h, wab_ref[...])

    o_ref[0] = _dot(h_scr[...], w_ref[...]).astype(o_ref.dtype)


def _inproj_call(x, g, sh, sc, w_main, w_ab, tm, col0, ncol):
    b, t, _ = x.shape
    tn = 1024
    return pl.pallas_call(
        _inproj_kernel,
        grid=(b, t // tm, ncol),
        in_specs=[pl.BlockSpec((1, tm, D), lambda bi, i, j: (bi, i, 0)),
                  pl.BlockSpec((1, D), lambda bi, i, j: (0, 0)),
                  pl.BlockSpec((1, 1, D), lambda bi, i, j: (bi, 0, 0)),
                  pl.BlockSpec((1, 1, D), lambda bi, i, j: (bi, 0, 0)),
                  pl.BlockSpec((D, tn), lambda bi, i, j: (0, j + col0)),
                  pl.BlockSpec((D, LANES), lambda bi, i, j: (0, 0))],
        out_specs=[pl.BlockSpec((1, tm, tn), lambda bi, i, j: (bi, i, j)),
                   pl.BlockSpec((1, tm, LANES), lambda bi, i, j: (bi, i, 0))],
        out_shape=[jax.ShapeDtypeStruct((b, t, ncol * tn), BF16),
                   jax.ShapeDtypeStruct((b, t, LANES), F32)],
        scratch_shapes=[pltpu.VMEM((tm, D), BF16)],
        compiler_params=_cparams(("parallel", "parallel", "arbitrary")),
        name="inproj",
    )(x, g, sh, sc, w_main, w_ab)


def _conv_kernel(x_ref, w_ref, o_ref, *, row_len):
    j = pl.program_id(2)
    x = x_ref[0].astype(F32)
    t = x.shape[0]
    pos = lax.broadcasted_iota(I32, x.shape, 0) & (row_len - 1)
    half = CONV_K // 2
    acc = x * w_ref[half:half + 1, :]
    for tap in range(CONV_K):
        off = tap - half
        if off == 0:
            continue
        xs = pltpu.roll(x, shift=(-off) % t, axis=0)
        valid = jnp.logical_and(pos + off >= 0, pos + off < row_len)
        acc = acc + jnp.where(valid, xs, 0.0) * w_ref[tap:tap + 1, :]
    y = _silu(acc)

    @pl.when(j == 2)
    def _():
        o_ref[0] = y.astype(o_ref.dtype)

    @pl.when(j < 2)
    def _():
        scale = jnp.where(j == 0, HD ** -0.5, 1.0).astype(F32)
        for h in range(NH):
            seg = y[:, h * HD:(h + 1) * HD]
            ss = jnp.sum(seg * seg, axis=-1, keepdims=True)
            o_ref[0, :, h * HD:(h + 1) * HD] = (seg * lax.rsqrt(ss + EPS) * scale).astype(o_ref.dtype)


def _conv_call(p, conv_w8, tt, row_len, col0):
    b, t, _ = p.shape
    return pl.pallas_call(
        functools.partial(_conv_kernel, row_len=row_len),
        grid=(b, t // tt, 3),
        in_specs=[pl.BlockSpec((1, tt, GW), lambda bi, i, j: (bi, i, j + col0)),
                  pl.BlockSpec((8, GW), lambda bi, i, j: (0, j))],
        out_specs=pl.BlockSpec((1, tt, GW), lambda bi, i, j: (bi, i, j)),
        out_shape=jax.ShapeDtypeStruct((b, t, 3 * GW), BF16),
        compiler_params=_cparams(("parallel", "parallel", "arbitrary")),
        name="conv",
    )(p, conv_w8)


def _gdn_prep_kernel(qf_ref, kf_ref, vf_ref, abf_ref, qb_ref, kb_ref, vb_ref, abb_ref, alog_ref, dtb_ref,
                     u_ref, w_ref, qd_ref, kdt_ref, qk_ref, gt_ref):
    row = lax.broadcasted_iota(I32, (GC, GC), 0)
    col = lax.broadcasted_iota(I32, (GC, GC), 1)
    eye = (row == col).astype(F32)
    for d in range(N_DIR):
        q_ref, k_ref, v_ref, ab_ref = ((qf_ref, kf_ref, vf_ref, abf_ref) if d == 0
                                       else (qb_ref, kb_ref, vb_ref, abb_ref))
        incl = (col <= row) if d == 0 else (col >= row)
        strict = (col < row) if d == 0 else (col > row)
        ab = ab_ref[0]
        g_all = -jnp.exp(alog_ref[...]) * _softplus(ab + dtb_ref[...])
        beta_all = _sigmoid(ab)
        gc_all = jnp.dot(incl.astype(F32), g_all, preferred_element_type=F32,
                         precision=lax.Precision.HIGHEST)
        gc_t = gc_all.T
        last = GC - 1 if d == 0 else 0
        for h in range(NH):
            un = d * NH + h
            ca = d * NH + h
            cb = N_DIR * NH + d * NH + h
            gcol = gc_all[:, ca:ca + 1]
            grow = gc_t[ca:ca + 1, :]
            glast = gc_all[last:last + 1, ca:ca + 1]
            beta = beta_all[:, cb:cb + 1]
            q = q_ref[0, :, h * HD:(h + 1) * HD].astype(F32)
            k = k_ref[0, :, h * HD:(h + 1) * HD]
            kf = k.astype(F32)
            v = v_ref[0, :, h * HD:(h + 1) * HD].astype(F32)
            decay = jnp.where(incl, jnp.exp(jnp.where(incl, gcol - grow, 0.0)), 0.0)
            egc = jnp.exp(gcol)
            kbeta = kf * beta
            a2 = _dot_nt(jnp.concatenate([kbeta, q], axis=0).astype(BF16), k)
            m = jnp.where(strict, a2[:GC] * decay, 0.0)
            qk = a2[GC:] * decay
            blk = lambda s: (row >> int(math.log2(s))) == (col >> int(math.log2(s)))
            mb = jnp.where(blk(8), m, 0.0)
            p = eye - mb
            m2 = _dot_split(mb, mb)
            p = p + _dot_split(p, m2)
            p = p + _dot_split(p, _dot_split(m2, m2))
            s = 8
            while s < GC:
                cpart = jnp.where(jnp.logical_and(blk(2 * s), jnp.logical_not(blk(s))), m, 0.0)
                p = p - _dot_split(p, _dot_split(cpart, p))
                s *= 2
            rhs = jnp.concatenate([v * beta, kbeta * egc], axis=1).astype(BF16)
            uw = _dot(p.astype(BF16), rhs)
            u_ref[0, 0, un] = uw[:, :HD].astype(BF16)
            w_ref[0, 0, un] = uw[:, HD:].astype(BF16)
            qd_ref[0, 0, un] = (q * egc).astype(BF16)
            kdt_ref[0, 0, un] = (kf * jnp.exp(glast - gcol)).T.astype(BF16)
            qk_ref[0, 0, un] = qk.astype(BF16)
            gt_ref[0, 0, un] = jnp.broadcast_to(jnp.exp(glast), (8, LANES))


def _gdn_prep_call(qkv, ab, alog_row, dtb_row):
    b, t, _ = qkv.shape
    nc = t // GC
    blk = lambda c, rev: pl.BlockSpec((1, GC, GW), (lambda bi, n: (bi, nc - 1 - n, c)) if rev
                                      else (lambda bi, n: (bi, n, c)))
    abblk = lambda rev: pl.BlockSpec((1, GC, LANES), (lambda bi, n: (bi, nc - 1 - n, 0)) if rev
                                     else (lambda bi, n: (bi, n, 0)))
    unit = pl.BlockSpec((1, 1, NU, GC, HD), lambda bi, n: (bi, n, 0, 0, 0))
    ushape = jax.ShapeDtypeStruct((b, nc, NU, GC, HD), BF16)
    return pl.pallas_call(
        _gdn_prep_kernel,
        grid=(b, nc),
        in_specs=[blk(0, False), blk(1, False), blk(2, False), abblk(False),
                  blk(0, True), blk(1, True), blk(2, True), abblk(True),
                  pl.BlockSpec((1, LANES), lambda bi, n: (0, 0)),
                  pl.BlockSpec((1, LANES), lambda bi, n: (0, 0))],
        out_specs=[unit, unit, unit, unit, unit,
                   pl.BlockSpec((1, 1, NU, 8, LANES), lambda bi, n: (bi, n, 0, 0, 0))],
        out_shape=[ushape, ushape, ushape, ushape, ushape,
                   jax.ShapeDtypeStruct((b, nc, NU, 8, LANES), F32)],
        compiler_params=_cparams(("parallel", "parallel")),
        name="gdn_prep",
    )(qkv, qkv, qkv, ab, qkv, qkv, qkv, ab, alog_row, dtb_row)


def _gdn_scan_kernel(u_ref, w_ref, qd_ref, kdt_ref, qk_ref, gt_ref, s0_ref, of_ref, ob_ref, sout_ref, s_scr):
    n = pl.program_id(1)

    @pl.when(n == 0)
    def _():
        s_scr[...] = s0_ref[0]

    for un in range(NU):
        s = s_scr[un]
        sb = s.astype(BF16)
        r = _dot(jnp.concatenate([w_ref[0, 0, un], qd_ref[0, 0, un]], axis=0), sb)
        v_new = u_ref[0, 0, un].astype(F32) - r[:GC]
        vb = v_new.astype(BF16)
        o = r[GC:] + _dot(qk_ref[0, 0, un], vb)
        s_scr[un] = s * gt_ref[0, 0, un][0:1, :] + _dot(kdt_ref[0, 0, un], vb)
        h = un % NH
        if un < NH:
            of_ref[0, :, h * HD:(h + 1) * HD] = o
        else:
            ob_ref[0, :, h * HD:(h + 1) * HD] = o

    @pl.when(n == pl.num_programs(1) - 1)
    def _():
        sout_ref[0] = s_scr[...]


def _gdn_scan_call(u, w, qd, kdt, qk, gt, s0):
    b, nc = u.shape[0], u.shape[1]
    unit = pl.BlockSpec((1, 1, NU, GC, HD), lambda bi, n: (bi, n, 0, 0, 0))
    sblk = pl.BlockSpec((1, NU, HD, HD), lambda bi, n: (bi, 0, 0, 0))
    return pl.pallas_call(
        _gdn_scan_kernel,
        grid=(b, nc),
        in_specs=[unit, unit, unit, unit, unit,
                  pl.BlockSpec((1, 1, NU, 8, LANES), lambda bi, n: (bi, n, 0, 0, 0)), sblk],
        out_specs=[pl.BlockSpec((1, GC, GW), lambda bi, n: (bi, n, 0)),
                   pl.BlockSpec((1, GC, GW), lambda bi, n: (bi, nc - 1 - n, 0)),
                   sblk],
        out_shape=[jax.ShapeDtypeStruct((b, nc * GC, GW), F32),
                   jax.ShapeDtypeStruct((b, nc * GC, GW), F32),
                   jax.ShapeDtypeStruct((b, NU, HD, HD), F32)],
        scratch_shapes=[pltpu.VMEM((NU, HD, HD), F32)],
        compiler_params=_cparams(("parallel", "arbitrary")),
        name="gdn_scan",
    )(u, w, qd, kdt, qk, gt, s0)


def _dft_tables():
    r = GRID_W
    a = 2.0 * np.pi * np.outer(np.arange(r), np.arange(r)) / r
    c64, s64 = np.cos(a), np.sin(a)
    ac = 2.0 * np.pi * np.outer(np.arange(FGD), np.arange(FGD)) / FGD
    cc, sc = np.cos(ac), np.sin(ac)
    at = 2.0 * np.pi * np.outer(np.arange(r), np.arange(r)) / (r * r)
    row_fwd = np.concatenate([c64, -s64], axis=0)
    chan = np.concatenate([cc, -sc], axis=1)
    col_re = np.concatenate([c64, s64], axis=1)
    as_bf16 = lambda t: jnp.asarray(t, F32).astype(BF16)
    return (as_bf16(row_fwd), as_bf16(chan), as_bf16(col_re),
            jnp.asarray(np.cos(at), F32), jnp.asarray(np.sin(at), F32))


def _fourier_a_kernel(x_ref, rowf_ref, chan_ref, twc_ref, tws_ref, o_ref):
    xr = _dot(rowf_ref[...], x_ref[0])
    xb = xr.astype(BF16)
    tc = twc_ref[0]
    ts = tws_ref[0]
    r = GRID_W
    for g in range(FG):
        y = _dot(xb[:, g * FGD:(g + 1) * FGD], chan_ref[...])
        vr = y[:r, :FGD] - y[r:, FGD:]
        vi = y[r:, :FGD] + y[:r, FGD:]
        o_ref[0, :, g * FGD:(g + 1) * FGD] = (vr * tc + vi * ts).astype(BF16)
        o_ref[0, :, FW + g * FGD:FW + (g + 1) * FGD] = (vi * tc - vr * ts).astype(BF16)


def _fourier_b_kernel(v_ref, colre_ref, o_ref):
    v = v_ref[0]
    st = jnp.concatenate([v[:, :FW], v[:, FW:]], axis=0)
    y = _dot(colre_ref[...], st)
    o_ref[0] = (y * (1.0 / math.sqrt(SEQ_N * FGD))).astype(o_ref.dtype)


def _fourier_call(p_main):
    b = p_main.shape[0]
    r = GRID_W
    pc = p_main.shape[2]
    rowf, chan, colre, twc, tws = _dft_tables()
    twc = jnp.broadcast_to(twc[:, :, None], (r, r, LANES))
    tws = jnp.broadcast_to(tws[:, :, None], (r, r, LANES))
    x3 = p_main.reshape(b, r, r * pc)
    nblk = pc // FW
    va = pl.pallas_call(
        _fourier_a_kernel,
        grid=(b, r),
        in_specs=[pl.BlockSpec((1, r, FW), lambda bi, n1: (bi, 0, n1 * nblk)),
                  pl.BlockSpec((2 * r, r), lambda bi, n1: (0, 0)),
                  pl.BlockSpec((FGD, 2 * FGD), lambda bi, n1: (0, 0)),
                  pl.BlockSpec((1, r, LANES), lambda bi, n1: (n1, 0, 0)),
                  pl.BlockSpec((1, r, LANES), lambda bi, n1: (n1, 0, 0))],
        out_specs=pl.BlockSpec((1, r, 2 * FW), lambda bi, n1: (bi, 0, n1)),
        out_shape=jax.ShapeDtypeStruct((b, r, r * 2 * FW), BF16),
        compiler_params=_cparams(("parallel", "parallel")),
        name="fourier_a",
    )(x3, rowf, chan, twc, tws)
    vb = va.reshape(b, r * r, 2 * FW)
    y = pl.pallas_call(
        _fourier_b_kernel,
        grid=(b, r),
        in_specs=[pl.BlockSpec((1, r, 2 * FW), lambda bi, k2: (bi, k2, 0)),
                  pl.BlockSpec((r, 2 * r), lambda bi, k2: (0, 0))],
        out_specs=pl.BlockSpec((1, r, FW), lambda bi, k2: (bi, 0, k2)),
        out_shape=jax.ShapeDtypeStruct((b, r, r * FW), BF16),
        compiler_params=_cparams(("parallel", "parallel")),
        name="fourier_b",
    )(vb, colre)
    return y.reshape(b, r * r, FW)


def _outproj_kernel(yf_ref, of_ref, ob_ref, z_ref, x_ref, wout_ref, gnw_ref, gt1_ref, g2_ref, sh2_ref, sc2_ref,
                    wr_ref, x1_ref, hx_ref, lg_ref):
    o = of_ref[0] + ob_ref[0]
    z = z_ref[0].astype(F32)
    parts = [yf_ref[0]]
    for h in range(NH):
        oh = o[:, h * HD:(h + 1) * HD]
        ms = jnp.mean(oh * oh, axis=-1, keepdims=True)
        on = oh * lax.rsqrt(ms + EPS) * gnw_ref[...]
        parts.append((on * _silu(z[:, h * HD:(h + 1) * HD])).astype(BF16))
    mix = jnp.concatenate(parts, axis=1)
    x1 = x_ref[0] + gt1_ref[0] * _dot(mix, wout_ref[...])
    x1_ref[0] = x1
    ms = jnp.mean(x1 * x1, axis=-1, keepdims=True)
    hx = x1 * lax.rsqrt(ms + EPS) * g2_ref[...] * (1.0 + sc2_ref[0]) + sh2_ref[0]
    hx_ref[0] = hx
    lg = jnp.dot(hx, wr_ref[...], preferred_element_type=F32, precision=lax.Precision.HIGHEST)
    lg_ref[0] = lg.T[:NE, :]


def _outproj_call(yf, o_f, o_b, p_main, x, w_out, gnw, gt1, g2, sh2, sc2, w_r):
    b, t, _ = x.shape
    tm = 256
    zcol = p_main.shape[2] // GW - 1
    row = lambda c: pl.BlockSpec((1, tm, c), lambda bi, i: (bi, i, 0))
    vec = pl.BlockSpec((1, 1, D), lambda bi, i: (bi, 0, 0))
    return pl.pallas_call(
        _outproj_kernel,
        grid=(b, t // tm),
        in_specs=[row(FW), row(GW), row(GW),
                  pl.BlockSpec((1, tm, GW), lambda bi, i: (bi, i, zcol)),
                  row(D),
                  pl.BlockSpec((D, D), lambda bi, i: (0, 0)),
                  pl.BlockSpec((1, HD), lambda bi, i: (0, 0)),
                  vec,
                  pl.BlockSpec((1, D), lambda bi, i: (0, 0)),
                  vec, vec,
                  pl.BlockSpec((D, LANES), lambda bi, i: (0, 0))],
        out_specs=[row(D), row(D), pl.BlockSpec((1, NE, tm), lambda bi, i: (bi, 0, i))],
        out_shape=[jax.ShapeDtypeStruct((b, t, D), F32),
                   jax.ShapeDtypeStruct((b, t, D), F32),
                   jax.ShapeDtypeStruct((b, NE, t), F32)],
        compiler_params=_cparams(("parallel", "parallel")),
        name="outproj",
    )(yf, o_f, o_b, p_main, x, w_out, gnw, gt1, g2, sh2, sc2, w_r)


def _lane_cumsum(x):
    n = x.shape[-1]
    lane = lax.broadcasted_iota(I32, x.shape, x.ndim - 1)
    sh = 1
    while sh < n:
        x = x + jnp.where(lane >= sh, pltpu.roll(x, shift=sh, axis=x.ndim - 1), 0.0)
        sh *= 2
    return x


def _topk_kernel(lg_ref, idx_ref, val_ref, rsel_scr, aff_scr, *, cap):
    lg = lg_ref[0]
    t = lg.shape[1]
    mx = jnp.max(lg, axis=0, keepdims=True)
    ex = jnp.exp(lg - mx)
    aff = ex / jnp.sum(ex, axis=0, keepdims=True)

    def search(i, cur):
        cand = cur | (1 << (30 - i))
        cnt = jnp.sum((aff >= pltpu.bitcast(cand, F32)).astype(F32), axis=1, keepdims=True)
        return jnp.where(cnt >= cap, cand, cur)

    thr = pltpu.bitcast(lax.fori_loop(0, 31, search, jnp.zeros((NE, 1), I32)), F32)
    gt = aff > thr
    eq = aff == thr
    n_gt = jnp.sum(gt.astype(F32), axis=1, keepdims=True)
    eq_rank = _lane_cumsum(eq.astype(F32))
    sel = jnp.logical_or(gt, jnp.logical_and(eq, eq_rank <= cap - n_gt))
    rank = _lane_cumsum(sel.astype(F32))
    rsel_scr[...] = jnp.where(sel, rank, 0.0)
    aff_scr[...] = aff
    idx_ref[0] = jnp.zeros((cap, LANES), I32)
    val_ref[0] = jnp.zeros((cap, LANES), F32)
    tpos = lax.broadcasted_iota(I32, (1, t), 1).astype(F32)
    sblk = 64
    lane = lax.broadcasted_iota(I32, (sblk, LANES), 1)

    def per_expert(e, carry):
        rrow = rsel_scr[pl.ds(e, 1), :]
        arow = aff_scr[pl.ds(e, 1), :]
        for sb in range(cap // sblk):
            slot = (lax.broadcasted_iota(I32, (sblk, 1), 0) + (sb * sblk + 1)).astype(F32)
            hit = rrow == slot
            ic = jnp.sum(jnp.where(hit, tpos, 0.0), axis=1, keepdims=True)
            vc = jnp.sum(jnp.where(hit, arow, 0.0), axis=1, keepdims=True)
            rows = slice(sb * sblk, (sb + 1) * sblk)
            idx_ref[0, rows, :] = jnp.where(lane == e, ic.astype(I32), idx_ref[0, rows, :])
            val_ref[0, rows, :] = jnp.where(lane == e, vc, val_ref[0, rows, :])
        return carry

    lax.fori_loop(0, NE, per_expert, 0)


def _topk_call(lg_t, cap):
    b, _, t = lg_t.shape
    return pl.pallas_call(
        functools.partial(_topk_kernel, cap=cap),
        grid=(b,),
        in_specs=[pl.BlockSpec((1, NE, t), lambda bi: (bi, 0, 0))],
        out_specs=[pl.BlockSpec((1, cap, LANES), lambda bi: (bi, 0, 0)),
                   pl.BlockSpec((1, cap, LANES), lambda bi: (bi, 0, 0))],
        out_shape=[jax.ShapeDtypeStruct((b, cap, LANES), I32),
                   jax.ShapeDtypeStruct((b, cap, LANES), F32)],
        scratch_shapes=[pltpu.VMEM((NE, t), F32), pltpu.VMEM((NE, t), F32)],
        compiler_params=_cparams(("parallel",)),
        name="topk",
    )(lg_t)


MOE_TF = 256
MOE_MC = 512
MOE_RC = 128


def _moe_kernel(idx_ref, hx_hbm, res_hbm, val_ref, gt2_ref, wg_ref, wu_ref, wd_ref, out_hbm,
                xg_scr, acc_scr, rmw_scr, vcol_scr, gsem, rsem, wsem, *, rows, cap):
    e = pl.program_id(0)
    f = pl.program_id(1)
    base = e * rows

    def row_copy(src, r, dst_scr, dr, sem):
        return pltpu.make_async_copy(src.at[pl.ds(idx_ref[base + r], 1), :], dst_scr.at[pl.ds(dr, 1), :], sem)

    @pl.when(f == 0)
    def _():
        def start(r, c):
            row_copy(hx_hbm, r, acc_scr, r, gsem).start()
            return c
        lax.fori_loop(0, rows, start, 0)

        def wait(r, c):
            row_copy(hx_hbm, r, acc_scr, r, gsem).wait()
            return c
        lax.fori_loop(0, rows, wait, 0)
        xg_scr[...] = acc_scr[...].astype(BF16)
        acc_scr[...] = jnp.zeros_like(acc_scr)

    wg = wg_ref[0].astype(BF16)
    wu = wu_ref[0].astype(BF16)
    wd = wd_ref[0].astype(BF16)
    for mc in range(rows // MOE_MC):
        xs = xg_scr[mc * MOE_MC:(mc + 1) * MOE_MC, :]
        hid = (_silu(_dot(xs, wg)) * _dot(xs, wu)).astype(BF16)
        acc_scr[mc * MOE_MC:(mc + 1) * MOE_MC, :] += _dot(hid, wd)

    @pl.when(f == pl.num_programs(1) - 1)
    def _():
        vcol_scr[...] = jnp.broadcast_to(val_ref[0], (LANES, rows)).T
        nchunk = rows // MOE_RC

        def chunk(ci, c):
            r0 = pl.multiple_of(ci * MOE_RC, MOE_RC)

            def rstart(r, c2):
                row_copy(out_hbm, r0 + r, rmw_scr, r, rsem).start()
                return c2
            lax.fori_loop(0, MOE_RC, rstart, 0)

            def rwait(r, c2):
                row_copy(out_hbm, r0 + r, rmw_scr, r, rsem).wait()
                return c2
            lax.fori_loop(0, MOE_RC, rwait, 0)
            bi = lax.div(ci, cap // MOE_RC)
            y = acc_scr[pl.ds(r0, MOE_RC), :] * vcol_scr[pl.ds(r0, MOE_RC), :][:, 0:1]
            rmw_scr[...] = rmw_scr[...] + gt2_ref[bi] * y

            def wcopy(r):
                return pltpu.make_async_copy(rmw_scr.at[pl.ds(r, 1), :],
                                             out_hbm.at[pl.ds(idx_ref[base + r0 + r], 1), :], wsem)

            def wstart(r, c2):
                wcopy(r).start()
                return c2
            lax.fori_loop(0, MOE_RC, wstart, 0)

            def wwait(r, c2):
                wcopy(r).wait()
                return c2
            lax.fori_loop(0, MOE_RC, wwait, 0)
            return c
        lax.fori_loop(0, nchunk, chunk, 0)


def _moe_call(idx_flat, hx2, x1, vals, gt2, w_gate, w_up, w_down, cap):
    rows = vals.shape[2]
    nf = EFF // MOE_TF
    grid_spec = pltpu.PrefetchScalarGridSpec(
        num_scalar_prefetch=1,
        grid=(NE, nf),
        in_specs=[pl.BlockSpec(memory_space=pl.ANY),
                  pl.BlockSpec(memory_space=pl.ANY),
                  pl.BlockSpec((1, 1, rows), lambda e, f, idx: (e, 0, 0)),
                  pl.BlockSpec(gt2.shape, lambda e, f, idx: (0, 0, 0)),
                  pl.BlockSpec((1, D, MOE_TF), lambda e, f, idx: (e, 0, f)),
                  pl.BlockSpec((1, D, MOE_TF), lambda e, f, idx: (e, 0, f)),
                  pl.BlockSpec((1, MOE_TF, D), lambda e, f, idx: (e, f, 0))],
        out_specs=pl.BlockSpec(memory_space=pl.ANY),
        scratch_shapes=[pltpu.VMEM((rows, D), BF16),
                        pltpu.VMEM((rows, D), F32),
                        pltpu.VMEM((MOE_RC, D), F32),
                        pltpu.VMEM((rows, LANES), F32),
                        pltpu.SemaphoreType.DMA(()),
                        pltpu.SemaphoreType.DMA(()),
                        pltpu.SemaphoreType.DMA(())],
    )
    return pl.pallas_call(
        functools.partial(_moe_kernel, rows=rows, cap=cap),
        grid_spec=grid_spec,
        out_shape=jax.ShapeDtypeStruct(x1.shape, F32),
        input_output_aliases={2: 0},
        compiler_params=_cparams(("arbitrary", "arbitrary")),
        name="moe",
    )(idx_flat, hx2, x1, vals, gt2, w_gate, w_up, w_down)


def _final_kernel(x_ref, g_ref, o_ref):
    x = x_ref[...]
    ms = jnp.mean(x * x, axis=-1, keepdims=True)
    o_ref[...] = x * lax.rsqrt(ms + EPS) * g_ref[...]


def _final_call(x2d, g):
    m = x2d.shape[0]
    tm = 512
    return pl.pallas_call(
        _final_kernel,
        grid=(m // tm,),
        in_specs=[pl.BlockSpec((tm, D), lambda i: (i, 0)), pl.BlockSpec((1, D), lambda i: (0, 0))],
        out_specs=pl.BlockSpec((tm, D), lambda i: (i, 0)),
        out_shape=jax.ShapeDtypeStruct(x2d.shape, F32),
        compiler_params=_cparams(("parallel",)),
        name="final_norm",
    )(x2d, g)


def kernel(x, c, ctx, c_ctx, w_mod, b_mod, norm1_g, norm2_g, w_in, conv_w, a_log, dt_bias, gdn_norm_w, w_out,
           w_router, w_gate, w_up, w_down, norm_f):
    b, n, _ = x.shape
    nctx = ctx.shape[1]
    cap = 2 * n // NE
    i = 0

    cc = jnp.concatenate([c, c_ctx[None, :], jnp.zeros((8 - b - 1, D), F32)], axis=0)
    mod = _mod_call(cc, w_mod[i], b_mod[i][None, :])
    sh1, sc1, gt1, sh2, sc2, gt2 = [mod[:b, k * D:(k + 1) * D][:, None, :] for k in range(6)]
    sh1c = jnp.broadcast_to(mod[b:b + 1, 0:D][:, None, :], (b, 1, D))
    sc1c = jnp.broadcast_to(mod[b:b + 1, D:2 * D][:, None, :], (b, 1, D))

    nmain = FW + 4 * GW
    w_main = w_in[i][:, :nmain].astype(BF16)
    w_ab = jnp.pad(w_in[i][:, nmain:], ((0, 0), (0, LANES - 2 * N_DIR * NH))).astype(BF16)
    conv_w8 = jnp.pad(conv_w[i], ((0, 8 - CONV_K), (0, 0)))
    alog_row = jnp.pad(a_log[i].reshape(1, NU), ((0, 0), (0, LANES - NU)))
    dtb_row = jnp.pad(dt_bias[i].reshape(1, NU), ((0, 0), (0, LANES - NU)))
    g1 = norm1_g[i][None, :]

    p_ctx, ab_ctx = _inproj_call(ctx, g1, sh1c, sc1c, w_main, w_ab, nctx, 1, 3)
    qkv_ctx = _conv_call(p_ctx, conv_w8, nctx, nctx, 0)
    prep_ctx = _gdn_prep_call(qkv_ctx, ab_ctx, alog_row, dtb_row)
    s_zero = jnp.zeros((b, NU, HD, HD), F32)
    _, _, s_ctx = _gdn_scan_call(*prep_ctx, s_zero)

    p_main, ab = _inproj_call(x, g1, sh1, sc1, w_main, w_ab, 512, 0, 5)
    qkv = _conv_call(p_main, conv_w8, 512, GRID_W, 1)
    prep = _gdn_prep_call(qkv, ab, alog_row, dtb_row)
    o_f, o_b, _ = _gdn_scan_call(*prep, s_ctx)
    yf = _fourier_call(p_main)

    w_r = jnp.pad(w_router[i], ((0, 0), (0, LANES - NE)))
    x1, hx2, lg_t = _outproj_call(yf, o_f, o_b, p_main, x, w_out[i].astype(BF16), gdn_norm_w[i][None, :], gt1,
                                  norm2_g[i][None, :], sh2, sc2, w_r)

    idx_c, val_c = _topk_call(lg_t, cap)
    idx = jnp.transpose(idx_c[:, :, :NE], (2, 0, 1))
    val = jnp.transpose(val_c[:, :, :NE], (2, 0, 1))
    idx_flat = (idx + (jnp.arange(b, dtype=I32) * n)[None, :, None]).reshape(NE * b * cap)
    vals = val.reshape(NE, 1, b * cap)

    out = _moe_call(idx_flat, hx2.reshape(b * n, D), x1.reshape(b * n, D), vals, gt2, w_gate[i], w_up[i], w_down[i],
                    cap)
    return _final_call(out, norm_f[None, :]).reshape(b, n, D)
```

```python
import functools
import math

import numpy as np
import jax
import jax.numpy as jnp
from jax import lax
from jax.experimental import pallas as pl
from jax.experimental.pallas import tpu as pltpu

F32 = jnp.float32
BF16 = jnp.bfloat16
I32 = jnp.int32

D = 2048
SEQ_N = 4096
GRID_W = 64
FW = 1024
FG = 8
FGD = 128
GW = 1024
NH = 8
HD = 128
CONV_K = 5
N_DIR = 2
NE = 16
EFF = 1536
EPS = 1e-6
GC = 128
NU = N_DIR * NH
LANES = 128
VMEM_LIMIT = 56 * 1024 * 1024


def _sigmoid(x):
    return 1.0 / (1.0 + jnp.exp(-x))


def _silu(x):
    return x * _sigmoid(x)


def _softplus(x):
    return jnp.maximum(x, 0.0) + jnp.log(1.0 + jnp.exp(-jnp.abs(x)))


def _dot(a, b):
    return jnp.dot(a, b, preferred_element_type=F32)


def _dot_nt(a, b):
    return lax.dot_general(a, b, (((1,), (1,)), ((), ())), preferred_element_type=F32)


def _bmm(a, b):
    return jnp.einsum('uij,ujk->uik', a.astype(BF16), b.astype(BF16), preferred_element_type=F32)


def _bmm_nt(a, b):
    return jnp.einsum('uik,ujk->uij', a, b, preferred_element_type=F32)


def _cparams(sem, vmem=VMEM_LIMIT):
    return pltpu.CompilerParams(dimension_semantics=sem, vmem_limit_bytes=vmem)


def _mod_kernel(c_ref, w_ref, b_ref, o_ref):
    s = _silu(c_ref[...]).astype(BF16)
    o_ref[...] = _dot(s, w_ref[...].astype(BF16)) + b_ref[...]


def _mod_call(cc, w_mod, b_mod):
    tn = 1024
    n = w_mod.shape[1]
    return pl.pallas_call(
        _mod_kernel,
        grid=(n // tn,),
        in_specs=[pl.BlockSpec((8, D), lambda j: (0, 0)),
                  pl.BlockSpec((D, tn), lambda j: (0, j)),
                  pl.BlockSpec((1, tn), lambda j: (0, j))],
        out_specs=pl.BlockSpec((8, tn), lambda j: (0, j)),
        out_shape=jax.ShapeDtypeStruct((8, n), F32),
        compiler_params=_cparams(("parallel",)),
        name="mod",
    )(cc, w_mod, b_mod)


def _inproj_kernel(x_ref, g_ref, sh_ref, sc_ref, w_ref, wab_ref, o_ref, ab_ref, h_scr):
    @pl.when(pl.program_id(2) == 0)
    def _():
        x = x_ref[0]
        ms = jnp.mean(x * x, axis=-1, keepdims=True)
        y = x * lax.rsqrt(ms + EPS) * g_ref[...]
        h = (y * (1.0 + sc_ref[0]) + sh_ref[0]).astype(BF16)
        h_scr[...] = h
        ab_ref[0] = _dot(h, wab_ref[...])

    o_ref[0] = _dot(h_scr[...], w_ref[...]).astype(o_ref.dtype)


def _inproj_call(x, g, sh, sc, w_main, w_ab, tm, col0, ncol):
    b, t, _ = x.shape
    tn = 1024
    return pl.pallas_call(
        _inproj_kernel,
        grid=(b, t // tm, ncol),
        in_specs=[pl.BlockSpec((1, tm, D), lambda bi, i, j: (bi, i, 0)),
                  pl.BlockSpec((1, D), lambda bi, i, j: (0, 0)),
                  pl.BlockSpec((1, 1, D), lambda bi, i, j: (bi, 0, 0)),
                  pl.BlockSpec((1, 1, D), lambda bi, i, j: (bi, 0, 0)),
                  pl.BlockSpec((D, tn), lambda bi, i, j: (0, j + col0)),
                  pl.BlockSpec((D, LANES), lambda bi, i, j: (0, 0))],
        out_specs=[pl.BlockSpec((1, tm, tn), lambda bi, i, j: (bi, i, j)),
                   pl.BlockSpec((1, tm, LANES), lambda bi, i, j: (bi, i, 0))],
        out_shape=[jax.ShapeDtypeStruct((b, t, ncol * tn), BF16),
                   jax.ShapeDtypeStruct((b, t, LANES), F32)],
        scratch_shapes=[pltpu.VMEM((tm, D), BF16)],
        compiler_params=_cparams(("parallel", "parallel", "arbitrary")),
        name="inproj",
    )(x, g, sh, sc, w_main, w_ab)


def _conv_kernel(x_ref, w_ref, o_ref, *, row_len):
    j = pl.program_id(2)
    x = x_ref[0].astype(F32)
    t = x.shape[0]
    pos = lax.broadcasted_iota(I32, x.shape, 0) & (row_len - 1)
    half = CONV_K // 2
    acc = x * w_ref[half:half + 1, :]
    for tap in range(CONV_K):
        off = tap - half
        if off == 0:
            continue
        xs = pltpu.roll(x, shift=(-off) % t, axis=0)
        valid = jnp.logical_and(pos + off >= 0, pos + off < row_len)
        acc = acc + jnp.where(valid, xs, 0.0) * w_ref[tap:tap + 1, :]
    y = _silu(acc)

    @pl.when(j == 2)
    def _():
        o_ref[0] = y.astype(o_ref.dtype)

    @pl.when(j < 2)
    def _():
        scale = jnp.where(j == 0, HD ** -0.5, 1.0).astype(F32)
        for h in range(NH):
            seg = y[:, h * HD:(h + 1) * HD]
            ss = jnp.sum(seg * seg, axis=-1, keepdims=True)
            o_ref[0, :, h * HD:(h + 1) * HD] = (seg * lax.rsqrt(ss + EPS) * scale).astype(o_ref.dtype)


def _conv_call(p, conv_w8, tt, row_len, col0):
    b, t, _ = p.shape
    return pl.pallas_call(
        functools.partial(_conv_kernel, row_len=row_len),
        grid=(b, t // tt, 3),
        in_specs=[pl.BlockSpec((1, tt, GW), lambda bi, i, j: (bi, i, j + col0)),
                  pl.BlockSpec((8, GW), lambda bi, i, j: (0, j))],
        out_specs=pl.BlockSpec((1, tt, GW), lambda bi, i, j: (bi, i, j)),
        out_shape=jax.ShapeDtypeStruct((b, t, 3 * GW), BF16),
        compiler_params=_cparams(("parallel", "parallel", "arbitrary")),
        name="conv",
    )(p, conv_w8)


def _gdn_prep_kernel(qf_ref, kf_ref, vf_ref, abf_ref, qb_ref, kb_ref, vb_ref, abb_ref, alog_ref, dtb_ref,
                     u_ref, w_ref, qd_ref, kdt_ref, qk_ref, gt_ref):
    row = lax.broadcasted_iota(I32, (GC, GC), 0)
    col = lax.broadcasted_iota(I32, (GC, GC), 1)
    eye = (row == col).astype(F32)
    m_parts, rhs_parts = [], []
    for d in range(N_DIR):
        q_ref, k_ref, v_ref, ab_ref = ((qf_ref, kf_ref, vf_ref, abf_ref) if d == 0
                                       else (qb_ref, kb_ref, vb_ref, abb_ref))
        incl = (col <= row) if d == 0 else (col >= row)
        strict = (col < row) if d == 0 else (col > row)
        ab = ab_ref[0]
        g_all = -jnp.exp(alog_ref[...]) * _softplus(ab + dtb_ref[...])
        beta_all = _sigmoid(ab)
        gc_all = jnp.dot(incl.astype(F32), g_all, preferred_element_type=F32,
                         precision=lax.Precision.HIGHEST)
        gc_t = gc_all.T
        last = GC - 1 if d == 0 else 0
        c0 = d * NH
        cb = N_DIR * NH + d * NH
        heads = range(NH)
        gcol = jnp.stack([gc_all[:, c0 + h:c0 + h + 1] for h in heads])
        grow = jnp.stack([gc_t[c0 + h:c0 + h + 1, :] for h in heads])
        glast = jnp.stack([gc_all[last:last + 1, c0 + h:c0 + h + 1] for h in heads])
        beta = jnp.stack([beta_all[:, cb + h:cb + h + 1] for h in heads])
        q = jnp.stack([q_ref[0, :, h * HD:(h + 1) * HD] for h in heads]).astype(F32)
        k = jnp.stack([k_ref[0, :, h * HD:(h + 1) * HD] for h in heads])
        kf = k.astype(F32)
        v = jnp.stack([v_ref[0, :, h * HD:(h + 1) * HD] for h in heads]).astype(F32)
        decay = jnp.where(incl[None], jnp.exp(jnp.where(incl[None], gcol - grow, 0.0)), 0.0)
        egc = jnp.exp(gcol)
        kbeta = kf * beta
        a2 = _bmm_nt(jnp.concatenate([kbeta, q], axis=1).astype(BF16), k)
        m_parts.append(jnp.where(strict[None], a2[:, :GC] * decay, 0.0))
        rhs_parts.append(jnp.concatenate([v * beta, kbeta * egc], axis=2).astype(BF16))
        qd_ref[0, 0, c0:c0 + NH] = (q * egc).astype(BF16)
        kdt_ref[0, 0, c0:c0 + NH] = jnp.swapaxes(kf * jnp.exp(glast - gcol), 1, 2).astype(BF16)
        qk_ref[0, 0, c0:c0 + NH] = (a2[:, GC:] * decay).astype(BF16)
        gt_ref[0, 0, c0:c0 + NH] = jnp.broadcast_to(jnp.exp(glast), (NH, 8, LANES))

    m = jnp.concatenate(m_parts, axis=0)
    rhs = jnp.concatenate(rhs_parts, axis=0)
    blk = lambda s: ((row >> int(math.log2(s))) == (col >> int(math.log2(s))))[None]
    mb = jnp.where(blk(8), m, 0.0)
    p = eye[None] - mb
    m2 = _bmm(mb, mb)
    p = p + _bmm(p, m2)
    p = p + _bmm(p, _bmm(m2, m2))
    s = 8
    while s < GC:
        cpart = jnp.where(jnp.logical_and(blk(2 * s), jnp.logical_not(blk(s))), m, 0.0)
        p = p - _bmm(p, _bmm(cpart, p))
        s *= 2
    uw = _bmm(p, rhs)
    u_ref[0, 0] = uw[:, :, :HD].astype(BF16)
    w_ref[0, 0] = uw[:, :, HD:].astype(BF16)


def _gdn_prep_call(qkv, ab, alog_row, dtb_row):
    b, t, _ = qkv.shape
    nc = t // GC
    blk = lambda c, rev: pl.BlockSpec((1, GC, GW), (lambda bi, n: (bi, nc - 1 - n, c)) if rev
                                      else (lambda bi, n: (bi, n, c)))
    abblk = lambda rev: pl.BlockSpec((1, GC, LANES), (lambda bi, n: (bi, nc - 1 - n, 0)) if rev
                                     else (lambda bi, n: (bi, n, 0)))
    unit = pl.BlockSpec((1, 1, NU, GC, HD), lambda bi, n: (bi, n, 0, 0, 0))
    ushape = jax.ShapeDtypeStruct((b, nc, NU, GC, HD), BF16)
    return pl.pallas_call(
        _gdn_prep_kernel,
        grid=(b, nc),
        in_specs=[blk(0, False), blk(1, False), blk(2, False), abblk(False),
                  blk(0, True), blk(1, True), blk(2, True), abblk(True),
                  pl.BlockSpec((1, LANES), lambda bi, n: (0, 0)),
                  pl.BlockSpec((1, LANES), lambda bi, n: (0, 0))],
        out_specs=[unit, unit, unit, unit, unit,
                   pl.BlockSpec((1, 1, NU, 8, LANES), lambda bi, n: (bi, n, 0, 0, 0))],
        out_shape=[ushape, ushape, ushape, ushape, ushape,
                   jax.ShapeDtypeStruct((b, nc, NU, 8, LANES), F32)],
        compiler_params=_cparams(("parallel", "parallel")),
        name="gdn_prep",
    )(qkv, qkv, qkv, ab, qkv, qkv, qkv, ab, alog_row, dtb_row)


def _gdn_scan_kernel(u_ref, w_ref, qd_ref, kdt_ref, qk_ref, gt_ref, s0_ref, of_ref, ob_ref, sout_ref, s_scr):
    n = pl.program_id(1)

    @pl.when(n == 0)
    def _():
        s_scr[...] = s0_ref[0]

    for un in range(NU):
        s = s_scr[un]
        sb = s.astype(BF16)
        r = _dot(jnp.concatenate([w_ref[0, 0, un], qd_ref[0, 0, un]], axis=0), sb)
        v_new = u_ref[0, 0, un].astype(F32) - r[:GC]
        vb = v_new.astype(BF16)
        o = r[GC:] + _dot(qk_ref[0, 0, un], vb)
        s_scr[un] = s * gt_ref[0, 0, un][0:1, :] + _dot(kdt_ref[0, 0, un], vb)
        h = un % NH
        if un < NH:
            of_ref[0, :, h * HD:(h + 1) * HD] = o
        else:
            ob_ref[0, :, h * HD:(h + 1) * HD] = o

    @pl.when(n == pl.num_programs(1) - 1)
    def _():
        sout_ref[0] = s_scr[...]


def _gdn_scan_call(u, w, qd, kdt, qk, gt, s0):
    b, nc = u.shape[0], u.shape[1]
    unit = pl.BlockSpec((1, 1, NU, GC, HD), lambda bi, n: (bi, n, 0, 0, 0))
    sblk = pl.BlockSpec((1, NU, HD, HD), lambda bi, n: (bi, 0, 0, 0))
    return pl.pallas_call(
        _gdn_scan_kernel,
        grid=(b, nc),
        in_specs=[unit, unit, unit, unit, unit,
                  pl.BlockSpec((1, 1, NU, 8, LANES), lambda bi, n: (bi, n, 0, 0, 0)), sblk],
        out_specs=[pl.BlockSpec((1, GC, GW), lambda bi, n: (bi, n, 0)),
                   pl.BlockSpec((1, GC, GW), lambda bi, n: (bi, nc - 1 - n, 0)),
                   sblk],
        out_shape=[jax.ShapeDtypeStruct((b, nc * GC, GW), F32),
                   jax.ShapeDtypeStruct((b, nc * GC, GW), F32),
                   jax.ShapeDtypeStruct((b, NU, HD, HD), F32)],
        scratch_shapes=[pltpu.VMEM((NU, HD, HD), F32)],
        compiler_params=_cparams(("parallel", "arbitrary")),
        name="gdn_scan",
    )(u, w, qd, kdt, qk, gt, s0)


def _dft_tables():
    r = GRID_W
    a = 2.0 * np.pi * np.outer(np.arange(r), np.arange(r)) / r
    c64, s64 = np.cos(a), np.sin(a)
    ac = 2.0 * np.pi * np.outer(np.arange(FGD), np.arange(FGD)) / FGD
    cc, sc = np.cos(ac), np.sin(ac)
    at = 2.0 * np.pi * np.outer(np.arange(r), np.arange(r)) / (r * r)
    row_fwd = np.concatenate([c64, -s64], axis=0)
    chan = np.concatenate([cc, -sc], axis=1)
    col_re = np.concatenate([c64, s64], axis=1)
    as_bf16 = lambda t: jnp.asarray(t, F32).astype(BF16)
    return (as_bf16(row_fwd), as_bf16(chan), as_bf16(col_re),
            jnp.asarray(np.cos(at), F32), jnp.asarray(np.sin(at), F32))


def _fourier_a_kernel(x_ref, rowf_ref, chan_ref, twc_ref, tws_ref, o_ref):
    xr = _dot(rowf_ref[...], x_ref[0])
    xb = xr.astype(BF16)
    tc = twc_ref[0]
    ts = tws_ref[0]
    r = GRID_W
    for g in range(FG):
        y = _dot(xb[:, g * FGD:(g + 1) * FGD], chan_ref[...])
        vr = y[:r, :FGD] - y[r:, FGD:]
        vi = y[r:, :FGD] + y[:r, FGD:]
        o_ref[0, :, g * FGD:(g + 1) * FGD] = (vr * tc + vi * ts).astype(BF16)
        o_ref[0, :, FW + g * FGD:FW + (g + 1) * FGD] = (vi * tc - vr * ts).astype(BF16)


def _fourier_b_kernel(v_ref, colre_ref, o_ref):
    v = v_ref[0]
    st = jnp.concatenate([v[:, :FW], v[:, FW:]], axis=0)
    y = _dot(colre_ref[...], st)
    o_ref[0] = (y * (1.0 / math.sqrt(SEQ_N * FGD))).astype(o_ref.dtype)


def _fourier_call(p_main):
    b = p_main.shape[0]
    r = GRID_W
    pc = p_main.shape[2]
    rowf, chan, colre, twc, tws = _dft_tables()
    twc = jnp.broadcast_to(twc[:, :, None], (r, r, LANES))
    tws = jnp.broadcast_to(tws[:, :, None], (r, r, LANES))
    x3 = p_main.reshape(b, r, r * pc)
    nblk = pc // FW
    va = pl.pallas_call(
        _fourier_a_kernel,
        grid=(b, r),
        in_specs=[pl.BlockSpec((1, r, FW), lambda bi, n1: (bi, 0, n1 * nblk)),
                  pl.BlockSpec((2 * r, r), lambda bi, n1: (0, 0)),
                  pl.BlockSpec((FGD, 2 * FGD), lambda bi, n1: (0, 0)),
                  pl.BlockSpec((1, r, LANES), lambda bi, n1: (n1, 0, 0)),
                  pl.BlockSpec((1, r, LANES), lambda bi, n1: (n1, 0, 0))],
        out_specs=pl.BlockSpec((1, r, 2 * FW), lambda bi, n1: (bi, 0, n1)),
        out_shape=jax.ShapeDtypeStruct((b, r, r * 2 * FW), BF16),
        compiler_params=_cparams(("parallel", "parallel")),
        name="fourier_a",
    )(x3, rowf, chan, twc, tws)
    vb = va.reshape(b, r * r, 2 * FW)
    y = pl.pallas_call(
        _fourier_b_kernel,
        grid=(b, r),
        in_specs=[pl.BlockSpec((1, r, 2 * FW), lambda bi, k2: (bi, k2, 0)),
                  pl.BlockSpec((r, 2 * r), lambda bi, k2: (0, 0))],
        out_specs=pl.BlockSpec((1, r, FW), lambda bi, k2: (bi, 0, k2)),
        out_shape=jax.ShapeDtypeStruct((b, r, r * FW), BF16),
        compiler_params=_cparams(("parallel", "parallel")),
        name="fourier_b",
    )(vb, colre)
    return y.reshape(b, r * r, FW)


def _outproj_kernel(yf_ref, of_ref, ob_ref, z_ref, x_ref, wout_ref, gnw_ref, gt1_ref, g2_ref, sh2_ref, sc2_ref,
                    wr_ref, x1_ref, hx_ref, lg_ref):
    o = of_ref[0] + ob_ref[0]
    z = z_ref[0].astype(F32)
    parts = [yf_ref[0]]
    for h in range(NH):
        oh = o[:, h * HD:(h + 1) * HD]
        ms = jnp.mean(oh * oh, axis=-1, keepdims=True)
        on = oh * lax.rsqrt(ms + EPS) * gnw_ref[...]
        parts.append((on * _silu(z[:, h * HD:(h + 1) * HD])).astype(BF16))
    mix = jnp.concatenate(parts, axis=1)
    x1 = x_ref[0] + gt1_ref[0] * _dot(mix, wout_ref[...])
    x1_ref[0] = x1
    ms = jnp.mean(x1 * x1, axis=-1, keepdims=True)
    hx = x1 * lax.rsqrt(ms + EPS) * g2_ref[...] * (1.0 + sc2_ref[0]) + sh2_ref[0]
    hx_ref[0] = hx
    lg = jnp.dot(hx, wr_ref[...], preferred_element_type=F32, precision=lax.Precision.HIGHEST)
    lg_ref[0] = lg.T[:NE, :]


def _outproj_call(yf, o_f, o_b, p_main, x, w_out, gnw, gt1, g2, sh2, sc2, w_r):
    b, t, _ = x.shape
    tm = 256
    zcol = p_main.shape[2] // GW - 1
    row = lambda c: pl.BlockSpec((1, tm, c), lambda bi, i: (bi, i, 0))
    vec = pl.BlockSpec((1, 1, D), lambda bi, i: (bi, 0, 0))
    return pl.pallas_call(
        _outproj_kernel,
        grid=(b, t // tm),
        in_specs=[row(FW), row(GW), row(GW),
                  pl.BlockSpec((1, tm, GW), lambda bi, i: (bi, i, zcol)),
                  row(D),
                  pl.BlockSpec((D, D), lambda bi, i: (0, 0)),
                  pl.BlockSpec((1, HD), lambda bi, i: (0, 0)),
                  vec,
                  pl.BlockSpec((1, D), lambda bi, i: (0, 0)),
                  vec, vec,
                  pl.BlockSpec((D, LANES), lambda bi, i: (0, 0))],
        out_specs=[row(D), row(D), pl.BlockSpec((1, NE, tm), lambda bi, i: (bi, 0, i))],
        out_shape=[jax.ShapeDtypeStruct((b, t, D), F32),
                   jax.ShapeDtypeStruct((b, t, D), F32),
                   jax.ShapeDtypeStruct((b, NE, t), F32)],
        compiler_params=_cparams(("parallel", "parallel")),
        name="outproj",
    )(yf, o_f, o_b, p_main, x, w_out, gnw, gt1, g2, sh2, sc2, w_r)


def _lane_cumsum(x):
    n = x.shape[-1]
    lane = lax.broadcasted_iota(I32, x.shape, x.ndim - 1)
    sh = 1
    while sh < n:
        x = x + jnp.where(lane >= sh, pltpu.roll(x, shift=sh, axis=x.ndim - 1), 0.0)
        sh *= 2
    return x


def _topk_kernel(lg_ref, idx_ref, val_ref, rsel_scr, aff_scr, *, cap):
    lg = lg_ref[0]
    t = lg.shape[1]
    mx = jnp.max(lg, axis=0, keepdims=True)
    ex = jnp.exp(lg - mx)
    aff = ex / jnp.sum(ex, axis=0, keepdims=True)

    def search(i, cur):
        cand = cur | (1 << (30 - i))
        cnt = jnp.sum((aff >= pltpu.bitcast(cand, F32)).astype(F32), axis=1, keepdims=True)
        return jnp.where(cnt >= cap, cand, cur)

    thr = pltpu.bitcast(lax.fori_loop(0, 31, search, jnp.zeros((NE, 1), I32)), F32)
    gt = aff > thr
    eq = aff == thr
    n_gt = jnp.sum(gt.astype(F32), axis=1, keepdims=True)
    eq_rank = _lane_cumsum(eq.astype(F32))
    sel = jnp.logical_or(gt, jnp.logical_and(eq, eq_rank <= cap - n_gt))
    rank = _lane_cumsum(sel.astype(F32))
    rsel_scr[...] = jnp.where(sel, rank, 0.0)
    aff_scr[...] = aff
    idx_ref[0] = jnp.zeros((cap, LANES), I32)
    val_ref[0] = jnp.zeros((cap, LANES), F32)
    tpos = lax.broadcasted_iota(I32, (1, t), 1).astype(F32)
    sblk = 64
    lane = lax.broadcasted_iota(I32, (sblk, LANES), 1)

    def per_expert(e, carry):
        rrow = rsel_scr[pl.ds(e, 1), :]
        arow = aff_scr[pl.ds(e, 1), :]
        for sb in range(cap // sblk):
            slot = (lax.broadcasted_iota(I32, (sblk, 1), 0) + (sb * sblk + 1)).astype(F32)
            hit = rrow == slot
            ic = jnp.sum(jnp.where(hit, tpos, 0.0), axis=1, keepdims=True)
            vc = jnp.sum(jnp.where(hit, arow, 0.0), axis=1, keepdims=True)
            rows = slice(sb * sblk, (sb + 1) * sblk)
            idx_ref[0, rows, :] = jnp.where(lane == e, ic.astype(I32), idx_ref[0, rows, :])
            val_ref[0, rows, :] = jnp.where(lane == e, vc, val_ref[0, rows, :])
        return carry

    lax.fori_loop(0, NE, per_expert, 0)


def _topk_call(lg_t, cap):
    b, _, t = lg_t.shape
    return pl.pallas_call(
        functools.partial(_topk_kernel, cap=cap),
        grid=(b,),
        in_specs=[pl.BlockSpec((1, NE, t), lambda bi: (bi, 0, 0))],
        out_specs=[pl.BlockSpec((1, cap, LANES), lambda bi: (bi, 0, 0)),
                   pl.BlockSpec((1, cap, LANES), lambda bi: (bi, 0, 0))],
        out_shape=[jax.ShapeDtypeStruct((b, cap, LANES), I32),
                   jax.ShapeDtypeStruct((b, cap, LANES), F32)],
        scratch_shapes=[pltpu.VMEM((NE, t), F32), pltpu.VMEM((NE, t), F32)],
        compiler_params=_cparams(("parallel",)),
        name="topk",
    )(lg_t)


MOE_TF = 256
MOE_TN = 256
MOE_MC = 512
MOE_GC = 256
MOE_NF = EFF // MOE_TF
MOE_NN = D // MOE_TN


def _moe_ffn_kernel(idx_ref, hx_hbm, val_ref, gt2_ref, wg_ref, wu_ref, wd_ref, y_ref,
                    xg_scr, hid_scr, stg_scr, vcol_scr, gsem, *, rows):
    e = pl.program_id(0)
    s = pl.program_id(1)
    ne = pl.num_programs(0)
    cur = e % 2
    nchunk = rows // MOE_GC
    nxt = jnp.minimum(e + 1, ne - 1)

    def issue(expert, c):
        base = expert * rows + c * MOE_GC
        for r in range(MOE_GC):
            pltpu.make_async_copy(hx_hbm.at[pl.ds(idx_ref[base + r], 1), :],
                                  stg_scr.at[pl.ds(r, 1), :], gsem).start()

    def wait_chunk():
        pltpu.make_async_copy(hx_hbm.at[pl.ds(0, MOE_GC), :], stg_scr, gsem).wait()

    def land(c, buf):
        wait_chunk()
        r0 = pl.multiple_of(c * MOE_GC, MOE_GC)
        xg_scr[buf, pl.ds(r0, MOE_GC), :] = stg_scr[...].astype(BF16)

    @pl.when(jnp.logical_and(e == 0, s == 0))
    def _():
        for c in range(nchunk):
            issue(0, c)
            land(c, 0)

    @pl.when(jnp.logical_and(e > 0, s == 0))
    def _():
        land(nchunk - 1, cur)

    @pl.when(s < MOE_NF)
    def _():
        xs = xg_scr[cur]
        hid = (_silu(_dot(xs, wg_ref[0].astype(BF16))) * _dot(xs, wu_ref[0].astype(BF16))).astype(BF16)
        hid_scr[:, pl.ds(pl.multiple_of(s * MOE_TF, MOE_TF), MOE_TF)] = hid

    @pl.when(s == MOE_NF)
    def _():
        vcol_scr[...] = jnp.broadcast_to(val_ref[0], (LANES, rows)).T

    @pl.when(s > MOE_NF)
    def _():
        land(s - MOE_NF - 1, 1 - cur)

    @pl.when(s >= MOE_NF)
    def _():
        y = _dot(hid_scr[...], wd_ref[0].astype(BF16))
        for mc in range(rows // MOE_MC):
            sl = slice(mc * MOE_MC, (mc + 1) * MOE_MC)
            y_ref[0, sl, :] = y[sl] * vcol_scr[sl, 0:1] * gt2_ref[mc, 0]
        issue(nxt, s - MOE_NF)

    @pl.when(jnp.logical_and(e == ne - 1, s == pl.num_programs(1) - 1))
    def _():
        wait_chunk()


def _moe_ffn_call(idx_flat, hx2, vals, gt2, w_gate, w_up, w_down, cap):
    rows = vals.shape[2]
    assert rows // MOE_MC == gt2.shape[0] and cap == MOE_MC and rows // MOE_GC == MOE_NN
    gt2t = gt2.reshape(gt2.shape[0], MOE_NN, 1, MOE_TN)
    ph1 = lambda s: jnp.minimum(s, MOE_NF - 1)
    ph2 = lambda s: jnp.maximum(s - MOE_NF, 0)
    grid_spec = pltpu.PrefetchScalarGridSpec(
        num_scalar_prefetch=1,
        grid=(NE, MOE_NF + MOE_NN),
        in_specs=[pl.BlockSpec(memory_space=pl.ANY),
                  pl.BlockSpec((1, 1, rows), lambda e, s, idx: (e, 0, 0)),
                  pl.BlockSpec((gt2.shape[0], 1, 1, MOE_TN), lambda e, s, idx: (0, ph2(s), 0, 0)),
                  pl.BlockSpec((1, D, MOE_TF), lambda e, s, idx: (e, 0, ph1(s))),
                  pl.BlockSpec((1, D, MOE_TF), lambda e, s, idx: (e, 0, ph1(s))),
                  pl.BlockSpec((1, EFF, MOE_TN), lambda e, s, idx: (e, 0, ph2(s)))],
        out_specs=pl.BlockSpec((1, rows, MOE_TN), lambda e, s, idx: (e, 0, ph2(s))),
        scratch_shapes=[pltpu.VMEM((2, rows, D), BF16),
                        pltpu.VMEM((rows, EFF), BF16),
                        pltpu.VMEM((MOE_GC, D), F32),
                        pltpu.VMEM((rows, LANES), F32),
                        pltpu.SemaphoreType.DMA(())],
    )
    return pl.pallas_call(
        functools.partial(_moe_ffn_kernel, rows=rows),
        grid_spec=grid_spec,
        out_shape=jax.ShapeDtypeStruct((NE, rows, D), F32),
        compiler_params=_cparams(("arbitrary", "arbitrary")),
        name="moe_ffn",
    )(idx_flat, hx2, vals, gt2t, w_gate, w_up, w_down)


MOE_SC = 256


def _moe_scatter_kernel(idx_ref, y_ref, res_hbm, out_hbm, buf_scr, rsem, wsem, *, rows):
    del res_hbm
    e = pl.program_id(0)
    c = pl.program_id(1)
    nc = pl.num_programs(1)
    slot = c % 2

    def reads(chunk, sl):
        base = e * rows + chunk * MOE_SC
        for r in range(MOE_SC):
            pltpu.make_async_copy(out_hbm.at[pl.ds(idx_ref[base + r], 1), :],
                                  buf_scr.at[sl, pl.ds(r, 1), :], rsem.at[sl]).start()

    def writes(chunk, sl):
        base = e * rows + chunk * MOE_SC
        for r in range(MOE_SC):
            pltpu.make_async_copy(buf_scr.at[sl, pl.ds(r, 1), :],
                                  out_hbm.at[pl.ds(idx_ref[base + r], 1), :], wsem.at[sl]).start()

    def wait_all(sem, sl):
        pltpu.make_async_copy(out_hbm.at[pl.ds(0, MOE_SC), :], buf_scr.at[sl], sem.at[sl]).wait()

    @pl.when(c == 0)
    def _():
        reads(c, slot)

    wait_all(rsem, slot)

    @pl.when(c + 1 < nc)
    def _():
        @pl.when(c >= 1)
        def _():
            wait_all(wsem, 1 - slot)
        reads(c + 1, 1 - slot)

    buf_scr[slot] = buf_scr[slot] + y_ref[0]
    writes(c, slot)

    @pl.when(c == nc - 1)
    def _():
        wait_all(wsem, 1 - slot)
        wait_all(wsem, slot)


def _moe_scatter_call(idx_flat, y, x1):
    rows = y.shape[1]
    grid_spec = pltpu.PrefetchScalarGridSpec(
        num_scalar_prefetch=1,
        grid=(NE, rows // MOE_SC),
        in_specs=[pl.BlockSpec((1, MOE_SC, D), lambda e, c, idx: (e, c, 0)),
                  pl.BlockSpec(memory_space=pl.ANY)],
        out_specs=pl.BlockSpec(memory_space=pl.ANY),
        scratch_shapes=[pltpu.VMEM((2, MOE_SC, D), F32),
                        pltpu.SemaphoreType.DMA((2,)),
                        pltpu.SemaphoreType.DMA((2,))],
    )
    return pl.pallas_call(
        functools.partial(_moe_scatter_kernel, rows=rows),
        grid_spec=grid_spec,
        out_shape=jax.ShapeDtypeStruct(x1.shape, F32),
        input_output_aliases={2: 0},
        compiler_params=_cparams(("arbitrary", "arbitrary")),
        name="moe_scatter",
    )(idx_flat, y, x1)


def _final_kernel(x_ref, g_ref, o_ref):
    x = x_ref[...]
    ms = jnp.mean(x * x, axis=-1, keepdims=True)
    o_ref[...] = x * lax.rsqrt(ms + EPS) * g_ref[...]


def _final_call(x2d, g):
    m = x2d.shape[0]
    tm = 512
    return pl.pallas_call(
        _final_kernel,
        grid=(m // tm,),
        in_specs=[pl.BlockSpec((tm, D), lambda i: (i, 0)), pl.BlockSpec((1, D), lambda i: (0, 0))],
        out_specs=pl.BlockSpec((tm, D), lambda i: (i, 0)),
        out_shape=jax.ShapeDtypeStruct(x2d.shape, F32),
        compiler_params=_cparams(("parallel",)),
        name="final_norm",
    )(x2d, g)


def kernel(x, c, ctx, c_ctx, w_mod, b_mod, norm1_g, norm2_g, w_in, conv_w, a_log, dt_bias, gdn_norm_w, w_out,
           w_router, w_gate, w_up, w_down, norm_f):
    b, n, _ = x.shape
    nctx = ctx.shape[1]
    cap = 2 * n // NE
    i = 0

    cc = jnp.concatenate([c, c_ctx[None, :], jnp.zeros((8 - b - 1, D), F32)], axis=0)
    mod = _mod_call(cc, w_mod[i], b_mod[i][None, :])
    sh1, sc1, gt1, sh2, sc2, gt2 = [mod[:b, k * D:(k + 1) * D][:, None, :] for k in range(6)]
    sh1c = jnp.broadcast_to(mod[b:b + 1, 0:D][:, None, :], (b, 1, D))
    sc1c = jnp.broadcast_to(mod[b:b + 1, D:2 * D][:, None, :], (b, 1, D))

    nmain = FW + 4 * GW
    w_main = w_in[i][:, :nmain].astype(BF16)
    w_ab = jnp.pad(w_in[i][:, nmain:], ((0, 0), (0, LANES - 2 * N_DIR * NH))).astype(BF16)
    conv_w8 = jnp.pad(conv_w[i], ((0, 8 - CONV_K), (0, 0)))
    alog_row = jnp.pad(a_log[i].reshape(1, NU), ((0, 0), (0, LANES - NU)))
    dtb_row = jnp.pad(dt_bias[i].reshape(1, NU), ((0, 0), (0, LANES - NU)))
    g1 = norm1_g[i][None, :]

    p_ctx, ab_ctx = _inproj_call(ctx, g1, sh1c, sc1c, w_main, w_ab, nctx, 1, 3)
    qkv_ctx = _conv_call(p_ctx, conv_w8, nctx, nctx, 0)
    prep_ctx = _gdn_prep_call(qkv_ctx, ab_ctx, alog_row, dtb_row)
    s_zero = jnp.zeros((b, NU, HD, HD), F32)
    _, _, s_ctx = _gdn_scan_call(*prep_ctx, s_zero)

    p_main, ab = _inproj_call(x, g1, sh1, sc1, w_main, w_ab, 512, 0, 5)
    qkv = _conv_call(p_main, conv_w8, 512, GRID_W, 1)
    prep = _gdn_prep_call(qkv, ab, alog_row, dtb_row)
    o_f, o_b, _ = _gdn_scan_call(*prep, s_ctx)
    yf = _fourier_call(p_main)

    w_r = jnp.pad(w_router[i], ((0, 0), (0, LANES - NE)))
    x1, hx2, lg_t = _outproj_call(yf, o_f, o_b, p_main, x, w_out[i].astype(BF16), gdn_norm_w[i][None, :], gt1,
                                  norm2_g[i][None, :], sh2, sc2, w_r)

    idx_c, val_c = _topk_call(lg_t, cap)
    idx = jnp.transpose(idx_c[:, :, :NE], (2, 0, 1))
    val = jnp.transpose(val_c[:, :, :NE], (2, 0, 1))
    idx_flat = (idx + (jnp.arange(b, dtype=I32) * n)[None, :, None]).reshape(NE * b * cap)
    vals = val.reshape(NE, 1, b * cap)

    y = _moe_ffn_call(idx_flat, hx2.reshape(b * n, D), vals, gt2, w_gate[i], w_up[i], w_down[i], cap)
    out = _moe_scatter_call(idx_flat, y, x1.reshape(b * n, D))
    return _final_call(out, norm_f[None, :]).reshape(b, n, D)
```

```python
import functools
import math

import numpy as np
import jax
import jax.numpy as jnp
from jax import lax
from jax.experimental import pallas as pl
from jax.experimental.pallas import tpu as pltpu

F32 = jnp.float32
BF16 = jnp.bfloat16
I32 = jnp.int32

D = 2048
SEQ_N = 4096
GRID_W = 64
FW = 1024
FG = 8
FGD = 128
GW = 1024
NH = 8
HD = 128
CONV_K = 5
N_DIR = 2
NE = 16
EFF = 1536
EPS = 1e-6
GC = 128
NU = N_DIR * NH
LANES = 128
VMEM_LIMIT = 56 * 1024 * 1024


def _sigmoid(x):
    return 1.0 / (1.0 + jnp.exp(-x))


def _silu(x):
    return x * _sigmoid(x)


def _softplus(x):
    return jnp.maximum(x, 0.0) + jnp.log(1.0 + jnp.exp(-jnp.abs(x)))


def _dot(a, b):
    return jnp.dot(a, b, preferred_element_type=F32)


def _dot_nt(a, b):
    return lax.dot_general(a, b, (((1,), (1,)), ((), ())), preferred_element_type=F32)


def _bmm(a, b):
    return jnp.einsum('uij,ujk->uik', a.astype(BF16), b.astype(BF16), preferred_element_type=F32)


def _bmm_nt(a, b):
    return jnp.einsum('uik,ujk->uij', a, b, preferred_element_type=F32)


def _cparams(sem, vmem=VMEM_LIMIT):
    return pltpu.CompilerParams(dimension_semantics=sem, vmem_limit_bytes=vmem)


def _mod_kernel(c_ref, w_ref, b_ref, o_ref):
    s = _silu(c_ref[...]).astype(BF16)
    o_ref[...] = _dot(s, w_ref[...].astype(BF16)) + b_ref[...]


def _mod_call(cc, w_mod, b_mod):
    tn = 1024
    n = w_mod.shape[1]
    return pl.pallas_call(
        _mod_kernel,
        grid=(n // tn,),
        in_specs=[pl.BlockSpec((8, D), lambda j: (0, 0)),
                  pl.BlockSpec((D, tn), lambda j: (0, j)),
                  pl.BlockSpec((1, tn), lambda j: (0, j))],
        out_specs=pl.BlockSpec((8, tn), lambda j: (0, j)),
        out_shape=jax.ShapeDtypeStruct((8, n), F32),
        compiler_params=_cparams(("parallel",)),
        name="mod",
    )(cc, w_mod, b_mod)


def _inproj_kernel(x_ref, g_ref, sh_ref, sc_ref, w_ref, wab_ref, o_ref, ab_ref, h_scr):
    @pl.when(pl.program_id(2) == 0)
    def _():
        x = x_ref[0]
        ms = jnp.mean(x * x, axis=-1, keepdims=True)
        y = x * lax.rsqrt(ms + EPS) * g_ref[...]
        h = (y * (1.0 + sc_ref[0]) + sh_ref[0]).astype(BF16)
        h_scr[...] = h
        ab_ref[0] = _dot(h, wab_ref[...])

    o_ref[0] = _dot(h_scr[...], w_ref[...]).astype(o_ref.dtype)


def _inproj_call(x, g, sh, sc, w_main, w_ab, tm, col0, ncol):
    b, t, _ = x.shape
    tn = 1024
    return pl.pallas_call(
        _inproj_kernel,
        grid=(b, t // tm, ncol),
        in_specs=[pl.BlockSpec((1, tm, D), lambda bi, i, j: (bi, i, 0)),
                  pl.BlockSpec((1, D), lambda bi, i, j: (0, 0)),
                  pl.BlockSpec((1, 1, D), lambda bi, i, j: (bi, 0, 0)),
                  pl.BlockSpec((1, 1, D), lambda bi, i, j: (bi, 0, 0)),
                  pl.BlockSpec((D, tn), lambda bi, i, j: (0, j + col0)),
                  pl.BlockSpec((D, LANES), lambda bi, i, j: (0, 0))],
        out_specs=[pl.BlockSpec((1, tm, tn), lambda bi, i, j: (bi, i, j)),
                   pl.BlockSpec((1, tm, LANES), lambda bi, i, j: (bi, i, 0))],
        out_shape=[jax.ShapeDtypeStruct((b, t, ncol * tn), BF16),
                   jax.ShapeDtypeStruct((b, t, LANES), F32)],
        scratch_shapes=[pltpu.VMEM((tm, D), BF16)],
        compiler_params=_cparams(("parallel", "parallel", "arbitrary")),
        name="inproj",
    )(x, g, sh, sc, w_main, w_ab)


def _conv_kernel(x_ref, w_ref, o_ref, *, row_len):
    j = pl.program_id(2)
    half = CONV_K // 2
    rr = lax.broadcasted_iota(I32, (row_len, row_len), 0)
    cc = lax.broadcasted_iota(I32, (row_len, row_len), 1)
    taps = [t for t in range(CONV_K) if t != half]
    shifts = jnp.concatenate([(cc == rr + (t - half)).astype(BF16) for t in taps], axis=0)
    qscale = jnp.where(j == 0, HD ** -0.5, 1.0).astype(F32)

    def group(g, carry):
        r0 = pl.multiple_of(g * row_len, row_len)
        xb = x_ref[0, pl.ds(r0, row_len), :]
        sh = _dot(shifts, xb)
        for h in range(NH):
            cols = slice(h * HD, (h + 1) * HD)
            acc = xb[:, cols].astype(F32) * w_ref[half:half + 1, cols]
            for ti, tap in enumerate(taps):
                acc = acc + sh[ti * row_len:(ti + 1) * row_len, cols] * w_ref[tap:tap + 1, cols]
            y = _silu(acc)
            ss = jnp.sum(y * y, axis=-1, keepdims=True)
            nrm = jnp.where(j < 2, lax.rsqrt(ss + EPS) * qscale, 1.0)
            o_ref[0, pl.ds(r0, row_len), cols] = (y * nrm).astype(o_ref.dtype)
        return carry

    lax.fori_loop(0, x_ref.shape[1] // row_len, group, 0, unroll=True)


def _conv_call(p, conv_w8, tt, row_len, col0):
    b, t, _ = p.shape
    return pl.pallas_call(
        functools.partial(_conv_kernel, row_len=row_len),
        grid=(b, t // tt, 3),
        in_specs=[pl.BlockSpec((1, tt, GW), lambda bi, i, j: (bi, i, j + col0)),
                  pl.BlockSpec((8, GW), lambda bi, i, j: (0, j))],
        out_specs=pl.BlockSpec((1, tt, GW), lambda bi, i, j: (bi, i, j)),
        out_shape=jax.ShapeDtypeStruct((b, t, 3 * GW), BF16),
        compiler_params=_cparams(("parallel", "parallel", "arbitrary")),
        name="conv",
    )(p, conv_w8)


def _gdn_prep_kernel(qf_ref, kf_ref, vf_ref, abf_ref, qb_ref, kb_ref, vb_ref, abb_ref, alog_ref, dtb_ref,
                     u_ref, w_ref, qd_ref, kdt_ref, qk_ref, gt_ref):
    row = lax.broadcasted_iota(I32, (GC, GC), 0)
    col = lax.broadcasted_iota(I32, (GC, GC), 1)
    eye = (row == col).astype(F32)
    m_parts, rhs_parts = [], []
    for d in range(N_DIR):
        q_ref, k_ref, v_ref, ab_ref = ((qf_ref, kf_ref, vf_ref, abf_ref) if d == 0
                                       else (qb_ref, kb_ref, vb_ref, abb_ref))
        incl = (col <= row) if d == 0 else (col >= row)
        strict = (col < row) if d == 0 else (col > row)
        ab = ab_ref[0]
        g_all = -jnp.exp(alog_ref[...]) * _softplus(ab + dtb_ref[...])
        beta_all = _sigmoid(ab)
        gc_all = jnp.dot(incl.astype(F32), g_all, preferred_element_type=F32,
                         precision=lax.Precision.HIGHEST)
        gc_t = gc_all.T
        last = GC - 1 if d == 0 else 0
        c0 = d * NH
        cb = N_DIR * NH + d * NH
        heads = range(NH)
        gcol = jnp.stack([gc_all[:, c0 + h:c0 + h + 1] for h in heads])
        grow = jnp.stack([gc_t[c0 + h:c0 + h + 1, :] for h in heads])
        glast = jnp.stack([gc_all[last:last + 1, c0 + h:c0 + h + 1] for h in heads])
        beta = jnp.stack([beta_all[:, cb + h:cb + h + 1] for h in heads])
        q = jnp.stack([q_ref[0, :, h * HD:(h + 1) * HD] for h in heads]).astype(F32)
        k = jnp.stack([k_ref[0, :, h * HD:(h + 1) * HD] for h in heads])
        kf = k.astype(F32)
        v = jnp.stack([v_ref[0, :, h * HD:(h + 1) * HD] for h in heads]).astype(F32)
        decay = jnp.where(incl[None], jnp.exp(jnp.where(incl[None], gcol - grow, 0.0)), 0.0)
        egc = jnp.exp(gcol)
        kbeta = kf * beta
        a2 = _bmm_nt(jnp.concatenate([kbeta, q], axis=1).astype(BF16), k)
        m_parts.append(jnp.where(strict[None], a2[:, :GC] * decay, 0.0))
        rhs_parts.append(jnp.concatenate([v * beta, kbeta * egc], axis=2).astype(BF16))
        qd_ref[0, 0, c0:c0 + NH] = (q * egc).astype(BF16)
        kdt_ref[0, 0, c0:c0 + NH] = jnp.swapaxes(kf * jnp.exp(glast - gcol), 1, 2).astype(BF16)
        qk_ref[0, 0, c0:c0 + NH] = (a2[:, GC:] * decay).astype(BF16)
        gt_ref[0, 0, c0:c0 + NH] = jnp.broadcast_to(jnp.exp(glast), (NH, 8, LANES))

    m = jnp.concatenate(m_parts, axis=0)
    rhs = jnp.concatenate(rhs_parts, axis=0)
    blk = lambda s: ((row >> int(math.log2(s))) == (col >> int(math.log2(s))))[None]
    mb = jnp.where(blk(8), m, 0.0)
    p = eye[None] - mb
    m2 = _bmm(mb, mb)
    p = p + _bmm(p, m2)
    p = p + _bmm(p, _bmm(m2, m2))
    s = 8
    while s < GC:
        cpart = jnp.where(jnp.logical_and(blk(2 * s), jnp.logical_not(blk(s))), m, 0.0)
        p = p - _bmm(p, _bmm(cpart, p))
        s *= 2
    uw = _bmm(p, rhs)
    u_ref[0, 0] = uw[:, :, :HD].astype(BF16)
    w_ref[0, 0] = uw[:, :, HD:].astype(BF16)


def _gdn_prep_call(qkv, ab, alog_row, dtb_row):
    b, t, _ = qkv.shape
    nc = t // GC
    blk = lambda c, rev: pl.BlockSpec((1, GC, GW), (lambda bi, n: (bi, nc - 1 - n, c)) if rev
                                      else (lambda bi, n: (bi, n, c)))
    abblk = lambda rev: pl.BlockSpec((1, GC, LANES), (lambda bi, n: (bi, nc - 1 - n, 0)) if rev
                                     else (lambda bi, n: (bi, n, 0)))
    unit = pl.BlockSpec((1, 1, NU, GC, HD), lambda bi, n: (bi, n, 0, 0, 0))
    ushape = jax.ShapeDtypeStruct((b, nc, NU, GC, HD), BF16)
    return pl.pallas_call(
        _gdn_prep_kernel,
        grid=(b, nc),
        in_specs=[blk(0, False), blk(1, False), blk(2, False), abblk(False),
                  blk(0, True), blk(1, True), blk(2, True), abblk(True),
                  pl.BlockSpec((1, LANES), lambda bi, n: (0, 0)),
                  pl.BlockSpec((1, LANES), lambda bi, n: (0, 0))],
        out_specs=[unit, unit, unit, unit, unit,
                   pl.BlockSpec((1, 1, NU, 8, LANES), lambda bi, n: (bi, n, 0, 0, 0))],
        out_shape=[ushape, ushape, ushape, ushape, ushape,
                   jax.ShapeDtypeStruct((b, nc, NU, 8, LANES), F32)],
        compiler_params=_cparams(("parallel", "parallel")),
        name="gdn_prep",
    )(qkv, qkv, qkv, ab, qkv, qkv, qkv, ab, alog_row, dtb_row)


def _gdn_scan_kernel(u_ref, w_ref, qd_ref, kdt_ref, qk_ref, gt_ref, s0_ref, of_ref, ob_ref, sout_ref, s_scr):
    n = pl.program_id(1)

    @pl.when(n == 0)
    def _():
        s_scr[...] = s0_ref[0]

    s = s_scr[...]
    r = _bmm(jnp.concatenate([w_ref[0, 0], qd_ref[0, 0]], axis=1), s)
    vb = (u_ref[0, 0].astype(F32) - r[:, :GC]).astype(BF16)
    o = r[:, GC:] + _bmm(qk_ref[0, 0], vb)
    s_scr[...] = s * gt_ref[0, 0][:, 0:1, :] + _bmm(kdt_ref[0, 0], vb)
    for un in range(NU):
        h = un % NH
        if un < NH:
            of_ref[0, :, h * HD:(h + 1) * HD] = o[un]
        else:
            ob_ref[0, :, h * HD:(h + 1) * HD] = o[un]

    @pl.when(n == pl.num_programs(1) - 1)
    def _():
        sout_ref[0] = s_scr[...]


def _gdn_scan_call(u, w, qd, kdt, qk, gt, s0):
    b, nc = u.shape[0], u.shape[1]
    unit = pl.BlockSpec((1, 1, NU, GC, HD), lambda bi, n: (bi, n, 0, 0, 0))
    sblk = pl.BlockSpec((1, NU, HD, HD), lambda bi, n: (bi, 0, 0, 0))
    return pl.pallas_call(
        _gdn_scan_kernel,
        grid=(b, nc),
        in_specs=[unit, unit, unit, unit, unit,
                  pl.BlockSpec((1, 1, NU, 8, LANES), lambda bi, n: (bi, n, 0, 0, 0)), sblk],
        out_specs=[pl.BlockSpec((1, GC, GW), lambda bi, n: (bi, n, 0)),
                   pl.BlockSpec((1, GC, GW), lambda bi, n: (bi, nc - 1 - n, 0)),
                   sblk],
        out_shape=[jax.ShapeDtypeStruct((b, nc * GC, GW), F32),
                   jax.ShapeDtypeStruct((b, nc * GC, GW), F32),
                   jax.ShapeDtypeStruct((b, NU, HD, HD), F32)],
        scratch_shapes=[pltpu.VMEM((NU, HD, HD), F32)],
        compiler_params=_cparams(("parallel", "arbitrary")),
        name="gdn_scan",
    )(u, w, qd, kdt, qk, gt, s0)


def _dft_tables():
    r = GRID_W
    a = 2.0 * np.pi * np.outer(np.arange(r), np.arange(r)) / r
    c64, s64 = np.cos(a), np.sin(a)
    ac = 2.0 * np.pi * np.outer(np.arange(FGD), np.arange(FGD)) / FGD
    cc, sc = np.cos(ac), np.sin(ac)
    at = (2.0 * np.pi * np.outer(np.arange(r), np.arange(r)) / (r * r)).reshape(r * r, 1)
    chan = np.concatenate([cc, -sc], axis=1)
    row_cs = np.concatenate([c64, s64], axis=0)
    col_re = np.concatenate([c64, s64], axis=1)
    as_bf16 = lambda t: jnp.asarray(t, F32).astype(BF16)
    return (as_bf16(chan), as_bf16(row_cs), as_bf16(col_re),
            jnp.asarray(np.cos(at), F32), jnp.asarray(np.sin(at), F32))


def _fourier_kernel(x_ref, chan_ref, rowcs_ref, colre_ref, twc_ref, tws_ref, o_ref, u_scr, v_scr, y_scr):
    r = GRID_W
    u = _dot(x_ref[0], chan_ref[...])
    u_scr[0] = u[:, :FGD]
    u_scr[1] = u[:, FGD:]

    def over_n2(n1, carry):
        x = jnp.concatenate([u_scr[0, pl.ds(n1, r, stride=r), :], u_scr[1, pl.ds(n1, r, stride=r), :]],
                            axis=1).astype(BF16)
        cs = _dot(rowcs_ref[...], x)
        v_scr[0, pl.ds(n1, r, stride=r), :] = cs[:r, :FGD] + cs[r:, FGD:]
        v_scr[1, pl.ds(n1, r, stride=r), :] = cs[:r, FGD:] - cs[r:, :FGD]
        return carry

    lax.fori_loop(0, r, over_n2, 0, unroll=4)

    def over_n1(k2, carry):
        r0 = pl.multiple_of(k2 * r, r)
        vr = v_scr[0, pl.ds(r0, r), :]
        vi = v_scr[1, pl.ds(r0, r), :]
        tc = twc_ref[pl.ds(r0, r), :]
        ts = tws_ref[pl.ds(r0, r), :]
        st = jnp.concatenate([vr * tc + vi * ts, vi * tc - vr * ts], axis=0).astype(BF16)
        y = _dot(colre_ref[...], st)
        y_scr[pl.ds(k2, r, stride=r), :] = y * (1.0 / math.sqrt(SEQ_N * FGD))
        return carry

    lax.fori_loop(0, r, over_n1, 0, unroll=4)
    o_ref[0] = y_scr[...].astype(o_ref.dtype)


def _fourier_call(p_main):
    b, n, _ = p_main.shape
    r = GRID_W
    chan, rowcs, colre, twc, tws = _dft_tables()
    twc = jnp.broadcast_to(twc, (n, LANES))
    tws = jnp.broadcast_to(tws, (n, LANES))
    const = lambda shape: pl.BlockSpec(shape, lambda bi, g: (0, 0))
    return pl.pallas_call(
        _fourier_kernel,
        grid=(b, FG),
        in_specs=[pl.BlockSpec((1, n, FGD), lambda bi, g: (bi, 0, g)),
                  const((FGD, 2 * FGD)), const((2 * r, r)), const((r, 2 * r)),
                  const((n, LANES)), const((n, LANES))],
        out_specs=pl.BlockSpec((1, n, FGD), lambda bi, g: (bi, 0, g)),
        out_shape=jax.ShapeDtypeStruct((b, n, FW), BF16),
        scratch_shapes=[pltpu.VMEM((2, n, FGD), F32), pltpu.VMEM((2, n, FGD), F32), pltpu.VMEM((n, FGD), F32)],
        compiler_params=_cparams(("parallel", "parallel")),
        name="fourier",
    )(p_main, chan, rowcs, colre, twc, tws)


def _outproj_kernel(yf_ref, of_ref, ob_ref, z_ref, x_ref, wout_ref, gnw_ref, gt1_ref, g2_ref, sh2_ref, sc2_ref,
                    wr_ref, x1_ref, hx_ref, lg_ref):
    sub = 128
    for rb in range(x_ref.shape[1] // sub):
        rows = slice(rb * sub, (rb + 1) * sub)
        parts = [yf_ref[0, rows, :]]
        for h in range(NH):
            cols = slice(h * HD, (h + 1) * HD)
            oh = of_ref[0, rows, cols] + ob_ref[0, rows, cols]
            ms = jnp.mean(oh * oh, axis=-1, keepdims=True)
            on = oh * lax.rsqrt(ms + EPS) * gnw_ref[...]
            parts.append((on * _silu(z_ref[0, rows, cols].astype(F32))).astype(BF16))
        mix = jnp.concatenate(parts, axis=1)
        x1 = x_ref[0, rows, :] + gt1_ref[0] * _dot(mix, wout_ref[...])
        x1_ref[0, rows, :] = x1
        ms = jnp.mean(x1 * x1, axis=-1, keepdims=True)
        hx = x1 * lax.rsqrt(ms + EPS) * g2_ref[...] * (1.0 + sc2_ref[0]) + sh2_ref[0]
        hx_ref[0, rows, :] = hx
        hx_hi = hx.astype(BF16)
        hx_lo = (hx - hx_hi.astype(F32)).astype(BF16)
        l2 = _dot(hx_hi, wr_ref[...])
        lg = l2[:, :LANES] + l2[:, LANES:] + _dot(hx_lo, wr_ref[:, :LANES])
        lg_ref[0, :, rows] = lg.T[:NE, :]


def _outproj_call(yf, o_f, o_b, p_main, x, w_out, gnw, gt1, g2, sh2, sc2, w_r):
    b, t, _ = x.shape
    tm = 256
    zcol = p_main.shape[2] // GW - 1
    row = lambda c: pl.BlockSpec((1, tm, c), lambda bi, i: (bi, i, 0))
    vec = pl.BlockSpec((1, 1, D), lambda bi, i: (bi, 0, 0))
    return pl.pallas_call(
        _outproj_kernel,
        grid=(b, t // tm),
        in_specs=[row(FW), row(GW), row(GW),
                  pl.BlockSpec((1, tm, GW), lambda bi, i: (bi, i, zcol)),
                  row(D),
                  pl.BlockSpec((D, D), lambda bi, i: (0, 0)),
                  pl.BlockSpec((1, HD), lambda bi, i: (0, 0)),
                  vec,
                  pl.BlockSpec((1, D), lambda bi, i: (0, 0)),
                  vec, vec,
                  pl.BlockSpec((D, 2 * LANES), lambda bi, i: (0, 0))],
        out_specs=[row(D), row(D), pl.BlockSpec((1, NE, tm), lambda bi, i: (bi, 0, i))],
        out_shape=[jax.ShapeDtypeStruct((b, t, D), F32),
                   jax.ShapeDtypeStruct((b, t, D), F32),
                   jax.ShapeDtypeStruct((b, NE, t), F32)],
        compiler_params=_cparams(("parallel", "parallel")),
        name="outproj",
    )(yf, o_f, o_b, p_main, x, w_out, gnw, gt1, g2, sh2, sc2, w_r)


def _lane_cumsum(x):
    n = x.shape[-1]
    lane = lax.broadcasted_iota(I32, x.shape, x.ndim - 1)
    sh = 1
    while sh < n:
        x = x + jnp.where(lane >= sh, pltpu.roll(x, shift=sh, axis=x.ndim - 1), 0.0)
        sh *= 2
    return x


def _topk_kernel(lg_ref, idx_ref, val_ref, rsel_scr, aff_scr, *, cap):
    lg = lg_ref[0]
    t = lg.shape[1]
    mx = jnp.max(lg, axis=0, keepdims=True)
    ex = jnp.exp(lg - mx)
    aff = ex / jnp.sum(ex, axis=0, keepdims=True)

    def search(i, cur):
        cand = cur | (1 << (30 - i))
        cnt = jnp.sum((aff >= pltpu.bitcast(cand, F32)).astype(F32), axis=1, keepdims=True)
        return jnp.where(cnt >= cap, cand, cur)

    thr = pltpu.bitcast(lax.fori_loop(0, 31, search, jnp.zeros((NE, 1), I32)), F32)
    gt = aff > thr
    eq = aff == thr
    n_gt = jnp.sum(gt.astype(F32), axis=1, keepdims=True)
    eq_rank = _lane_cumsum(eq.astype(F32))
    sel = jnp.logical_or(gt, jnp.logical_and(eq, eq_rank <= cap - n_gt))
    rank = _lane_cumsum(sel.astype(F32))
    rsel_scr[...] = jnp.where(sel, rank, 0.0)
    aff_scr[...] = aff
    idx_ref[0] = jnp.zeros((cap, LANES), I32)
    val_ref[0] = jnp.zeros((cap, LANES), F32)
    tpos = lax.broadcasted_iota(I32, (1, t), 1).astype(F32)
    sblk = 64
    lane = lax.broadcasted_iota(I32, (sblk, LANES), 1)

    def per_expert(e, carry):
        rrow = rsel_scr[pl.ds(e, 1), :]
        arow = aff_scr[pl.ds(e, 1), :]
        for sb in range(cap // sblk):
            slot = (lax.broadcasted_iota(I32, (sblk, 1), 0) + (sb * sblk + 1)).astype(F32)
            hit = rrow == slot
            ic = jnp.sum(jnp.where(hit, tpos, 0.0), axis=1, keepdims=True)
            vc = jnp.sum(jnp.where(hit, arow, 0.0), axis=1, keepdims=True)
            rows = slice(sb * sblk, (sb + 1) * sblk)
            idx_ref[0, rows, :] = jnp.where(lane == e, ic.astype(I32), idx_ref[0, rows, :])
            val_ref[0, rows, :] = jnp.where(lane == e, vc, val_ref[0, rows, :])
        return carry

    lax.fori_loop(0, NE, per_expert, 0)


def _topk_call(lg_t, cap):
    b, _, t = lg_t.shape
    return pl.pallas_call(
        functools.partial(_topk_kernel, cap=cap),
        grid=(b,),
        in_specs=[pl.BlockSpec((1, NE, t), lambda bi: (bi, 0, 0))],
        out_specs=[pl.BlockSpec((1, cap, LANES), lambda bi: (bi, 0, 0)),
                   pl.BlockSpec((1, cap, LANES), lambda bi: (bi, 0, 0))],
        out_shape=[jax.ShapeDtypeStruct((b, cap, LANES), I32),
                   jax.ShapeDtypeStruct((b, cap, LANES), F32)],
        scratch_shapes=[pltpu.VMEM((NE, t), F32), pltpu.VMEM((NE, t), F32)],
        compiler_params=_cparams(("parallel",)),
        name="topk",
    )(lg_t)


MOE_TF = 256
MOE_TN = 256
MOE_MC = 512
MOE_GC = 256
MOE_NF = EFF // MOE_TF
MOE_NN = D // MOE_TN


def _moe_ffn_kernel(idx_ref, hx_hbm, val_ref, gt2_ref, wg_ref, wu_ref, wd_ref, y_ref,
                    xg_scr, hid_scr, stg_scr, vcol_scr, gsem, *, rows):
    e = pl.program_id(0)
    s = pl.program_id(1)
    ne = pl.num_programs(0)
    cur = e % 2
    nchunk = rows // MOE_GC
    nxt = jnp.minimum(e + 1, ne - 1)

    def issue(expert, c):
        base = expert * rows + c * MOE_GC
        for r in range(MOE_GC):
            pltpu.make_async_copy(hx_hbm.at[pl.ds(idx_ref[base + r], 1), :],
                                  stg_scr.at[pl.ds(r, 1), :], gsem).start()

    def wait_chunk():
        pltpu.make_async_copy(hx_hbm.at[pl.ds(0, MOE_GC), :], stg_scr, gsem).wait()

    def land(c, buf):
        wait_chunk()
        r0 = pl.multiple_of(c * MOE_GC, MOE_GC)
        xg_scr[buf, pl.ds(r0, MOE_GC), :] = stg_scr[...].astype(BF16)

    @pl.when(jnp.logical_and(e == 0, s == 0))
    def _():
        for c in range(nchunk):
            issue(0, c)
            land(c, 0)

    @pl.when(jnp.logical_and(e > 0, s == 0))
    def _():
        land(nchunk - 1, cur)

    @pl.when(s < MOE_NF)
    def _():
        xs = xg_scr[cur]
        hid = (_silu(_dot(xs, wg_ref[0].astype(BF16))) * _dot(xs, wu_ref[0].astype(BF16))).astype(BF16)
        hid_scr[:, pl.ds(pl.multiple_of(s * MOE_TF, MOE_TF), MOE_TF)] = hid

    @pl.when(s == MOE_NF)
    def _():
        vcol_scr[...] = jnp.broadcast_to(val_ref[0], (LANES, rows)).T

    @pl.when(s > MOE_NF)
    def _():
        land(s - MOE_NF - 1, 1 - cur)

    @pl.when(s >= MOE_NF)
    def _():
        y = _dot(hid_scr[...], wd_ref[0].astype(BF16))
        for mc in range(rows // MOE_MC):
            sl = slice(mc * MOE_MC, (mc + 1) * MOE_MC)
            y_ref[0, sl, :] = y[sl] * vcol_scr[sl, 0:1] * gt2_ref[mc, 0]
        issue(nxt, s - MOE_NF)

    @pl.when(jnp.logical_and(e == ne - 1, s == pl.num_programs(1) - 1))
    def _():
        wait_chunk()


def _moe_ffn_call(idx_flat, hx2, vals, gt2, w_gate, w_up, w_down, cap):
    rows = vals.shape[2]
    assert rows // MOE_MC == gt2.shape[0] and cap == MOE_MC and rows // MOE_GC == MOE_NN
    gt2t = gt2.reshape(gt2.shape[0], MOE_NN, 1, MOE_TN)
    ph1 = lambda s: jnp.minimum(s, MOE_NF - 1)
    ph2 = lambda s: jnp.maximum(s - MOE_NF, 0)
    grid_spec = pltpu.PrefetchScalarGridSpec(
        num_scalar_prefetch=1,
        grid=(NE, MOE_NF + MOE_NN),
        in_specs=[pl.BlockSpec(memory_space=pl.ANY),
                  pl.BlockSpec((1, 1, rows), lambda e, s, idx: (e, 0, 0)),
                  pl.BlockSpec((gt2.shape[0], 1, 1, MOE_TN), lambda e, s, idx: (0, ph2(s), 0, 0)),
                  pl.BlockSpec((1, D, MOE_TF), lambda e, s, idx: (e, 0, ph1(s))),
                  pl.BlockSpec((1, D, MOE_TF), lambda e, s, idx: (e, 0, ph1(s))),
                  pl.BlockSpec((1, EFF, MOE_TN), lambda e, s, idx: (e, 0, ph2(s)))],
        out_specs=pl.BlockSpec((1, rows, MOE_TN), lambda e, s, idx: (e, 0, ph2(s))),
        scratch_shapes=[pltpu.VMEM((2, rows, D), BF16),
                        pltpu.VMEM((rows, EFF), BF16),
                        pltpu.VMEM((MOE_GC, D), F32),
                        pltpu.VMEM((rows, LANES), F32),
                        pltpu.SemaphoreType.DMA(())],
    )
    return pl.pallas_call(
        functools.partial(_moe_ffn_kernel, rows=rows),
        grid_spec=grid_spec,
        out_shape=jax.ShapeDtypeStruct((NE, rows, D), F32),
        compiler_params=_cparams(("arbitrary", "arbitrary")),
        name="moe_ffn",
    )(idx_flat, hx2, vals, gt2t, w_gate, w_up, w_down)


MOE_SC = 256


def _moe_scatter_kernel(idx_ref, y_ref, res_hbm, out_hbm, buf_scr, rsem, wsem, *, rows):
    del res_hbm
    e = pl.program_id(0)
    c = pl.program_id(1)
    nc = pl.num_programs(1)
    slot = c % 2

    def reads(chunk, sl):
        base = e * rows + chunk * MOE_SC
        for r in range(MOE_SC):
            pltpu.make_async_copy(out_hbm.at[pl.ds(idx_ref[base + r], 1), :],
                                  buf_scr.at[sl, pl.ds(r, 1), :], rsem.at[sl]).start()

    def writes(chunk, sl):
        base = e * rows + chunk * MOE_SC
        for r in range(MOE_SC):
            pltpu.make_async_copy(buf_scr.at[sl, pl.ds(r, 1), :],
                                  out_hbm.at[pl.ds(idx_ref[base + r], 1), :], wsem.at[sl]).start()

    def wait_all(sem, sl):
        pltpu.make_async_copy(out_hbm.at[pl.ds(0, MOE_SC), :], buf_scr.at[sl], sem.at[sl]).wait()

    @pl.when(c == 0)
    def _():
        reads(c, slot)

    wait_all(rsem, slot)

    @pl.when(c + 1 < nc)
    def _():
        @pl.when(c >= 1)
        def _():
            wait_all(wsem, 1 - slot)
        reads(c + 1, 1 - slot)

    buf_scr[slot] = buf_scr[slot] + y_ref[0]
    writes(c, slot)

    @pl.when(c == nc - 1)
    def _():
        wait_all(wsem, 1 - slot)
        wait_all(wsem, slot)


def _moe_scatter_call(idx_flat, y, x1):
    rows = y.shape[1]
    grid_spec = pltpu.PrefetchScalarGridSpec(
        num_scalar_prefetch=1,
        grid=(NE, rows // MOE_SC),
        in_specs=[pl.BlockSpec((1, MOE_SC, D), lambda e, c, idx: (e, c, 0)),
                  pl.BlockSpec(memory_space=pl.ANY)],
        out_specs=pl.BlockSpec(memory_space=pl.ANY),
        scratch_shapes=[pltpu.VMEM((2, MOE_SC, D), F32),
                        pltpu.SemaphoreType.DMA((2,)),
                        pltpu.SemaphoreType.DMA((2,))],
    )
    return pl.pallas_call(
        functools.partial(_moe_scatter_kernel, rows=rows),
        grid_spec=grid_spec,
        out_shape=jax.ShapeDtypeStruct(x1.shape, F32),
        input_output_aliases={2: 0},
        compiler_params=_cparams(("arbitrary", "arbitrary")),
        name="moe_scatter",
    )(idx_flat, y, x1)


def _final_kernel(x_ref, g_ref, o_ref):
    x = x_ref[...]
    ms = jnp.mean(x * x, axis=-1, keepdims=True)
    o_ref[...] = x * lax.rsqrt(ms + EPS) * g_ref[...]


def _final_call(x2d, g):
    m = x2d.shape[0]
    tm = 512
    return pl.pallas_call(
        _final_kernel,
        grid=(m // tm,),
        in_specs=[pl.BlockSpec((tm, D), lambda i: (i, 0)), pl.BlockSpec((1, D), lambda i: (0, 0))],
        out_specs=pl.BlockSpec((tm, D), lambda i: (i, 0)),
        out_shape=jax.ShapeDtypeStruct(x2d.shape, F32),
        compiler_params=_cparams(("parallel",)),
        name="final_norm",
    )(x2d, g)


def kernel(x, c, ctx, c_ctx, w_mod, b_mod, norm1_g, norm2_g, w_in, conv_w, a_log, dt_bias, gdn_norm_w, w_out,
           w_router, w_gate, w_up, w_down, norm_f):
    b, n, _ = x.shape
    nctx = ctx.shape[1]
    cap = 2 * n // NE
    i = 0

    cc = jnp.concatenate([c, c_ctx[None, :], jnp.zeros((8 - b - 1, D), F32)], axis=0)
    mod = _mod_call(cc, w_mod[i], b_mod[i][None, :])
    sh1, sc1, gt1, sh2, sc2, gt2 = [mod[:b, k * D:(k + 1) * D][:, None, :] for k in range(6)]
    sh1c = jnp.broadcast_to(mod[b:b + 1, 0:D][:, None, :], (b, 1, D))
    sc1c = jnp.broadcast_to(mod[b:b + 1, D:2 * D][:, None, :], (b, 1, D))

    nmain = FW + 4 * GW
    w_main = w_in[i][:, :nmain].astype(BF16)
    w_ab = jnp.pad(w_in[i][:, nmain:], ((0, 0), (0, LANES - 2 * N_DIR * NH))).astype(BF16)
    conv_w8 = jnp.pad(conv_w[i], ((0, 8 - CONV_K), (0, 0)))
    alog_row = jnp.pad(a_log[i].reshape(1, NU), ((0, 0), (0, LANES - NU)))
    dtb_row = jnp.pad(dt_bias[i].reshape(1, NU), ((0, 0), (0, LANES - NU)))
    g1 = norm1_g[i][None, :]

    p_ctx, ab_ctx = _inproj_call(ctx, g1, sh1c, sc1c, w_main, w_ab, nctx, 1, 3)
    qkv_ctx = _conv_call(p_ctx, conv_w8, nctx, nctx, 0)
    prep_ctx = _gdn_prep_call(qkv_ctx, ab_ctx, alog_row, dtb_row)
    s_zero = jnp.zeros((b, NU, HD, HD), F32)
    _, _, s_ctx = _gdn_scan_call(*prep_ctx, s_zero)

    p_main, ab = _inproj_call(x, g1, sh1, sc1, w_main, w_ab, 1024, 0, 5)
    qkv = _conv_call(p_main, conv_w8, 512, GRID_W, 1)
    prep = _gdn_prep_call(qkv, ab, alog_row, dtb_row)
    o_f, o_b, _ = _gdn_scan_call(*prep, s_ctx)
    yf = _fourier_call(p_main)

    w_r = jnp.pad(w_router[i], ((0, 0), (0, LANES - NE)))
    w_r_hi = w_r.astype(BF16)
    w_r = jnp.concatenate([w_r_hi, (w_r - w_r_hi.astype(F32)).astype(BF16)], axis=1)
    x1, hx2, lg_t = _outproj_call(yf, o_f, o_b, p_main, x, w_out[i].astype(BF16), gdn_norm_w[i][None, :], gt1,
                                  norm2_g[i][None, :], sh2, sc2, w_r)

    idx_c, val_c = _topk_call(lg_t, cap)
    idx = jnp.transpose(idx_c[:, :, :NE], (2, 0, 1))
    val = jnp.transpose(val_c[:, :, :NE], (2, 0, 1))
    idx_flat = (idx + (jnp.arange(b, dtype=I32) * n)[None, :, None]).reshape(NE * b * cap)
    vals = val.reshape(NE, 1, b * cap)

    y = _moe_ffn_call(idx_flat, hx2.reshape(b * n, D), vals, gt2, w_gate[i], w_up[i], w_down[i], cap)
    out = _moe_scatter_call(idx_flat, y, x1.reshape(b * n, D))
    return _final_call(out, norm_f[None, :]).reshape(b, n, D)
```

```python
import functools
import math

import numpy as np
import jax
import jax.numpy as jnp
from jax import lax
from jax.experimental import pallas as pl
from jax.experimental.pallas import tpu as pltpu

F32 = jnp.float32
BF16 = jnp.bfloat16
I32 = jnp.int32

D = 2048
SEQ_N = 4096
GRID_W = 64
FW = 1024
FG = 8
FGD = 128
GW = 1024
NH = 8
HD = 128
CONV_K = 5
N_DIR = 2
NE = 16
EFF = 1536
EPS = 1e-6
GC = 128
NU = N_DIR * NH
LANES = 128
VMEM_LIMIT = 56 * 1024 * 1024


def _sigmoid(x):
    return 1.0 / (1.0 + jnp.exp(-x))


def _silu(x):
    return x * _sigmoid(x)


def _softplus(x):
    return jnp.maximum(x, 0.0) + jnp.log(1.0 + jnp.exp(-jnp.abs(x)))


def _dot(a, b):
    return jnp.dot(a, b, preferred_element_type=F32)


def _dot_nt(a, b):
    return lax.dot_general(a, b, (((1,), (1,)), ((), ())), preferred_element_type=F32)


def _bmm(a, b):
    return jnp.einsum('uij,ujk->uik', a.astype(BF16), b.astype(BF16), preferred_element_type=F32)


def _bmm_nt(a, b):
    return jnp.einsum('uik,ujk->uij', a, b, preferred_element_type=F32)


def _cparams(sem, vmem=VMEM_LIMIT):
    return pltpu.CompilerParams(dimension_semantics=sem, vmem_limit_bytes=vmem)


def _mod_kernel(c_ref, w_ref, b_ref, o_ref):
    s = _silu(c_ref[...]).astype(BF16)
    o_ref[...] = _dot(s, w_ref[...].astype(BF16)) + b_ref[...]


def _mod_call(cc, w_mod, b_mod):
    tn = 1024
    n = w_mod.shape[1]
    return pl.pallas_call(
        _mod_kernel,
        grid=(n // tn,),
        in_specs=[pl.BlockSpec((8, D), lambda j: (0, 0)),
                  pl.BlockSpec((D, tn), lambda j: (0, j)),
                  pl.BlockSpec((1, tn), lambda j: (0, j))],
        out_specs=pl.BlockSpec((8, tn), lambda j: (0, j)),
        out_shape=jax.ShapeDtypeStruct((8, n), F32),
        compiler_params=_cparams(("parallel",)),
        name="mod",
    )(cc, w_mod, b_mod)


def _inproj_kernel(x_ref, g_ref, sh_ref, sc_ref, w_ref, wab_ref, o_ref, ab_ref, h_scr):
    @pl.when(pl.program_id(2) == 0)
    def _():
        x = x_ref[0]
        ms = jnp.mean(x * x, axis=-1, keepdims=True)
        y = x * lax.rsqrt(ms + EPS) * g_ref[...]
        h = (y * (1.0 + sc_ref[0]) + sh_ref[0]).astype(BF16)
        h_scr[...] = h
        ab_ref[0] = _dot(h, wab_ref[...])

    o_ref[0] = _dot(h_scr[...], w_ref[...]).astype(o_ref.dtype)


def _inproj_call(x, g, sh, sc, w_main, w_ab, tm, col0, ncol):
    b, t, _ = x.shape
    tn = 1024
    return pl.pallas_call(
        _inproj_kernel,
        grid=(b, t // tm, ncol),
        in_specs=[pl.BlockSpec((1, tm, D), lambda bi, i, j: (bi, i, 0)),
                  pl.BlockSpec((1, D), lambda bi, i, j: (0, 0)),
                  pl.BlockSpec((1, 1, D), lambda bi, i, j: (bi, 0, 0)),
                  pl.BlockSpec((1, 1, D), lambda bi, i, j: (bi, 0, 0)),
                  pl.BlockSpec((D, tn), lambda bi, i, j: (0, j + col0)),
                  pl.BlockSpec((D, LANES), lambda bi, i, j: (0, 0))],
        out_specs=[pl.BlockSpec((1, tm, tn), lambda bi, i, j: (bi, i, j)),
                   pl.BlockSpec((1, tm, LANES), lambda bi, i, j: (bi, i, 0))],
        out_shape=[jax.ShapeDtypeStruct((b, t, ncol * tn), BF16),
                   jax.ShapeDtypeStruct((b, t, LANES), F32)],
        scratch_shapes=[pltpu.VMEM((tm, D), BF16)],
        compiler_params=_cparams(("parallel", "parallel", "arbitrary")),
        name="inproj",
    )(x, g, sh, sc, w_main, w_ab)


def _conv_kernel(x_ref, w_ref, o_ref, *, row_len):
    j = pl.program_id(2)
    half = CONV_K // 2
    rr = lax.broadcasted_iota(I32, (row_len, row_len), 0)
    cc = lax.broadcasted_iota(I32, (row_len, row_len), 1)
    taps = [t for t in range(CONV_K) if t != half]
    shifts = jnp.concatenate([(cc == rr + (t - half)).astype(BF16) for t in taps], axis=0)
    qscale = jnp.where(j == 0, HD ** -0.5, 1.0).astype(F32)

    def group(g, carry):
        r0 = pl.multiple_of(g * row_len, row_len)
        xb = x_ref[0, pl.ds(r0, row_len), :]
        sh = _dot(shifts, xb)
        for h in range(NH):
            cols = slice(h * HD, (h + 1) * HD)
            acc = xb[:, cols].astype(F32) * w_ref[half:half + 1, cols]
            for ti, tap in enumerate(taps):
                acc = acc + sh[ti * row_len:(ti + 1) * row_len, cols] * w_ref[tap:tap + 1, cols]
            y = _silu(acc)
            ss = jnp.sum(y * y, axis=-1, keepdims=True)
            nrm = jnp.where(j < 2, lax.rsqrt(ss + EPS) * qscale, 1.0)
            o_ref[0, pl.ds(r0, row_len), cols] = (y * nrm).astype(o_ref.dtype)
        return carry

    lax.fori_loop(0, x_ref.shape[1] // row_len, group, 0, unroll=True)


def _conv_call(p, conv_w8, tt, row_len, col0):
    b, t, _ = p.shape
    return pl.pallas_call(
        functools.partial(_conv_kernel, row_len=row_len),
        grid=(b, t // tt, 3),
        in_specs=[pl.BlockSpec((1, tt, GW), lambda bi, i, j: (bi, i, j + col0)),
                  pl.BlockSpec((8, GW), lambda bi, i, j: (0, j))],
        out_specs=pl.BlockSpec((1, tt, GW), lambda bi, i, j: (bi, i, j)),
        out_shape=jax.ShapeDtypeStruct((b, t, 3 * GW), BF16),
        compiler_params=_cparams(("parallel", "parallel", "arbitrary")),
        name="conv",
    )(p, conv_w8)


def _gdn_kernel(qf_ref, kf_ref, vf_ref, abf_ref, qb_ref, kb_ref, vb_ref, abb_ref, alog_ref, dtb_ref, s0_ref,
                of_ref, ob_ref, sout_ref, s_scr):
    n = pl.program_id(1)

    @pl.when(n == 0)
    def _():
        s_scr[...] = s0_ref[0]

    row = lax.broadcasted_iota(I32, (GC, GC), 0)
    col = lax.broadcasted_iota(I32, (GC, GC), 1)
    eye = (row == col).astype(F32)
    m_parts, rhs_parts, qd_parts, kdt_parts, qk_parts, gt_parts = [], [], [], [], [], []
    for d in range(N_DIR):
        q_ref, k_ref, v_ref, ab_ref = ((qf_ref, kf_ref, vf_ref, abf_ref) if d == 0
                                       else (qb_ref, kb_ref, vb_ref, abb_ref))
        incl = (col <= row) if d == 0 else (col >= row)
        strict = (col < row) if d == 0 else (col > row)
        ab = ab_ref[0]
        g_all = -jnp.exp(alog_ref[...]) * _softplus(ab + dtb_ref[...])
        beta_all = _sigmoid(ab)
        gc_all = jnp.dot(incl.astype(F32), g_all, preferred_element_type=F32,
                         precision=lax.Precision.HIGHEST)
        gc_t = gc_all.T
        last = GC - 1 if d == 0 else 0
        c0 = d * NH
        cb = N_DIR * NH + d * NH
        heads = range(NH)
        gcol = jnp.stack([gc_all[:, c0 + h:c0 + h + 1] for h in heads])
        grow = jnp.stack([gc_t[c0 + h:c0 + h + 1, :] for h in heads])
        glast = jnp.stack([gc_all[last:last + 1, c0 + h:c0 + h + 1] for h in heads])
        beta = jnp.stack([beta_all[:, cb + h:cb + h + 1] for h in heads])
        q = jnp.stack([q_ref[0, :, h * HD:(h + 1) * HD] for h in heads]).astype(F32)
        k = jnp.stack([k_ref[0, :, h * HD:(h + 1) * HD] for h in heads])
        kf = k.astype(F32)
        v = jnp.stack([v_ref[0, :, h * HD:(h + 1) * HD] for h in heads]).astype(F32)
        decay = jnp.where(incl[None], jnp.exp(jnp.where(incl[None], gcol - grow, 0.0)), 0.0)
        egc = jnp.exp(gcol)
        kbeta = kf * beta
        a2 = _bmm_nt(jnp.concatenate([kbeta, q], axis=1).astype(BF16), k)
        m_parts.append(jnp.where(strict[None], a2[:, :GC] * decay, 0.0))
        rhs_parts.append(jnp.concatenate([v * beta, kbeta * egc], axis=2).astype(BF16))
        qd_parts.append((q * egc).astype(BF16))
        kdt_parts.append(jnp.swapaxes(kf * jnp.exp(glast - gcol), 1, 2).astype(BF16))
        qk_parts.append((a2[:, GC:] * decay).astype(BF16))
        gt_parts.append(jnp.exp(glast))

    cat = lambda parts: jnp.concatenate(parts, axis=0)
    m, rhs, qd, kdt, qk, gtot = (cat(t) for t in (m_parts, rhs_parts, qd_parts, kdt_parts, qk_parts, gt_parts))
    blk = lambda s: ((row >> int(math.log2(s))) == (col >> int(math.log2(s))))[None]
    mb = jnp.where(blk(8), m, 0.0)
    p = eye[None] - mb
    m2 = _bmm(mb, mb)
    p = p + _bmm(p, m2)
    p = p + _bmm(p, _bmm(m2, m2))
    s = 8
    while s < GC:
        cpart = jnp.where(jnp.logical_and(blk(2 * s), jnp.logical_not(blk(s))), m, 0.0)
        p = p - _bmm(p, _bmm(cpart, p))
        s *= 2
    uw = _bmm(p, rhs)

    st = s_scr[...]
    r = _bmm(jnp.concatenate([uw[:, :, HD:].astype(BF16), qd], axis=1), st)
    vb = (uw[:, :, :HD] - r[:, :GC]).astype(BF16)
    o = r[:, GC:] + _bmm(qk, vb)
    s_scr[...] = st * gtot + _bmm(kdt, vb)
    for un in range(NU):
        h = un % NH
        if un < NH:
            of_ref[0, :, h * HD:(h + 1) * HD] = o[un]
        else:
            ob_ref[0, :, h * HD:(h + 1) * HD] = o[un]

    @pl.when(n == pl.num_programs(1) - 1)
    def _():
        sout_ref[0] = s_scr[...]


def _gdn_call(qkv, ab, alog_row, dtb_row, s0):
    b, t, _ = qkv.shape
    nc = t // GC
    blk = lambda c, rev: pl.BlockSpec((1, GC, GW), (lambda bi, n: (bi, nc - 1 - n, c)) if rev
                                      else (lambda bi, n: (bi, n, c)))
    abblk = lambda rev: pl.BlockSpec((1, GC, LANES), (lambda bi, n: (bi, nc - 1 - n, 0)) if rev
                                     else (lambda bi, n: (bi, n, 0)))
    sblk = pl.BlockSpec((1, NU, HD, HD), lambda bi, n: (bi, 0, 0, 0))
    return pl.pallas_call(
        _gdn_kernel,
        grid=(b, nc),
        in_specs=[blk(0, False), blk(1, False), blk(2, False), abblk(False),
                  blk(0, True), blk(1, True), blk(2, True), abblk(True),
                  pl.BlockSpec((1, LANES), lambda bi, n: (0, 0)),
                  pl.BlockSpec((1, LANES), lambda bi, n: (0, 0)),
                  sblk],
        out_specs=[pl.BlockSpec((1, GC, GW), lambda bi, n: (bi, n, 0)),
                   pl.BlockSpec((1, GC, GW), lambda bi, n: (bi, nc - 1 - n, 0)),
                   sblk],
        out_shape=[jax.ShapeDtypeStruct((b, t, GW), F32),
                   jax.ShapeDtypeStruct((b, t, GW), F32),
                   jax.ShapeDtypeStruct((b, NU, HD, HD), F32)],
        scratch_shapes=[pltpu.VMEM((NU, HD, HD), F32)],
        compiler_params=_cparams(("parallel", "arbitrary")),
        name="gdn",
    )(qkv, qkv, qkv, ab, qkv, qkv, qkv, ab, alog_row, dtb_row, s0)


def _dft_tables():
    r = GRID_W
    a = 2.0 * np.pi * np.outer(np.arange(r), np.arange(r)) / r
    c64, s64 = np.cos(a), np.sin(a)
    ac = 2.0 * np.pi * np.outer(np.arange(FGD), np.arange(FGD)) / FGD
    cc, sc = np.cos(ac), np.sin(ac)
    at = (2.0 * np.pi * np.outer(np.arange(r), np.arange(r)) / (r * r)).reshape(r * r, 1)
    chan = np.concatenate([cc, -sc], axis=1)
    row_cs = np.concatenate([c64, s64], axis=0)
    col_re = np.concatenate([c64, s64], axis=1)
    as_bf16 = lambda t: jnp.asarray(t, F32).astype(BF16)
    return (as_bf16(chan), as_bf16(row_cs), as_bf16(col_re),
            jnp.asarray(np.cos(at), F32), jnp.asarray(np.sin(at), F32))


def _fourier_kernel(x_ref, chan_ref, rowcs_ref, colre_ref, twc_ref, tws_ref, o_ref, u_scr, v_scr, y_scr):
    r = GRID_W
    u = _dot(x_ref[0], chan_ref[...])
    u_scr[0] = u[:, :FGD]
    u_scr[1] = u[:, FGD:]

    def over_n2(n1, carry):
        x = jnp.concatenate([u_scr[0, pl.ds(n1, r, stride=r), :], u_scr[1, pl.ds(n1, r, stride=r), :]],
                            axis=1).astype(BF16)
        cs = _dot(rowcs_ref[...], x)
        v_scr[0, pl.ds(n1, r, stride=r), :] = cs[:r, :FGD] + cs[r:, FGD:]
        v_scr[1, pl.ds(n1, r, stride=r), :] = cs[:r, FGD:] - cs[r:, :FGD]
        return carry

    lax.fori_loop(0, r, over_n2, 0, unroll=4)

    def over_n1(k2, carry):
        r0 = pl.multiple_of(k2 * r, r)
        vr = v_scr[0, pl.ds(r0, r), :]
        vi = v_scr[1, pl.ds(r0, r), :]
        tc = twc_ref[pl.ds(r0, r), :]
        ts = tws_ref[pl.ds(r0, r), :]
        st = jnp.concatenate([vr * tc + vi * ts, vi * tc - vr * ts], axis=0).astype(BF16)
        y = _dot(colre_ref[...], st)
        y_scr[pl.ds(k2, r, stride=r), :] = y * (1.0 / math.sqrt(SEQ_N * FGD))
        return carry

    lax.fori_loop(0, r, over_n1, 0, unroll=4)
    o_ref[0] = y_scr[...].astype(o_ref.dtype)


def _fourier_call(p_main):
    b, n, _ = p_main.shape
    r = GRID_W
    chan, rowcs, colre, twc, tws = _dft_tables()
    twc = jnp.broadcast_to(twc, (n, LANES))
    tws = jnp.broadcast_to(tws, (n, LANES))
    const = lambda shape: pl.BlockSpec(shape, lambda bi, g: (0, 0))
    return pl.pallas_call(
        _fourier_kernel,
        grid=(b, FG),
        in_specs=[pl.BlockSpec((1, n, FGD), lambda bi, g: (bi, 0, g)),
                  const((FGD, 2 * FGD)), const((2 * r, r)), const((r, 2 * r)),
                  const((n, LANES)), const((n, LANES))],
        out_specs=pl.BlockSpec((1, n, FGD), lambda bi, g: (bi, 0, g)),
        out_shape=jax.ShapeDtypeStruct((b, n, FW), BF16),
        scratch_shapes=[pltpu.VMEM((2, n, FGD), F32), pltpu.VMEM((2, n, FGD), F32), pltpu.VMEM((n, FGD), F32)],
        compiler_params=_cparams(("parallel", "parallel")),
        name="fourier",
    )(p_main, chan, rowcs, colre, twc, tws)


def _outproj_kernel(yf_ref, of_ref, ob_ref, z_ref, x_ref, wout_ref, gnw_ref, gt1_ref, g2_ref, sh2_ref, sc2_ref,
                    wr_ref, x1_ref, hx_ref, lg_ref):
    sub = 128
    for rb in range(x_ref.shape[1] // sub):
        rows = slice(rb * sub, (rb + 1) * sub)
        parts = [yf_ref[0, rows, :]]
        for h in range(NH):
            cols = slice(h * HD, (h + 1) * HD)
            oh = of_ref[0, rows, cols] + ob_ref[0, rows, cols]
            ms = jnp.mean(oh * oh, axis=-1, keepdims=True)
            on = oh * lax.rsqrt(ms + EPS) * gnw_ref[...]
            parts.append((on * _silu(z_ref[0, rows, cols].astype(F32))).astype(BF16))
        mix = jnp.concatenate(parts, axis=1)
        x1 = x_ref[0, rows, :] + gt1_ref[0] * _dot(mix, wout_ref[...])
        x1_ref[0, rows, :] = x1
        ms = jnp.mean(x1 * x1, axis=-1, keepdims=True)
        hx = x1 * lax.rsqrt(ms + EPS) * g2_ref[...] * (1.0 + sc2_ref[0]) + sh2_ref[0]
        hx_ref[0, rows, :] = hx
        hx_hi = hx.astype(BF16)
        hx_lo = (hx - hx_hi.astype(F32)).astype(BF16)
        l2 = _dot(hx_hi, wr_ref[...])
        lg = l2[:, :LANES] + l2[:, LANES:] + _dot(hx_lo, wr_ref[:, :LANES])
        lg_ref[0, :, rows] = lg.T[:NE, :]


def _outproj_call(yf, o_f, o_b, p_main, x, w_out, gnw, gt1, g2, sh2, sc2, w_r):
    b, t, _ = x.shape
    tm = 256
    zcol = p_main.shape[2] // GW - 1
    row = lambda c: pl.BlockSpec((1, tm, c), lambda bi, i: (bi, i, 0))
    vec = pl.BlockSpec((1, 1, D), lambda bi, i: (bi, 0, 0))
    return pl.pallas_call(
        _outproj_kernel,
        grid=(b, t // tm),
        in_specs=[row(FW), row(GW), row(GW),
                  pl.BlockSpec((1, tm, GW), lambda bi, i: (bi, i, zcol)),
                  row(D),
                  pl.BlockSpec((D, D), lambda bi, i: (0, 0)),
                  pl.BlockSpec((1, HD), lambda bi, i: (0, 0)),
                  vec,
                  pl.BlockSpec((1, D), lambda bi, i: (0, 0)),
                  vec, vec,
                  pl.BlockSpec((D, 2 * LANES), lambda bi, i: (0, 0))],
        out_specs=[row(D), row(D), pl.BlockSpec((1, NE, tm), lambda bi, i: (bi, 0, i))],
        out_shape=[jax.ShapeDtypeStruct((b, t, D), F32),
                   jax.ShapeDtypeStruct((b, t, D), F32),
                   jax.ShapeDtypeStruct((b, NE, t), F32)],
        compiler_params=_cparams(("parallel", "parallel")),
        name="outproj",
    )(yf, o_f, o_b, p_main, x, w_out, gnw, gt1, g2, sh2, sc2, w_r)


def _lane_cumsum(x):
    n = x.shape[-1]
    lane = lax.broadcasted_iota(I32, x.shape, x.ndim - 1)
    sh = 1
    while sh < n:
        x = x + jnp.where(lane >= sh, pltpu.roll(x, shift=sh, axis=x.ndim - 1), 0.0)
        sh *= 2
    return x


def _topk_kernel(lg_ref, idx_ref, val_ref, rsel_scr, aff_scr, *, cap):
    lg = lg_ref[0]
    t = lg.shape[1]
    mx = jnp.max(lg, axis=0, keepdims=True)
    ex = jnp.exp(lg - mx)
    aff = ex / jnp.sum(ex, axis=0, keepdims=True)

    def search(i, cur):
        cand = cur | (1 << (30 - i))
        cnt = jnp.sum((aff >= pltpu.bitcast(cand, F32)).astype(F32), axis=1, keepdims=True)
        return jnp.where(cnt >= cap, cand, cur)

    thr = pltpu.bitcast(lax.fori_loop(0, 31, search, jnp.zeros((NE, 1), I32)), F32)
    gt = aff > thr
    eq = aff == thr
    n_gt = jnp.sum(gt.astype(F32), axis=1, keepdims=True)
    eq_rank = _lane_cumsum(eq.astype(F32))
    sel = jnp.logical_or(gt, jnp.logical_and(eq, eq_rank <= cap - n_gt))
    rank = _lane_cumsum(sel.astype(F32))
    rsel_scr[...] = jnp.where(sel, rank, 0.0)
    aff_scr[...] = aff
    idx_ref[0] = jnp.zeros((cap, LANES), I32)
    val_ref[0] = jnp.zeros((cap, LANES), F32)
    tpos = lax.broadcasted_iota(I32, (1, t), 1).astype(F32)
    sblk = 64
    lane = lax.broadcasted_iota(I32, (sblk, LANES), 1)

    def per_expert(e, carry):
        rrow = rsel_scr[pl.ds(e, 1), :]
        arow = aff_scr[pl.ds(e, 1), :]
        for sb in range(cap // sblk):
            slot = (lax.broadcasted_iota(I32, (sblk, 1), 0) + (sb * sblk + 1)).astype(F32)
            hit = rrow == slot
            ic = jnp.sum(jnp.where(hit, tpos, 0.0), axis=1, keepdims=True)
            vc = jnp.sum(jnp.where(hit, arow, 0.0), axis=1, keepdims=True)
            rows = slice(sb * sblk, (sb + 1) * sblk)
            idx_ref[0, rows, :] = jnp.where(lane == e, ic.astype(I32), idx_ref[0, rows, :])
            val_ref[0, rows, :] = jnp.where(lane == e, vc, val_ref[0, rows, :])
        return carry

    lax.fori_loop(0, NE, per_expert, 0)


def _topk_call(lg_t, cap):
    b, _, t = lg_t.shape
    return pl.pallas_call(
        functools.partial(_topk_kernel, cap=cap),
        grid=(b,),
        in_specs=[pl.BlockSpec((1, NE, t), lambda bi: (bi, 0, 0))],
        out_specs=[pl.BlockSpec((1, cap, LANES), lambda bi: (bi, 0, 0)),
                   pl.BlockSpec((1, cap, LANES), lambda bi: (bi, 0, 0))],
        out_shape=[jax.ShapeDtypeStruct((b, cap, LANES), I32),
                   jax.ShapeDtypeStruct((b, cap, LANES), F32)],
        scratch_shapes=[pltpu.VMEM((NE, t), F32), pltpu.VMEM((NE, t), F32)],
        compiler_params=_cparams(("parallel",)),
        name="topk",
    )(lg_t)


MOE_TF = 256
MOE_TN = 256
MOE_MC = 512
MOE_NF = EFF // MOE_TF
MOE_NN = D // MOE_TN
MOE_GC = 352
MOE_GPAD = 32


def _moe_ffn_kernel(idx_ref, hx_hbm, val_ref, gt2_ref, wg_ref, wu_ref, wd_ref, y_ref,
                    xg_scr, hid_scr, stg_scr, vcol_scr, gsem, *, rows):
    e = pl.program_id(0)
    s = pl.program_id(1)
    cur = e % 2
    nxt = jnp.minimum(e + 1, pl.num_programs(0) - 1)

    def issue(expert, c):
        base = expert * rows + c * MOE_GC
        for r in range(MOE_GC):
            pltpu.make_async_copy(hx_hbm.at[pl.ds(idx_ref[base + r], 1), :],
                                  stg_scr.at[pl.ds(r, 1), :], gsem).start()

    def land(c, buf):
        pltpu.make_async_copy(hx_hbm.at[pl.ds(0, MOE_GC), :], stg_scr, gsem).wait()
        r0 = pl.multiple_of(c * MOE_GC, MOE_GPAD)
        xg_scr[buf, pl.ds(r0, MOE_GC), :] = stg_scr[...].astype(BF16)

    @pl.when(jnp.logical_and(e == 0, s == 0))
    def _():
        for c in range(MOE_NF):
            issue(0, c)
            land(c, 0)

    @pl.when(jnp.logical_and(s >= 1, s <= MOE_NF))
    def _():
        land(s - 1, 1 - cur)

    @pl.when(s < MOE_NF)
    def _():
        xs = xg_scr[cur, 0:rows, :]
        hid = (_silu(_dot(xs, wg_ref[0].astype(BF16))) * _dot(xs, wu_ref[0].astype(BF16))).astype(BF16)
        hid_scr[:, pl.ds(pl.multiple_of(s * MOE_TF, MOE_TF), MOE_TF)] = hid
        issue(nxt, s)

    @pl.when(s == MOE_NF)
    def _():
        vcol_scr[...] = jnp.broadcast_to(val_ref[0], (LANES, rows)).T

    @pl.when(s >= MOE_NF)
    def _():
        y = _dot(hid_scr[...], wd_ref[0].astype(BF16))
        for mc in range(rows // MOE_MC):
            sl = slice(mc * MOE_MC, (mc + 1) * MOE_MC)
            y_ref[0, sl, :] = (y[sl] * vcol_scr[sl, 0:1] * gt2_ref[mc, 0]).astype(y_ref.dtype)


def _moe_ffn_call(idx_flat, hx2, vals, gt2, w_gate, w_up, w_down, cap):
    rows = vals.shape[2]
    rows_pad = MOE_NF * MOE_GC
    assert rows // MOE_MC == gt2.shape[0] and cap == MOE_MC and rows_pad >= rows and MOE_GC % MOE_GPAD == 0
    idx_flat = jnp.pad(idx_flat, (0, rows_pad - rows))
    gt2t = gt2.reshape(gt2.shape[0], MOE_NN, 1, MOE_TN)
    ph1 = lambda s: jnp.minimum(s, MOE_NF - 1)
    ph2 = lambda s: jnp.maximum(s - MOE_NF, 0)
    grid_spec = pltpu.PrefetchScalarGridSpec(
        num_scalar_prefetch=1,
        grid=(NE, MOE_NF + MOE_NN),
        in_specs=[pl.BlockSpec(memory_space=pl.ANY),
                  pl.BlockSpec((1, 1, rows), lambda e, s, idx: (e, 0, 0)),
                  pl.BlockSpec((gt2.shape[0], 1, 1, MOE_TN), lambda e, s, idx: (0, ph2(s), 0, 0)),
                  pl.BlockSpec((1, D, MOE_TF), lambda e, s, idx: (e, 0, ph1(s))),
                  pl.BlockSpec((1, D, MOE_TF), lambda e, s, idx: (e, 0, ph1(s))),
                  pl.BlockSpec((1, EFF, MOE_TN), lambda e, s, idx: (e, 0, ph2(s)))],
        out_specs=pl.BlockSpec((1, rows, MOE_TN), lambda e, s, idx: (e, 0, ph2(s))),
        scratch_shapes=[pltpu.VMEM((2, rows_pad, D), BF16),
                        pltpu.VMEM((rows, EFF), BF16),
                        pltpu.VMEM((MOE_GC, D), F32),
                        pltpu.VMEM((rows, LANES), F32),
                        pltpu.SemaphoreType.DMA(())],
    )
    return pl.pallas_call(
        functools.partial(_moe_ffn_kernel, rows=rows),
        grid_spec=grid_spec,
        out_shape=jax.ShapeDtypeStruct((NE, rows, D), BF16),
        compiler_params=_cparams(("arbitrary", "arbitrary")),
        name="moe_ffn",
    )(idx_flat, hx2, vals, gt2t, w_gate, w_up, w_down)


MOE_SC = 256


def _moe_scatter_kernel(idx_ref, y_ref, res_hbm, out_hbm, buf_scr, rsem, wsem, *, rows):
    del res_hbm
    e = pl.program_id(0)
    c = pl.program_id(1)
    nc = pl.num_programs(1)
    slot = c % 2

    def reads(chunk, sl):
        base = e * rows + chunk * MOE_SC
        for r in range(MOE_SC):
            pltpu.make_async_copy(out_hbm.at[pl.ds(idx_ref[base + r], 1), :],
                                  buf_scr.at[sl, pl.ds(r, 1), :], rsem.at[sl]).start(priority=r % 2)

    def writes(chunk, sl):
        base = e * rows + chunk * MOE_SC
        for r in range(MOE_SC):
            pltpu.make_async_copy(buf_scr.at[sl, pl.ds(r, 1), :],
                                  out_hbm.at[pl.ds(idx_ref[base + r], 1), :], wsem.at[sl]).start(priority=r % 2)

    def wait_all(sem, sl):
        pltpu.make_async_copy(out_hbm.at[pl.ds(0, MOE_SC), :], buf_scr.at[sl], sem.at[sl]).wait()

    @pl.when(c == 0)
    def _():
        reads(c, slot)

    wait_all(rsem, slot)

    @pl.when(c + 1 < nc)
    def _():
        @pl.when(c >= 1)
        def _():
            wait_all(wsem, 1 - slot)
        reads(c + 1, 1 - slot)

    buf_scr[slot] = buf_scr[slot] + y_ref[0].astype(F32)
    writes(c, slot)

    @pl.when(c == nc - 1)
    def _():
        wait_all(wsem, 1 - slot)
        wait_all(wsem, slot)


def _moe_scatter_call(idx_flat, y, x1):
    rows = y.shape[1]
    grid_spec = pltpu.PrefetchScalarGridSpec(
        num_scalar_prefetch=1,
        grid=(NE, rows // MOE_SC),
        in_specs=[pl.BlockSpec((1, MOE_SC, D), lambda e, c, idx: (e, c, 0)),
                  pl.BlockSpec(memory_space=pl.ANY)],
        out_specs=pl.BlockSpec(memory_space=pl.ANY),
        scratch_shapes=[pltpu.VMEM((2, MOE_SC, D), F32),
                        pltpu.SemaphoreType.DMA((2,)),
                        pltpu.SemaphoreType.DMA((2,))],
    )
    return pl.pallas_call(
        functools.partial(_moe_scatter_kernel, rows=rows),
        grid_spec=grid_spec,
        out_shape=jax.ShapeDtypeStruct(x1.shape, F32),
        input_output_aliases={2: 0},
        compiler_params=_cparams(("arbitrary", "arbitrary")),
        name="moe_scatter",
    )(idx_flat, y, x1)


def _final_kernel(x_ref, g_ref, o_ref):
    x = x_ref[...]
    ms = jnp.mean(x * x, axis=-1, keepdims=True)
    o_ref[...] = x * lax.rsqrt(ms + EPS) * g_ref[...]


def _final_call(x2d, g):
    m = x2d.shape[0]
    tm = 512
    return pl.pallas_call(
        _final_kernel,
        grid=(m // tm,),
        in_specs=[pl.BlockSpec((tm, D), lambda i: (i, 0)), pl.BlockSpec((1, D), lambda i: (0, 0))],
        out_specs=pl.BlockSpec((tm, D), lambda i: (i, 0)),
        out_shape=jax.ShapeDtypeStruct(x2d.shape, F32),
        compiler_params=_cparams(("parallel",)),
        name="final_norm",
    )(x2d, g)


def kernel(x, c, ctx, c_ctx, w_mod, b_mod, norm1_g, norm2_g, w_in, conv_w, a_log, dt_bias, gdn_norm_w, w_out,
           w_router, w_gate, w_up, w_down, norm_f):
    b, n, _ = x.shape
    nctx = ctx.shape[1]
    cap = 2 * n // NE
    i = 0

    cc = jnp.concatenate([c, c_ctx[None, :], jnp.zeros((8 - b - 1, D), F32)], axis=0)
    mod = _mod_call(cc, w_mod[i], b_mod[i][None, :])
    sh1, sc1, gt1, sh2, sc2, gt2 = [mod[:b, k * D:(k + 1) * D][:, None, :] for k in range(6)]
    sh1c = jnp.broadcast_to(mod[b:b + 1, 0:D][:, None, :], (b, 1, D))
    sc1c = jnp.broadcast_to(mod[b:b + 1, D:2 * D][:, None, :], (b, 1, D))

    nmain = FW + 4 * GW
    w_main = w_in[i][:, :nmain].astype(BF16)
    w_ab = jnp.pad(w_in[i][:, nmain:], ((0, 0), (0, LANES - 2 * N_DIR * NH))).astype(BF16)
    conv_w8 = jnp.pad(conv_w[i], ((0, 8 - CONV_K), (0, 0)))
    alog_row = jnp.pad(a_log[i].reshape(1, NU), ((0, 0), (0, LANES - NU)))
    dtb_row = jnp.pad(dt_bias[i].reshape(1, NU), ((0, 0), (0, LANES - NU)))
    g1 = norm1_g[i][None, :]

    p_ctx, ab_ctx = _inproj_call(ctx, g1, sh1c, sc1c, w_main, w_ab, nctx, 1, 3)
    qkv_ctx = _conv_call(p_ctx, conv_w8, nctx, nctx, 0)
    _, _, s_ctx = _gdn_call(qkv_ctx, ab_ctx, alog_row, dtb_row, jnp.zeros((b, NU, HD, HD), F32))

    p_main, ab = _inproj_call(x, g1, sh1, sc1, w_main, w_ab, 1024, 0, 5)
    qkv = _conv_call(p_main, conv_w8, 512, GRID_W, 1)
    o_f, o_b, _ = _gdn_call(qkv, ab, alog_row, dtb_row, s_ctx)
    yf = _fourier_call(p_main)

    w_r = jnp.pad(w_router[i], ((0, 0), (0, LANES - NE)))
    w_r_hi = w_r.astype(BF16)
    w_r = jnp.concatenate([w_r_hi, (w_r - w_r_hi.astype(F32)).astype(BF16)], axis=1)
    x1, hx2, lg_t = _outproj_call(yf, o_f, o_b, p_main, x, w_out[i].astype(BF16), gdn_norm_w[i][None, :], gt1,
                                  norm2_g[i][None, :], sh2, sc2, w_r)

    idx_c, val_c = _topk_call(lg_t, cap)
    idx = jnp.transpose(idx_c[:, :, :NE], (2, 0, 1))
    val = jnp.transpose(val_c[:, :, :NE], (2, 0, 1))
    idx_flat = (idx + (jnp.arange(b, dtype=I32) * n)[None, :, None]).reshape(NE * b * cap)
    vals = val.reshape(NE, 1, b * cap)

    y = _moe_ffn_call(idx_flat, hx2.reshape(b * n, D), vals, gt2, w_gate[i], w_up[i], w_down[i], cap)
    out = _moe_scatter_call(idx_flat, y, x1.reshape(b * n, D))
    return _final_call(out, norm_f[None, :]).reshape(b, n, D)
```

```python
import functools
import math

import numpy as np
import jax
import jax.numpy as jnp
from jax import lax
from jax.experimental import pallas as pl
from jax.experimental.pallas import tpu as pltpu

F32 = jnp.float32
BF16 = jnp.bfloat16
I32 = jnp.int32

D = 2048
SEQ_N = 4096
GRID_W = 64
FW = 1024
FG = 8
FGD = 128
GW = 1024
NH = 8
HD = 128
CONV_K = 5
N_DIR = 2
NE = 16
EFF = 1536
EPS = 1e-6
GC = 128
NU = N_DIR * NH
LANES = 128
VMEM_LIMIT = 56 * 1024 * 1024


def _sigmoid(x):
    return 1.0 / (1.0 + jnp.exp(-x))


def _silu(x):
    return x * _sigmoid(x)


def _softplus(x):
    return jnp.maximum(x, 0.0) + jnp.log(1.0 + jnp.exp(-jnp.abs(x)))


def _dot(a, b):
    return jnp.dot(a, b, preferred_element_type=F32)


def _dot_nt(a, b):
    return lax.dot_general(a, b, (((1,), (1,)), ((), ())), preferred_element_type=F32)


def _bmm(a, b):
    return jnp.einsum('uij,ujk->uik', a.astype(BF16), b.astype(BF16), preferred_element_type=F32)


def _bmm_nt(a, b):
    return jnp.einsum('uik,ujk->uij', a, b, preferred_element_type=F32)


def _cparams(sem, vmem=VMEM_LIMIT):
    return pltpu.CompilerParams(dimension_semantics=sem, vmem_limit_bytes=vmem)


def _mod_kernel(c_ref, w_ref, b_ref, o_ref):
    s = _silu(c_ref[...]).astype(BF16)
    o_ref[...] = _dot(s, w_ref[...].astype(BF16)) + b_ref[...]


def _mod_call(cc, w_mod, b_mod):
    tn = 1024
    n = w_mod.shape[1]
    return pl.pallas_call(
        _mod_kernel,
        grid=(n // tn,),
        in_specs=[pl.BlockSpec((8, D), lambda j: (0, 0)),
                  pl.BlockSpec((D, tn), lambda j: (0, j)),
                  pl.BlockSpec((1, tn), lambda j: (0, j))],
        out_specs=pl.BlockSpec((8, tn), lambda j: (0, j)),
        out_shape=jax.ShapeDtypeStruct((8, n), F32),
        compiler_params=_cparams(("parallel",)),
        name="mod",
    )(cc, w_mod, b_mod)


def _inproj_kernel(x_ref, g_ref, sh_ref, sc_ref, w_ref, wab_ref, o_ref, ab_ref, h_scr):
    @pl.when(pl.program_id(2) == 0)
    def _():
        x = x_ref[0]
        ms = jnp.mean(x * x, axis=-1, keepdims=True)
        y = x * lax.rsqrt(ms + EPS) * g_ref[...]
        h = (y * (1.0 + sc_ref[0]) + sh_ref[0]).astype(BF16)
        h_scr[...] = h
        ab_ref[0] = _dot(h, wab_ref[...])

    o_ref[0] = _dot(h_scr[...], w_ref[...]).astype(o_ref.dtype)


def _inproj_call(x, g, sh, sc, w_main, w_ab, tm, col0, ncol):
    b, t, _ = x.shape
    tn = 1024
    return pl.pallas_call(
        _inproj_kernel,
        grid=(b, t // tm, ncol),
        in_specs=[pl.BlockSpec((1, tm, D), lambda bi, i, j: (bi, i, 0)),
                  pl.BlockSpec((1, D), lambda bi, i, j: (0, 0)),
                  pl.BlockSpec((1, 1, D), lambda bi, i, j: (bi, 0, 0)),
                  pl.BlockSpec((1, 1, D), lambda bi, i, j: (bi, 0, 0)),
                  pl.BlockSpec((D, tn), lambda bi, i, j: (0, j + col0)),
                  pl.BlockSpec((D, LANES), lambda bi, i, j: (0, 0))],
        out_specs=[pl.BlockSpec((1, tm, tn), lambda bi, i, j: (bi, i, j)),
                   pl.BlockSpec((1, tm, LANES), lambda bi, i, j: (bi, i, 0))],
        out_shape=[jax.ShapeDtypeStruct((b, t, ncol * tn), BF16),
                   jax.ShapeDtypeStruct((b, t, LANES), F32)],
        scratch_shapes=[pltpu.VMEM((tm, D), BF16)],
        compiler_params=_cparams(("parallel", "parallel", "arbitrary")),
        name="inproj",
    )(x, g, sh, sc, w_main, w_ab)


def _conv_kernel(x_ref, w_ref, o_ref, *, row_len):
    j = pl.program_id(2)
    half = CONV_K // 2
    rr = lax.broadcasted_iota(I32, (row_len, row_len), 0)
    cc = lax.broadcasted_iota(I32, (row_len, row_len), 1)
    taps = [t for t in range(CONV_K) if t != half]
    shifts = jnp.concatenate([(cc == rr + (t - half)).astype(BF16) for t in taps], axis=0)
    qscale = jnp.where(j == 0, HD ** -0.5, 1.0).astype(F32)

    def group(g, carry):
        r0 = pl.multiple_of(g * row_len, row_len)
        xb = x_ref[0, pl.ds(r0, row_len), :]
        sh = _dot(shifts, xb)
        for h in range(NH):
            cols = slice(h * HD, (h + 1) * HD)
            acc = xb[:, cols].astype(F32) * w_ref[half:half + 1, cols]
            for ti, tap in enumerate(taps):
                acc = acc + sh[ti * row_len:(ti + 1) * row_len, cols] * w_ref[tap:tap + 1, cols]
            y = _silu(acc)
            ss = jnp.sum(y * y, axis=-1, keepdims=True)
            nrm = jnp.where(j < 2, lax.rsqrt(ss + EPS) * qscale, 1.0)
            o_ref[0, pl.ds(r0, row_len), cols] = (y * nrm).astype(o_ref.dtype)
        return carry

    lax.fori_loop(0, x_ref.shape[1] // row_len, group, 0, unroll=True)


def _conv_call(p, conv_w8, tt, row_len, col0):
    b, t, _ = p.shape
    return pl.pallas_call(
        functools.partial(_conv_kernel, row_len=row_len),
        grid=(b, t // tt, 3),
        in_specs=[pl.BlockSpec((1, tt, GW), lambda bi, i, j: (bi, i, j + col0)),
                  pl.BlockSpec((8, GW), lambda bi, i, j: (0, j))],
        out_specs=pl.BlockSpec((1, tt, GW), lambda bi, i, j: (bi, i, j)),
        out_shape=jax.ShapeDtypeStruct((b, t, 3 * GW), BF16),
        compiler_params=_cparams(("parallel", "parallel", "arbitrary")),
        name="conv",
    )(p, conv_w8)


def _gdn_kernel(qf_ref, kf_ref, vf_ref, abf_ref, qb_ref, kb_ref, vb_ref, abb_ref, alog_ref, dtb_ref, s0_ref,
                of_ref, ob_ref, sout_ref, s_scr):
    n = pl.program_id(1)

    @pl.when(n == 0)
    def _():
        s_scr[...] = s0_ref[0]

    row = lax.broadcasted_iota(I32, (GC, GC), 0)
    col = lax.broadcasted_iota(I32, (GC, GC), 1)
    eye = (row == col).astype(F32)
    m_parts, rhs_parts, qd_parts, kdt_parts, qk_parts, gt_parts = [], [], [], [], [], []
    for d in range(N_DIR):
        q_ref, k_ref, v_ref, ab_ref = ((qf_ref, kf_ref, vf_ref, abf_ref) if d == 0
                                       else (qb_ref, kb_ref, vb_ref, abb_ref))
        incl = (col <= row) if d == 0 else (col >= row)
        strict = (col < row) if d == 0 else (col > row)
        ab = ab_ref[0]
        g_all = -jnp.exp(alog_ref[...]) * _softplus(ab + dtb_ref[...])
        beta_all = _sigmoid(ab)
        gc_all = jnp.dot(incl.astype(F32), g_all, preferred_element_type=F32,
                         precision=lax.Precision.HIGHEST)
        gc_t = gc_all.T
        last = GC - 1 if d == 0 else 0
        c0 = d * NH
        cb = N_DIR * NH + d * NH
        heads = range(NH)
        gcol = jnp.stack([gc_all[:, c0 + h:c0 + h + 1] for h in heads])
        grow = jnp.stack([gc_t[c0 + h:c0 + h + 1, :] for h in heads])
        glast = jnp.stack([gc_all[last:last + 1, c0 + h:c0 + h + 1] for h in heads])
        beta = jnp.stack([beta_all[:, cb + h:cb + h + 1] for h in heads])
        q = jnp.stack([q_ref[0, :, h * HD:(h + 1) * HD] for h in heads]).astype(F32)
        k = jnp.stack([k_ref[0, :, h * HD:(h + 1) * HD] for h in heads])
        kf = k.astype(F32)
        v = jnp.stack([v_ref[0, :, h * HD:(h + 1) * HD] for h in heads]).astype(F32)
        decay = jnp.where(incl[None], jnp.exp(jnp.where(incl[None], gcol - grow, 0.0)), 0.0)
        egc = jnp.exp(gcol)
        kbeta = kf * beta
        a2 = _bmm_nt(jnp.concatenate([kbeta, q], axis=1).astype(BF16), k)
        m_parts.append(jnp.where(strict[None], a2[:, :GC] * decay, 0.0))
        rhs_parts.append(jnp.concatenate([v * beta, kbeta * egc], axis=2).astype(BF16))
        qd_parts.append((q * egc).astype(BF16))
        kdt_parts.append(jnp.swapaxes(kf * jnp.exp(glast - gcol), 1, 2).astype(BF16))
        qk_parts.append((a2[:, GC:] * decay).astype(BF16))
        gt_parts.append(jnp.exp(glast))

    cat = lambda parts: jnp.concatenate(parts, axis=0)
    m, rhs, qd, kdt, qk, gtot = (cat(t) for t in (m_parts, rhs_parts, qd_parts, kdt_parts, qk_parts, gt_parts))
    blk = lambda s: ((row >> int(math.log2(s))) == (col >> int(math.log2(s))))[None]
    mb = jnp.where(blk(8), m, 0.0)
    p = eye[None] - mb
    m2 = _bmm(mb, mb)
    p = p + _bmm(p, m2)
    p = p + _bmm(p, _bmm(m2, m2))
    s = 8
    while s < GC:
        cpart = jnp.where(jnp.logical_and(blk(2 * s), jnp.logical_not(blk(s))), m, 0.0)
        p = p - _bmm(p, _bmm(cpart, p))
        s *= 2
    uw = _bmm(p, rhs)

    st = s_scr[...]
    r = _bmm(jnp.concatenate([uw[:, :, HD:].astype(BF16), qd], axis=1), st)
    vb = (uw[:, :, :HD] - r[:, :GC]).astype(BF16)
    o = r[:, GC:] + _bmm(qk, vb)
    s_scr[...] = st * gtot + _bmm(kdt, vb)
    for un in range(NU):
        h = un % NH
        if un < NH:
            of_ref[0, :, h * HD:(h + 1) * HD] = o[un]
        else:
            ob_ref[0, :, h * HD:(h + 1) * HD] = o[un]

    @pl.when(n == pl.num_programs(1) - 1)
    def _():
        sout_ref[0] = s_scr[...]


def _gdn_call(qkv, ab, alog_row, dtb_row, s0):
    b, t, _ = qkv.shape
    nc = t // GC
    blk = lambda c, rev: pl.BlockSpec((1, GC, GW), (lambda bi, n: (bi, nc - 1 - n, c)) if rev
                                      else (lambda bi, n: (bi, n, c)))
    abblk = lambda rev: pl.BlockSpec((1, GC, LANES), (lambda bi, n: (bi, nc - 1 - n, 0)) if rev
                                     else (lambda bi, n: (bi, n, 0)))
    sblk = pl.BlockSpec((1, NU, HD, HD), lambda bi, n: (bi, 0, 0, 0))
    return pl.pallas_call(
        _gdn_kernel,
        grid=(b, nc),
        in_specs=[blk(0, False), blk(1, False), blk(2, False), abblk(False),
                  blk(0, True), blk(1, True), blk(2, True), abblk(True),
                  pl.BlockSpec((1, LANES), lambda bi, n: (0, 0)),
                  pl.BlockSpec((1, LANES), lambda bi, n: (0, 0)),
                  sblk],
        out_specs=[pl.BlockSpec((1, GC, GW), lambda bi, n: (bi, n, 0)),
                   pl.BlockSpec((1, GC, GW), lambda bi, n: (bi, nc - 1 - n, 0)),
                   sblk],
        out_shape=[jax.ShapeDtypeStruct((b, t, GW), F32),
                   jax.ShapeDtypeStruct((b, t, GW), F32),
                   jax.ShapeDtypeStruct((b, NU, HD, HD), F32)],
        scratch_shapes=[pltpu.VMEM((NU, HD, HD), F32)],
        compiler_params=_cparams(("parallel", "arbitrary")),
        name="gdn",
    )(qkv, qkv, qkv, ab, qkv, qkv, qkv, ab, alog_row, dtb_row, s0)


def _dft_tables():
    r = GRID_W
    a = 2.0 * np.pi * np.outer(np.arange(r), np.arange(r)) / r
    c64, s64 = np.cos(a), np.sin(a)
    ac = 2.0 * np.pi * np.outer(np.arange(FGD), np.arange(FGD)) / FGD
    cc, sc = np.cos(ac), np.sin(ac)
    at = (2.0 * np.pi * np.outer(np.arange(r), np.arange(r)) / (r * r)).reshape(r * r, 1)
    chan = np.concatenate([cc, -sc], axis=1)
    row_cs = np.concatenate([c64, s64], axis=0)
    col_re = np.concatenate([c64, s64], axis=1)
    as_bf16 = lambda t: jnp.asarray(t, F32).astype(BF16)
    return (as_bf16(chan), as_bf16(row_cs), as_bf16(col_re),
            jnp.asarray(np.cos(at), F32), jnp.asarray(np.sin(at), F32))


def _fourier_kernel(x_ref, chan_ref, rowcs_ref, colre_ref, twc_ref, tws_ref, o_ref, u_scr, v_scr, y_scr):
    r = GRID_W

    def swap(t):
        return jnp.swapaxes(t.reshape(r, r, FGD), 0, 1).reshape(r * r, FGD)

    u = _dot(x_ref[0], chan_ref[...])
    u_scr[0] = swap(u[:, :FGD])
    u_scr[1] = swap(u[:, FGD:])

    def over_n2(n1, carry):
        rows = pl.ds(pl.multiple_of(n1 * r, r), r)
        x = jnp.concatenate([u_scr[0, rows, :], u_scr[1, rows, :]], axis=1).astype(BF16)
        cs = _dot(rowcs_ref[...], x)
        vr = cs[:r, :FGD] + cs[r:, FGD:]
        vi = cs[:r, FGD:] - cs[r:, :FGD]
        tc = twc_ref[rows, :]
        ts = tws_ref[rows, :]
        v_scr[0, rows, :] = vr * tc + vi * ts
        v_scr[1, rows, :] = vi * tc - vr * ts
        return carry

    lax.fori_loop(0, r, over_n2, 0, unroll=16)
    u_scr[0] = swap(v_scr[0])
    u_scr[1] = swap(v_scr[1])

    def over_n1(k2, carry):
        rows = pl.ds(pl.multiple_of(k2 * r, r), r)
        st = jnp.concatenate([u_scr[0, rows, :], u_scr[1, rows, :]], axis=0).astype(BF16)
        y_scr[rows, :] = _dot(colre_ref[...], st) * (1.0 / math.sqrt(SEQ_N * FGD))
        return carry

    lax.fori_loop(0, r, over_n1, 0, unroll=16)
    o_ref[0] = swap(y_scr[...]).astype(o_ref.dtype)


def _fourier_call(p_main):
    b, n, _ = p_main.shape
    r = GRID_W
    chan, rowcs, colre, twc, tws = _dft_tables()
    twc = jnp.broadcast_to(twc, (n, LANES))
    tws = jnp.broadcast_to(tws, (n, LANES))
    const = lambda shape: pl.BlockSpec(shape, lambda bi, g: (0, 0))
    return pl.pallas_call(
        _fourier_kernel,
        grid=(b, FG),
        in_specs=[pl.BlockSpec((1, n, FGD), lambda bi, g: (bi, 0, g)),
                  const((FGD, 2 * FGD)), const((2 * r, r)), const((r, 2 * r)),
                  const((n, LANES)), const((n, LANES))],
        out_specs=pl.BlockSpec((1, n, FGD), lambda bi, g: (bi, 0, g)),
        out_shape=jax.ShapeDtypeStruct((b, n, FW), BF16),
        scratch_shapes=[pltpu.VMEM((2, n, FGD), F32), pltpu.VMEM((2, n, FGD), F32), pltpu.VMEM((n, FGD), F32)],
        compiler_params=_cparams(("parallel", "parallel")),
        name="fourier",
    )(p_main, chan, rowcs, colre, twc, tws)


def _outproj_kernel(yf_ref, of_ref, ob_ref, z_ref, x_ref, wout_ref, gnw_ref, gt1_ref, g2_ref, sh2_ref, sc2_ref,
                    wr_ref, x1_ref, hx_ref, lg_ref):
    sub = 128
    for rb in range(x_ref.shape[1] // sub):
        rows = slice(rb * sub, (rb + 1) * sub)
        parts = [yf_ref[0, rows, :]]
        for h in range(NH):
            cols = slice(h * HD, (h + 1) * HD)
            oh = of_ref[0, rows, cols] + ob_ref[0, rows, cols]
            ms = jnp.mean(oh * oh, axis=-1, keepdims=True)
            on = oh * lax.rsqrt(ms + EPS) * gnw_ref[...]
            parts.append((on * _silu(z_ref[0, rows, cols].astype(F32))).astype(BF16))
        mix = jnp.concatenate(parts, axis=1)
        x1 = x_ref[0, rows, :] + gt1_ref[0] * _dot(mix, wout_ref[...])
        x1_ref[0, rows, :] = x1
        ms = jnp.mean(x1 * x1, axis=-1, keepdims=True)
        hx = x1 * lax.rsqrt(ms + EPS) * g2_ref[...] * (1.0 + sc2_ref[0]) + sh2_ref[0]
        hx_ref[0, rows, :] = hx
        hx_hi = hx.astype(BF16)
        hx_lo = (hx - hx_hi.astype(F32)).astype(BF16)
        l2 = _dot(hx_hi, wr_ref[...])
        lg = l2[:, :LANES] + l2[:, LANES:] + _dot(hx_lo, wr_ref[:, :LANES])
        lg_ref[0, :, rows] = lg.T[:NE, :]


def _outproj_call(yf, o_f, o_b, p_main, x, w_out, gnw, gt1, g2, sh2, sc2, w_r):
    b, t, _ = x.shape
    tm = 256
    zcol = p_main.shape[2] // GW - 1
    row = lambda c: pl.BlockSpec((1, tm, c), lambda bi, i: (bi, i, 0))
    vec = pl.BlockSpec((1, 1, D), lambda bi, i: (bi, 0, 0))
    return pl.pallas_call(
        _outproj_kernel,
        grid=(b, t // tm),
        in_specs=[row(FW), row(GW), row(GW),
                  pl.BlockSpec((1, tm, GW), lambda bi, i: (bi, i, zcol)),
                  row(D),
                  pl.BlockSpec((D, D), lambda bi, i: (0, 0)),
                  pl.BlockSpec((1, HD), lambda bi, i: (0, 0)),
                  vec,
                  pl.BlockSpec((1, D), lambda bi, i: (0, 0)),
                  vec, vec,
                  pl.BlockSpec((D, 2 * LANES), lambda bi, i: (0, 0))],
        out_specs=[row(D), row(D), pl.BlockSpec((1, NE, tm), lambda bi, i: (bi, 0, i))],
        out_shape=[jax.ShapeDtypeStruct((b, t, D), F32),
                   jax.ShapeDtypeStruct((b, t, D), F32),
                   jax.ShapeDtypeStruct((b, NE, t), F32)],
        compiler_params=_cparams(("parallel", "parallel")),
        name="outproj",
    )(yf, o_f, o_b, p_main, x, w_out, gnw, gt1, g2, sh2, sc2, w_r)


def _lane_cumsum(x):
    n = x.shape[-1]
    lane = lax.broadcasted_iota(I32, x.shape, x.ndim - 1)
    sh = 1
    while sh < n:
        x = x + jnp.where(lane >= sh, pltpu.roll(x, shift=sh, axis=x.ndim - 1), 0.0)
        sh *= 2
    return x


def _topk_kernel(lg_ref, idx_ref, val_ref, rsel_scr, aff_scr, *, cap):
    lg = lg_ref[0]
    t = lg.shape[1]
    mx = jnp.max(lg, axis=0, keepdims=True)
    ex = jnp.exp(lg - mx)
    aff = ex / jnp.sum(ex, axis=0, keepdims=True)

    def search(i, cur):
        cand = cur | (1 << (30 - i))
        cnt = jnp.sum((aff >= pltpu.bitcast(cand, F32)).astype(F32), axis=1, keepdims=True)
        return jnp.where(cnt >= cap, cand, cur)

    thr = pltpu.bitcast(lax.fori_loop(0, 31, search, jnp.zeros((NE, 1), I32)), F32)
    gt = aff > thr
    eq = aff == thr
    n_gt = jnp.sum(gt.astype(F32), axis=1, keepdims=True)
    eq_rank = _lane_cumsum(eq.astype(F32))
    sel = jnp.logical_or(gt, jnp.logical_and(eq, eq_rank <= cap - n_gt))
    rank = _lane_cumsum(sel.astype(F32))
    rsel_scr[...] = jnp.where(sel, rank, 0.0)
    aff_scr[...] = aff
    idx_ref[0] = jnp.zeros((cap, LANES), I32)
    val_ref[0] = jnp.zeros((cap, LANES), F32)
    tpos = lax.broadcasted_iota(I32, (1, t), 1).astype(F32)
    sblk = 64
    lane = lax.broadcasted_iota(I32, (sblk, LANES), 1)

    def per_expert(e, carry):
        rrow = rsel_scr[pl.ds(e, 1), :]
        arow = aff_scr[pl.ds(e, 1), :]
        for sb in range(cap // sblk):
            slot = (lax.broadcasted_iota(I32, (sblk, 1), 0) + (sb * sblk + 1)).astype(F32)
            hit = rrow == slot
            ic = jnp.sum(jnp.where(hit, tpos, 0.0), axis=1, keepdims=True)
            vc = jnp.sum(jnp.where(hit, arow, 0.0), axis=1, keepdims=True)
            rows = slice(sb * sblk, (sb + 1) * sblk)
            idx_ref[0, rows, :] = jnp.where(lane == e, ic.astype(I32), idx_ref[0, rows, :])
            val_ref[0, rows, :] = jnp.where(lane == e, vc, val_ref[0, rows, :])
        return carry

    lax.fori_loop(0, NE, per_expert, 0)


def _topk_call(lg_t, cap):
    b, _, t = lg_t.shape
    return pl.pallas_call(
        functools.partial(_topk_kernel, cap=cap),
        grid=(b,),
        in_specs=[pl.BlockSpec((1, NE, t), lambda bi: (bi, 0, 0))],
        out_specs=[pl.BlockSpec((1, cap, LANES), lambda bi: (bi, 0, 0)),
                   pl.BlockSpec((1, cap, LANES), lambda bi: (bi, 0, 0))],
        out_shape=[jax.ShapeDtypeStruct((b, cap, LANES), I32),
                   jax.ShapeDtypeStruct((b, cap, LANES), F32)],
        scratch_shapes=[pltpu.VMEM((NE, t), F32), pltpu.VMEM((NE, t), F32)],
        compiler_params=_cparams(("parallel",)),
        name="topk",
    )(lg_t)


MOE_TF = 256
MOE_TN = 256
MOE_MC = 512
MOE_NF = EFF // MOE_TF
MOE_NN = D // MOE_TN
MOE_GC = 352
MOE_GPAD = 32


def _moe_ffn_kernel(idx_ref, hx_hbm, val_ref, gt2_ref, wg_ref, wu_ref, wd_ref, y_ref,
                    xg_scr, hid_scr, stg_scr, vcol_scr, gsem, *, rows):
    e = pl.program_id(0)
    s = pl.program_id(1)
    cur = e % 2
    nxt = jnp.minimum(e + 1, pl.num_programs(0) - 1)

    def issue(expert, c):
        base = expert * rows + c * MOE_GC
        for r in range(MOE_GC):
            pltpu.make_async_copy(hx_hbm.at[pl.ds(idx_ref[base + r], 1), :],
                                  stg_scr.at[pl.ds(r, 1), :], gsem).start()

    def land(c, buf):
        pltpu.make_async_copy(hx_hbm.at[pl.ds(0, MOE_GC), :], stg_scr, gsem).wait()
        r0 = pl.multiple_of(c * MOE_GC, MOE_GPAD)
        xg_scr[buf, pl.ds(r0, MOE_GC), :] = stg_scr[...].astype(BF16)

    @pl.when(jnp.logical_and(e == 0, s == 0))
    def _():
        for c in range(MOE_NF):
            issue(0, c)
            land(c, 0)

    @pl.when(jnp.logical_and(s >= 1, s <= MOE_NF))
    def _():
        land(s - 1, 1 - cur)

    @pl.when(s < MOE_NF)
    def _():
        xs = xg_scr[cur, 0:rows, :]
        hid = (_silu(_dot(xs, wg_ref[0].astype(BF16))) * _dot(xs, wu_ref[0].astype(BF16))).astype(BF16)
        hid_scr[:, pl.ds(pl.multiple_of(s * MOE_TF, MOE_TF), MOE_TF)] = hid
        issue(nxt, s)

    @pl.when(s == MOE_NF)
    def _():
        vcol_scr[...] = jnp.broadcast_to(val_ref[0], (LANES, rows)).T

    @pl.when(s >= MOE_NF)
    def _():
        y = _dot(hid_scr[...], wd_ref[0].astype(BF16))
        for mc in range(rows // MOE_MC):
            sl = slice(mc * MOE_MC, (mc + 1) * MOE_MC)
            y_ref[0, sl, :] = (y[sl] * vcol_scr[sl, 0:1] * gt2_ref[mc, 0]).astype(y_ref.dtype)


def _moe_ffn_call(idx_flat, hx2, vals, gt2, w_gate, w_up, w_down, cap):
    rows = vals.shape[2]
    rows_pad = MOE_NF * MOE_GC
    assert rows // MOE_MC == gt2.shape[0] and cap == MOE_MC and rows_pad >= rows and MOE_GC % MOE_GPAD == 0
    idx_flat = jnp.pad(idx_flat, (0, rows_pad - rows))
    gt2t = gt2.reshape(gt2.shape[0], MOE_NN, 1, MOE_TN)
    ph1 = lambda s: jnp.minimum(s, MOE_NF - 1)
    ph2 = lambda s: jnp.maximum(s - MOE_NF, 0)
    grid_spec = pltpu.PrefetchScalarGridSpec(
        num_scalar_prefetch=1,
        grid=(NE, MOE_NF + MOE_NN),
        in_specs=[pl.BlockSpec(memory_space=pl.ANY),
                  pl.BlockSpec((1, 1, rows), lambda e, s, idx: (e, 0, 0)),
                  pl.BlockSpec((gt2.shape[0], 1, 1, MOE_TN), lambda e, s, idx: (0, ph2(s), 0, 0)),
                  pl.BlockSpec((1, D, MOE_TF), lambda e, s, idx: (e, 0, ph1(s))),
                  pl.BlockSpec((1, D, MOE_TF), lambda e, s, idx: (e, 0, ph1(s))),
                  pl.BlockSpec((1, EFF, MOE_TN), lambda e, s, idx: (e, 0, ph2(s)))],
        out_specs=pl.BlockSpec((1, rows, MOE_TN), lambda e, s, idx: (e, 0, ph2(s))),
        scratch_shapes=[pltpu.VMEM((2, rows_pad, D), BF16),
                        pltpu.VMEM((rows, EFF), BF16),
                        pltpu.VMEM((MOE_GC, D), F32),
                        pltpu.VMEM((rows, LANES), F32),
                        pltpu.SemaphoreType.DMA(())],
    )
    return pl.pallas_call(
        functools.partial(_moe_ffn_kernel, rows=rows),
        grid_spec=grid_spec,
        out_shape=jax.ShapeDtypeStruct((NE, rows, D), BF16),
        compiler_params=_cparams(("arbitrary", "arbitrary")),
        name="moe_ffn",
    )(idx_flat, hx2, vals, gt2t, w_gate, w_up, w_down)


MOE_SC = 256


def _moe_scatter_kernel(idx_ref, y_ref, res_hbm, out_hbm, buf_scr, rsem, wsem, *, rows):
    del res_hbm
    e = pl.program_id(0)
    c = pl.program_id(1)
    nc = pl.num_programs(1)
    slot = c % 2

    def reads(chunk, sl):
        base = e * rows + chunk * MOE_SC
        for r in range(MOE_SC):
            pltpu.make_async_copy(out_hbm.at[pl.ds(idx_ref[base + r], 1), :],
                                  buf_scr.at[sl, pl.ds(r, 1), :], rsem.at[sl]).start(priority=r % 2)

    def writes(chunk, sl):
        base = e * rows + chunk * MOE_SC
        for r in range(MOE_SC):
            pltpu.make_async_copy(buf_scr.at[sl, pl.ds(r, 1), :],
                                  out_hbm.at[pl.ds(idx_ref[base + r], 1), :], wsem.at[sl]).start(priority=r % 2)

    def wait_all(sem, sl):
        pltpu.make_async_copy(out_hbm.at[pl.ds(0, MOE_SC), :], buf_scr.at[sl], sem.at[sl]).wait()

    @pl.when(c == 0)
    def _():
        reads(c, slot)

    wait_all(rsem, slot)

    @pl.when(c + 1 < nc)
    def _():
        @pl.when(c >= 1)
        def _():
            wait_all(wsem, 1 - slot)
        reads(c + 1, 1 - slot)

    buf_scr[slot] = buf_scr[slot] + y_ref[0].astype(F32)
    writes(c, slot)

    @pl.when(c == nc - 1)
    def _():
        wait_all(wsem, 1 - slot)
        wait_all(wsem, slot)


def _moe_scatter_call(idx_flat, y, x1):
    rows = y.shape[1]
    grid_spec = pltpu.PrefetchScalarGridSpec(
        num_scalar_prefetch=1,
        grid=(NE, rows // MOE_SC),
        in_specs=[pl.BlockSpec((1, MOE_SC, D), lambda e, c, idx: (e, c, 0)),
                  pl.BlockSpec(memory_space=pl.ANY)],
        out_specs=pl.BlockSpec(memory_space=pl.ANY),
        scratch_shapes=[pltpu.VMEM((2, MOE_SC, D), F32),
                        pltpu.SemaphoreType.DMA((2,)),
                        pltpu.SemaphoreType.DMA((2,))],
    )
    return pl.pallas_call(
        functools.partial(_moe_scatter_kernel, rows=rows),
        grid_spec=grid_spec,
        out_shape=jax.ShapeDtypeStruct(x1.shape, F32),
        input_output_aliases={2: 0},
        compiler_params=_cparams(("arbitrary", "arbitrary")),
        name="moe_scatter",
    )(idx_flat, y, x1)


def _final_kernel(x_ref, g_ref, o_ref):
    x = x_ref[...]
    ms = jnp.mean(x * x, axis=-1, keepdims=True)
    o_ref[...] = x * lax.rsqrt(ms + EPS) * g_ref[...]


def _final_call(x2d, g):
    m = x2d.shape[0]
    tm = 512
    return pl.pallas_call(
        _final_kernel,
        grid=(m // tm,),
        in_specs=[pl.BlockSpec((tm, D), lambda i: (i, 0)), pl.BlockSpec((1, D), lambda i: (0, 0))],
        out_specs=pl.BlockSpec((tm, D), lambda i: (i, 0)),
        out_shape=jax.ShapeDtypeStruct(x2d.shape, F32),
        compiler_params=_cparams(("parallel",)),
        name="final_norm",
    )(x2d, g)


def kernel(x, c, ctx, c_ctx, w_mod, b_mod, norm1_g, norm2_g, w_in, conv_w, a_log, dt_bias, gdn_norm_w, w_out,
           w_router, w_gate, w_up, w_down, norm_f):
    b, n, _ = x.shape
    nctx = ctx.shape[1]
    cap = 2 * n // NE
    i = 0

    cc = jnp.concatenate([c, c_ctx[None, :], jnp.zeros((8 - b - 1, D), F32)], axis=0)
    mod = _mod_call(cc, w_mod[i], b_mod[i][None, :])
    sh1, sc1, gt1, sh2, sc2, gt2 = [mod[:b, k * D:(k + 1) * D][:, None, :] for k in range(6)]
    sh1c = jnp.broadcast_to(mod[b:b + 1, 0:D][:, None, :], (b, 1, D))
    sc1c = jnp.broadcast_to(mod[b:b + 1, D:2 * D][:, None, :], (b, 1, D))

    nmain = FW + 4 * GW
    w_main = w_in[i][:, :nmain].astype(BF16)
    w_ab = jnp.pad(w_in[i][:, nmain:], ((0, 0), (0, LANES - 2 * N_DIR * NH))).astype(BF16)
    conv_w8 = jnp.pad(conv_w[i], ((0, 8 - CONV_K), (0, 0)))
    alog_row = jnp.pad(a_log[i].reshape(1, NU), ((0, 0), (0, LANES - NU)))
    dtb_row = jnp.pad(dt_bias[i].reshape(1, NU), ((0, 0), (0, LANES - NU)))
    g1 = norm1_g[i][None, :]

    p_ctx, ab_ctx = _inproj_call(ctx, g1, sh1c, sc1c, w_main, w_ab, nctx, 1, 3)
    qkv_ctx = _conv_call(p_ctx, conv_w8, nctx, nctx, 0)
    _, _, s_ctx = _gdn_call(qkv_ctx, ab_ctx, alog_row, dtb_row, jnp.zeros((b, NU, HD, HD), F32))

    p_main, ab = _inproj_call(x, g1, sh1, sc1, w_main, w_ab, 1024, 0, 5)
    qkv = _conv_call(p_main, conv_w8, 512, GRID_W, 1)
    o_f, o_b, _ = _gdn_call(qkv, ab, alog_row, dtb_row, s_ctx)
    yf = _fourier_call(p_main)

    w_r = jnp.pad(w_router[i], ((0, 0), (0, LANES - NE)))
    w_r_hi = w_r.astype(BF16)
    w_r = jnp.concatenate([w_r_hi, (w_r - w_r_hi.astype(F32)).astype(BF16)], axis=1)
    x1, hx2, lg_t = _outproj_call(yf, o_f, o_b, p_main, x, w_out[i].astype(BF16), gdn_norm_w[i][None, :], gt1,
                                  norm2_g[i][None, :], sh2, sc2, w_r)

    idx_c, val_c = _topk_call(lg_t, cap)
    idx = jnp.transpose(idx_c[:, :, :NE], (2, 0, 1))
    val = jnp.transpose(val_c[:, :, :NE], (2, 0, 1))
    idx_flat = (idx + (jnp.arange(b, dtype=I32) * n)[None, :, None]).reshape(NE * b * cap)
    vals = val.reshape(NE, 1, b * cap)

    y = _moe_ffn_call(idx_flat, hx2.reshape(b * n, D), vals, gt2, w_gate[i], w_up[i], w_down[i], cap)
    out = _moe_scatter_call(idx_flat, y, x1.reshape(b * n, D))
    return _final_call(out, norm_f[None, :]).reshape(b, n, D)
```

```python
import functools
import math

import numpy as np
import jax
import jax.numpy as jnp
from jax import lax
from jax.experimental import pallas as pl
from jax.experimental.pallas import tpu as pltpu

F32 = jnp.float32
BF16 = jnp.bfloat16
I32 = jnp.int32

D = 2048
SEQ_N = 4096
GRID_W = 64
FW = 1024
FG = 8
FGD = 128
GW = 1024
NH = 8
HD = 128
CONV_K = 5
N_DIR = 2
NE = 16
EFF = 1536
EPS = 1e-6
GC = 128
NU = N_DIR * NH
LANES = 128
VMEM_LIMIT = 56 * 1024 * 1024


def _sigmoid(x):
    return 1.0 / (1.0 + jnp.exp(-x))


def _silu(x):
    return x * _sigmoid(x)


def _softplus(x):
    return jnp.maximum(x, 0.0) + jnp.log(1.0 + jnp.exp(-jnp.abs(x)))


def _dot(a, b):
    return jnp.dot(a, b, preferred_element_type=F32)


def _dot_nt(a, b):
    return lax.dot_general(a, b, (((1,), (1,)), ((), ())), preferred_element_type=F32)


def _bmm(a, b):
    return jnp.einsum('uij,ujk->uik', a.astype(BF16), b.astype(BF16), preferred_element_type=F32)


def _bmm_nt(a, b):
    return jnp.einsum('uik,ujk->uij', a, b, preferred_element_type=F32)


def _cparams(sem, vmem=VMEM_LIMIT):
    return pltpu.CompilerParams(dimension_semantics=sem, vmem_limit_bytes=vmem)


def _mod_kernel(c_ref, w_ref, b_ref, o_ref):
    s = _silu(c_ref[...]).astype(BF16)
    o_ref[...] = _dot(s, w_ref[...].astype(BF16)) + b_ref[...]


def _mod_call(cc, w_mod, b_mod):
    tn = 1024
    n = w_mod.shape[1]
    return pl.pallas_call(
        _mod_kernel,
        grid=(n // tn,),
        in_specs=[pl.BlockSpec((8, D), lambda j: (0, 0)),
                  pl.BlockSpec((D, tn), lambda j: (0, j)),
                  pl.BlockSpec((1, tn), lambda j: (0, j))],
        out_specs=pl.BlockSpec((8, tn), lambda j: (0, j)),
        out_shape=jax.ShapeDtypeStruct((8, n), F32),
        compiler_params=_cparams(("parallel",)),
        name="mod",
    )(cc, w_mod, b_mod)


INPROJ_SUB = 256


def _inproj_kernel(x_ref, g_ref, sh_ref, sc_ref, w_ref, wab_ref, cw_ref, o_ref, ab_ref, h_scr, *, col0, row_len):
    c = pl.program_id(2) + col0

    @pl.when(pl.program_id(2) == 0)
    def _():
        x = x_ref[0]
        ms = jnp.mean(x * x, axis=-1, keepdims=True)
        y = x * lax.rsqrt(ms + EPS) * g_ref[...]
        h = (y * (1.0 + sc_ref[0]) + sh_ref[0]).astype(BF16)
        h_scr[...] = h
        ab_ref[0] = _dot(h, wab_ref[...])

    is_qkv = jnp.logical_and(c >= 1, c <= 3)

    @pl.when(jnp.logical_not(is_qkv))
    def _():
        o_ref[0] = _dot(h_scr[...], w_ref[...]).astype(o_ref.dtype)

    @pl.when(is_qkv)
    def _():
        half = CONV_K // 2
        qscale = jnp.where(c == 1, HD ** -0.5, 1.0).astype(F32)

        def shifted(x, off):
            zero = jnp.zeros((abs(off), HD), F32)
            return (jnp.concatenate([x[off:], zero], axis=0) if off > 0
                    else jnp.concatenate([zero, x[:off]], axis=0))

        for rb in range(x_ref.shape[1] // INPROJ_SUB):
            p = _dot(h_scr[rb * INPROJ_SUB:(rb + 1) * INPROJ_SUB, :], w_ref[...])
            for g in range(INPROJ_SUB // row_len):
                rows = slice(g * row_len, (g + 1) * row_len)
                for h in range(NH):
                    cols = slice(h * HD, (h + 1) * HD)
                    x = p[rows, cols]
                    acc = x * cw_ref[half:half + 1, cols]
                    for tap in range(CONV_K):
                        off = tap - half
                        if off != 0:
                            acc = acc + shifted(x, off) * cw_ref[tap:tap + 1, cols]
                    y = _silu(acc)
                    ss = jnp.sum(y * y, axis=-1, keepdims=True)
                    nrm = jnp.where(c < 3, lax.rsqrt(ss + EPS) * qscale, 1.0)
                    o_ref[0, rb * INPROJ_SUB + g * row_len:rb * INPROJ_SUB + (g + 1) * row_len, cols] = (
                        y * nrm).astype(o_ref.dtype)


def _inproj_call(x, g, sh, sc, w_main, w_ab, conv_w8, tm, col0, ncol, row_len):
    b, t, _ = x.shape
    tn = 1024
    assert tm % INPROJ_SUB == 0 and INPROJ_SUB % row_len == 0
    return pl.pallas_call(
        functools.partial(_inproj_kernel, col0=col0, row_len=row_len),
        grid=(b, t // tm, ncol),
        in_specs=[pl.BlockSpec((1, tm, D), lambda bi, i, j: (bi, i, 0)),
                  pl.BlockSpec((1, D), lambda bi, i, j: (0, 0)),
                  pl.BlockSpec((1, 1, D), lambda bi, i, j: (bi, 0, 0)),
                  pl.BlockSpec((1, 1, D), lambda bi, i, j: (bi, 0, 0)),
                  pl.BlockSpec((D, tn), lambda bi, i, j: (0, j + col0)),
                  pl.BlockSpec((D, LANES), lambda bi, i, j: (0, 0)),
                  pl.BlockSpec((8, GW), lambda bi, i, j: (0, jnp.clip(j + col0 - 1, 0, 2)))],
        out_specs=[pl.BlockSpec((1, tm, tn), lambda bi, i, j: (bi, i, j)),
                   pl.BlockSpec((1, tm, LANES), lambda bi, i, j: (bi, i, 0))],
        out_shape=[jax.ShapeDtypeStruct((b, t, ncol * tn), BF16),
                   jax.ShapeDtypeStruct((b, t, LANES), F32)],
        scratch_shapes=[pltpu.VMEM((tm, D), BF16)],
        compiler_params=_cparams(("parallel", "parallel", "arbitrary")),
        name="inproj",
    )(x, g, sh, sc, w_main, w_ab, conv_w8)


def _gdn_kernel(qf_ref, kf_ref, vf_ref, abf_ref, qb_ref, kb_ref, vb_ref, abb_ref, alog_ref, dtb_ref, s0_ref,
                of_ref, ob_ref, sout_ref, s_scr):
    n = pl.program_id(1)

    @pl.when(n == 0)
    def _():
        s_scr[...] = s0_ref[0]

    row = lax.broadcasted_iota(I32, (GC, GC), 0)
    col = lax.broadcasted_iota(I32, (GC, GC), 1)
    eye = (row == col).astype(F32)
    m_parts, rhs_parts, qd_parts, kdt_parts, qk_parts, gt_parts = [], [], [], [], [], []
    for d in range(N_DIR):
        q_ref, k_ref, v_ref, ab_ref = ((qf_ref, kf_ref, vf_ref, abf_ref) if d == 0
                                       else (qb_ref, kb_ref, vb_ref, abb_ref))
        incl = (col <= row) if d == 0 else (col >= row)
        strict = (col < row) if d == 0 else (col > row)
        ab = ab_ref[0]
        g_all = -jnp.exp(alog_ref[...]) * _softplus(ab + dtb_ref[...])
        beta_all = _sigmoid(ab)
        gc_all = jnp.dot(incl.astype(F32), g_all, preferred_element_type=F32,
                         precision=lax.Precision.HIGHEST)
        gc_t = gc_all.T
        last = GC - 1 if d == 0 else 0
        c0 = d * NH
        cb = N_DIR * NH + d * NH
        heads = range(NH)
        gcol = jnp.stack([gc_all[:, c0 + h:c0 + h + 1] for h in heads])
        grow = jnp.stack([gc_t[c0 + h:c0 + h + 1, :] for h in heads])
        glast = jnp.stack([gc_all[last:last + 1, c0 + h:c0 + h + 1] for h in heads])
        beta = jnp.stack([beta_all[:, cb + h:cb + h + 1] for h in heads])
        q = jnp.stack([q_ref[0, :, h * HD:(h + 1) * HD] for h in heads]).astype(F32)
        k = jnp.stack([k_ref[0, :, h * HD:(h + 1) * HD] for h in heads])
        kf = k.astype(F32)
        v = jnp.stack([v_ref[0, :, h * HD:(h + 1) * HD] for h in heads]).astype(F32)
        decay = jnp.where(incl[None], jnp.exp(jnp.where(incl[None], gcol - grow, 0.0)), 0.0)
        egc = jnp.exp(gcol)
        kbeta = kf * beta
        a2 = _bmm_nt(jnp.concatenate([kbeta, q], axis=1).astype(BF16), k)
        m_parts.append(jnp.where(strict[None], a2[:, :GC] * decay, 0.0))
        rhs_parts.append(jnp.concatenate([v * beta, kbeta * egc], axis=2).astype(BF16))
        qd_parts.append((q * egc).astype(BF16))
        kdt_parts.append(jnp.swapaxes(kf * jnp.exp(glast - gcol), 1, 2).astype(BF16))
        qk_parts.append((a2[:, GC:] * decay).astype(BF16))
        gt_parts.append(jnp.exp(glast))

    cat = lambda parts: jnp.concatenate(parts, axis=0)
    m, rhs, qd, kdt, qk, gtot = (cat(t) for t in (m_parts, rhs_parts, qd_parts, kdt_parts, qk_parts, gt_parts))
    blk = lambda s: ((row >> int(math.log2(s))) == (col >> int(math.log2(s))))[None]
    mb = jnp.where(blk(8), m, 0.0)
    p = eye[None] - mb
    m2 = _bmm(mb, mb)
    p = p + _bmm(p, m2)
    p = p + _bmm(p, _bmm(m2, m2))
    s = 8
    while s < GC:
        cpart = jnp.where(jnp.logical_and(blk(2 * s), jnp.logical_not(blk(s))), m, 0.0)
        p = p - _bmm(p, _bmm(cpart, p))
        s *= 2
    uw = _bmm(p, rhs)

    st = s_scr[...]
    r = _bmm(jnp.concatenate([uw[:, :, HD:].astype(BF16), qd], axis=1), st)
    vb = (uw[:, :, :HD] - r[:, :GC]).astype(BF16)
    o = r[:, GC:] + _bmm(qk, vb)
    s_scr[...] = st * gtot + _bmm(kdt, vb)
    for un in range(NU):
        h = un % NH
        if un < NH:
            of_ref[0, :, h * HD:(h + 1) * HD] = o[un]
        else:
            ob_ref[0, :, h * HD:(h + 1) * HD] = o[un]

    @pl.when(n == pl.num_programs(1) - 1)
    def _():
        sout_ref[0] = s_scr[...]


def _gdn_call(qkv, col0, ab, alog_row, dtb_row, s0):
    b, t, _ = qkv.shape
    nc = t // GC
    blk = lambda c, rev: pl.BlockSpec((1, GC, GW), (lambda bi, n: (bi, nc - 1 - n, c + col0)) if rev
                                      else (lambda bi, n: (bi, n, c + col0)))
    abblk = lambda rev: pl.BlockSpec((1, GC, LANES), (lambda bi, n: (bi, nc - 1 - n, 0)) if rev
                                     else (lambda bi, n: (bi, n, 0)))
    sblk = pl.BlockSpec((1, NU, HD, HD), lambda bi, n: (bi, 0, 0, 0))
    return pl.pallas_call(
        _gdn_kernel,
        grid=(b, nc),
        in_specs=[blk(0, False), blk(1, False), blk(2, False), abblk(False),
                  blk(0, True), blk(1, True), blk(2, True), abblk(True),
                  pl.BlockSpec((1, LANES), lambda bi, n: (0, 0)),
                  pl.BlockSpec((1, LANES), lambda bi, n: (0, 0)),
                  sblk],
        out_specs=[pl.BlockSpec((1, GC, GW), lambda bi, n: (bi, n, 0)),
                   pl.BlockSpec((1, GC, GW), lambda bi, n: (bi, nc - 1 - n, 0)),
                   sblk],
        out_shape=[jax.ShapeDtypeStruct((b, t, GW), F32),
                   jax.ShapeDtypeStruct((b, t, GW), F32),
                   jax.ShapeDtypeStruct((b, NU, HD, HD), F32)],
        scratch_shapes=[pltpu.VMEM((NU, HD, HD), F32)],
        compiler_params=_cparams(("parallel", "arbitrary")),
        name="gdn",
    )(qkv, qkv, qkv, ab, qkv, qkv, qkv, ab, alog_row, dtb_row, s0)


def _dft_tables():
    r = GRID_W
    a = 2.0 * np.pi * np.outer(np.arange(r), np.arange(r)) / r
    c64, s64 = np.cos(a), np.sin(a)
    ac = 2.0 * np.pi * np.outer(np.arange(FGD), np.arange(FGD)) / FGD
    cc, sc = np.cos(ac), np.sin(ac)
    at = (2.0 * np.pi * np.outer(np.arange(r), np.arange(r)) / (r * r)).reshape(r * r, 1)
    chan = np.concatenate([cc, -sc], axis=1)
    row_cs = np.concatenate([c64, s64], axis=0)
    col_re = np.concatenate([c64, s64], axis=1)
    as_bf16 = lambda t: jnp.asarray(t, F32).astype(BF16)
    return (as_bf16(chan), as_bf16(row_cs), as_bf16(col_re),
            jnp.asarray(np.cos(at), F32), jnp.asarray(np.sin(at), F32))


def _fourier_kernel(x_ref, chan_ref, rowcs_ref, colre_ref, twc_ref, tws_ref, o_ref, u_scr, v_scr, y_scr):
    r = GRID_W

    def swap(t):
        return jnp.swapaxes(t.reshape(r, r, FGD), 0, 1).reshape(r * r, FGD)

    u = _dot(x_ref[0], chan_ref[...])
    u_scr[0] = swap(u[:, :FGD])
    u_scr[1] = swap(u[:, FGD:])

    def over_n2(n1, carry):
        rows = pl.ds(pl.multiple_of(n1 * r, r), r)
        x = jnp.concatenate([u_scr[0, rows, :], u_scr[1, rows, :]], axis=1).astype(BF16)
        cs = _dot(rowcs_ref[...], x)
        vr = cs[:r, :FGD] + cs[r:, FGD:]
        vi = cs[:r, FGD:] - cs[r:, :FGD]
        tc = twc_ref[rows, :]
        ts = tws_ref[rows, :]
        v_scr[0, rows, :] = vr * tc + vi * ts
        v_scr[1, rows, :] = vi * tc - vr * ts
        return carry

    lax.fori_loop(0, r, over_n2, 0, unroll=16)
    u_scr[0] = swap(v_scr[0])
    u_scr[1] = swap(v_scr[1])

    def over_n1(k2, carry):
        rows = pl.ds(pl.multiple_of(k2 * r, r), r)
        st = jnp.concatenate([u_scr[0, rows, :], u_scr[1, rows, :]], axis=0).astype(BF16)
        y_scr[rows, :] = _dot(colre_ref[...], st) * (1.0 / math.sqrt(SEQ_N * FGD))
        return carry

    lax.fori_loop(0, r, over_n1, 0, unroll=16)
    o_ref[0] = swap(y_scr[...]).astype(o_ref.dtype)


def _fourier_call(p_main):
    b, n, _ = p_main.shape
    r = GRID_W
    chan, rowcs, colre, twc, tws = _dft_tables()
    twc = jnp.broadcast_to(twc, (n, LANES))
    tws = jnp.broadcast_to(tws, (n, LANES))
    const = lambda shape: pl.BlockSpec(shape, lambda bi, g: (0, 0))
    return pl.pallas_call(
        _fourier_kernel,
        grid=(b, FG),
        in_specs=[pl.BlockSpec((1, n, FGD), lambda bi, g: (bi, 0, g)),
                  const((FGD, 2 * FGD)), const((2 * r, r)), const((r, 2 * r)),
                  const((n, LANES)), const((n, LANES))],
        out_specs=pl.BlockSpec((1, n, FGD), lambda bi, g: (bi, 0, g)),
        out_shape=jax.ShapeDtypeStruct((b, n, FW), BF16),
        scratch_shapes=[pltpu.VMEM((2, n, FGD), F32), pltpu.VMEM((2, n, FGD), F32), pltpu.VMEM((n, FGD), F32)],
        compiler_params=_cparams(("parallel", "parallel")),
        name="fourier",
    )(p_main, chan, rowcs, colre, twc, tws)


def _outproj_kernel(yf_ref, of_ref, ob_ref, z_ref, x_ref, wout_ref, gnw_ref, gt1_ref, g2_ref, sh2_ref, sc2_ref,
                    wr_ref, x1_ref, hx_ref, lg_ref):
    sub = 128
    for rb in range(x_ref.shape[1] // sub):
        rows = slice(rb * sub, (rb + 1) * sub)
        parts = [yf_ref[0, rows, :]]
        for h in range(NH):
            cols = slice(h * HD, (h + 1) * HD)
            oh = of_ref[0, rows, cols] + ob_ref[0, rows, cols]
            ms = jnp.mean(oh * oh, axis=-1, keepdims=True)
            on = oh * lax.rsqrt(ms + EPS) * gnw_ref[...]
            parts.append((on * _silu(z_ref[0, rows, cols].astype(F32))).astype(BF16))
        mix = jnp.concatenate(parts, axis=1)
        x1 = x_ref[0, rows, :] + gt1_ref[0] * _dot(mix, wout_ref[...])
        x1_ref[0, rows, :] = x1
        ms = jnp.mean(x1 * x1, axis=-1, keepdims=True)
        hx = x1 * lax.rsqrt(ms + EPS) * g2_ref[...] * (1.0 + sc2_ref[0]) + sh2_ref[0]
        hx_ref[0, rows, :] = hx
        hx_hi = hx.astype(BF16)
        hx_lo = (hx - hx_hi.astype(F32)).astype(BF16)
        l2 = _dot(hx_hi, wr_ref[...])
        lg = l2[:, :LANES] + l2[:, LANES:] + _dot(hx_lo, wr_ref[:, :LANES])
        lg_ref[0, :, rows] = lg.T[:NE, :]


def _outproj_call(yf, o_f, o_b, p_main, x, w_out, gnw, gt1, g2, sh2, sc2, w_r):
    b, t, _ = x.shape
    tm = 256
    zcol = p_main.shape[2] // GW - 1
    row = lambda c: pl.BlockSpec((1, tm, c), lambda bi, i: (bi, i, 0))
    vec = pl.BlockSpec((1, 1, D), lambda bi, i: (bi, 0, 0))
    return pl.pallas_call(
        _outproj_kernel,
        grid=(b, t // tm),
        in_specs=[row(FW), row(GW), row(GW),
                  pl.BlockSpec((1, tm, GW), lambda bi, i: (bi, i, zcol)),
                  row(D),
                  pl.BlockSpec((D, D), lambda bi, i: (0, 0)),
                  pl.BlockSpec((1, HD), lambda bi, i: (0, 0)),
                  vec,
                  pl.BlockSpec((1, D), lambda bi, i: (0, 0)),
                  vec, vec,
                  pl.BlockSpec((D, 2 * LANES), lambda bi, i: (0, 0))],
        out_specs=[row(D), row(D), pl.BlockSpec((1, NE, tm), lambda bi, i: (bi, 0, i))],
        out_shape=[jax.ShapeDtypeStruct((b, t, D), F32),
                   jax.ShapeDtypeStruct((b, t, D), F32),
                   jax.ShapeDtypeStruct((b, NE, t), F32)],
        compiler_params=_cparams(("parallel", "parallel")),
        name="outproj",
    )(yf, o_f, o_b, p_main, x, w_out, gnw, gt1, g2, sh2, sc2, w_r)


def _lane_cumsum(x):
    n = x.shape[-1]
    lane = lax.broadcasted_iota(I32, x.shape, x.ndim - 1)
    sh = 1
    while sh < n:
        x = x + jnp.where(lane >= sh, pltpu.roll(x, shift=sh, axis=x.ndim - 1), 0.0)
        sh *= 2
    return x


def _topk_kernel(lg_ref, idx_ref, val_ref, tab_scr, rend_scr, *, cap, nt):
    nr = NE * nt
    lg = lg_ref[0]
    lg3 = lg.reshape(NE, nt, LANES)
    ex3 = jnp.exp(lg3 - jnp.max(lg3, axis=0, keepdims=True))
    aff = (ex3 / jnp.sum(ex3, axis=0, keepdims=True)).reshape(nr, LANES)

    rr = lax.broadcasted_iota(I32, (nr, nr), 0)
    rc = lax.broadcasted_iota(I32, (nr, nr), 1)
    ntb = int(math.log2(nt))
    same = (rr >> ntb) == (rc >> ntb)
    grp = same.astype(BF16)
    before = jnp.logical_and(same, rc < rr).astype(BF16)

    def per_expert_sum(mat, mask):
        return jnp.sum(_dot(mat, mask.astype(BF16)), axis=1, keepdims=True)

    def search(i, cur):
        cand = cur | (1 << (30 - i))
        cnt = per_expert_sum(grp, (aff >= pltpu.bitcast(cand, F32)).astype(F32))
        return jnp.where(cnt >= cap, cand, cur)

    thr = pltpu.bitcast(lax.fori_loop(0, 31, search, jnp.zeros((nr, 1), I32)), F32)
    gt = aff > thr
    eq = (aff == thr).astype(F32)

    def rank_of(m):
        return _lane_cumsum(m) + per_expert_sum(before, m)

    n_gt = per_expert_sum(grp, gt.astype(F32))
    sel = jnp.logical_or(gt, jnp.logical_and(eq > 0.0, rank_of(eq) <= cap - n_gt))
    selm = sel.astype(F32)
    rsel = jnp.where(sel, rank_of(selm), 0.0)
    rprev = per_expert_sum(before, selm)
    rend_scr[...] = jnp.concatenate([jnp.broadcast_to(rprev, (nr, LANES)),
                                     jnp.broadcast_to(rprev + jnp.sum(selm, axis=1, keepdims=True), (nr, LANES))],
                                    axis=1)
    hi = jnp.floor(rsel * (1.0 / 32.0))
    a1 = aff.astype(BF16)
    a2 = (aff - a1.astype(F32)).astype(BF16)
    a3 = (aff - a1.astype(F32) - a2.astype(F32)).astype(BF16)
    tab_scr[...] = jnp.concatenate([hi.astype(BF16), (rsel - 32.0 * hi).astype(BF16), a1, a2, a3], axis=1)
    idx_ref[0] = jnp.zeros((cap, LANES), I32)
    val_ref[0] = jnp.zeros((cap, LANES), F32)
    slot = (lax.broadcasted_iota(I32, (cap, 1), 0) + 1).astype(F32)
    lane = lax.broadcasted_iota(I32, (cap, LANES), 1)
    lanef = lane.astype(F32)
    pad = jnp.full((LANES - nt, 2 * LANES), 2.0 * cap, F32)

    def per_expert(e, carry):
        r0 = pl.multiple_of(e * nt, nt)
        rt = jnp.concatenate([rend_scr[pl.ds(r0, nt), :], pad], axis=0)
        prev_l = rt[:, :LANES].T[0:1, :]
        end_l = rt[:, LANES:].T[0:1, :]
        inrow = jnp.logical_and(prev_l < slot, slot <= end_l)
        tab = jnp.concatenate([tab_scr[pl.ds(r0, nt), :], jnp.zeros((LANES - nt, 5 * LANES), BF16)], axis=0)
        g = _dot(inrow.astype(BF16), tab)
        hit = (32.0 * g[:, :LANES] + g[:, LANES:2 * LANES]) == slot
        ag = g[:, 2 * LANES:3 * LANES] + g[:, 3 * LANES:4 * LANES] + g[:, 4 * LANES:]
        pos = (jnp.sum(jnp.where(inrow, lanef, 0.0), axis=1, keepdims=True) * float(LANES)
               + jnp.sum(jnp.where(hit, lanef, 0.0), axis=1, keepdims=True))
        val = jnp.sum(jnp.where(hit, ag, 0.0), axis=1, keepdims=True)
        idx_ref[0] = jnp.where(lane == e, pos.astype(I32), idx_ref[0])
        val_ref[0] = jnp.where(lane == e, val, val_ref[0])
        return carry

    lax.fori_loop(0, NE, per_expert, 0)


def _topk_call(lg_t, cap):
    b, _, t = lg_t.shape
    nt = t // LANES
    nr = NE * nt
    return pl.pallas_call(
        functools.partial(_topk_kernel, cap=cap, nt=nt),
        grid=(b,),
        in_specs=[pl.BlockSpec((1, nr, LANES), lambda bi: (bi, 0, 0))],
        out_specs=[pl.BlockSpec((1, cap, LANES), lambda bi: (bi, 0, 0)),
                   pl.BlockSpec((1, cap, LANES), lambda bi: (bi, 0, 0))],
        out_shape=[jax.ShapeDtypeStruct((b, cap, LANES), I32),
                   jax.ShapeDtypeStruct((b, cap, LANES), F32)],
        scratch_shapes=[pltpu.VMEM((nr, 5 * LANES), BF16), pltpu.VMEM((nr, 2 * LANES), F32)],
        compiler_params=_cparams(("parallel",)),
        name="topk",
    )(lg_t.reshape(b, nr, LANES))


MOE_TF = 256
MOE_TN = 256
MOE_MC = 512
MOE_NF = EFF // MOE_TF
MOE_NN = D // MOE_TN
MOE_GC = 352
MOE_GPAD = 32


def _moe_ffn_kernel(idx_ref, hx_hbm, val_ref, gt2_ref, wg_ref, wu_ref, wd_ref, y_ref,
                    xg_scr, hid_scr, stg_scr, vcol_scr, gsem, *, rows):
    e = pl.program_id(0)
    s = pl.program_id(1)
    cur = e % 2
    nxt = jnp.minimum(e + 1, pl.num_programs(0) - 1)

    def issue(expert, c):
        base = expert * rows + c * MOE_GC
        for r in range(MOE_GC):
            pltpu.make_async_copy(hx_hbm.at[pl.ds(idx_ref[base + r], 1), :],
                                  stg_scr.at[pl.ds(r, 1), :], gsem).start()

    def land(c, buf):
        pltpu.make_async_copy(hx_hbm.at[pl.ds(0, MOE_GC), :], stg_scr, gsem).wait()
        r0 = pl.multiple_of(c * MOE_GC, MOE_GPAD)
        xg_scr[buf, pl.ds(r0, MOE_GC), :] = stg_scr[...].astype(BF16)

    @pl.when(jnp.logical_and(e == 0, s == 0))
    def _():
        for c in range(MOE_NF):
            issue(0, c)
            land(c, 0)

    @pl.when(jnp.logical_and(s >= 1, s <= MOE_NF))
    def _():
        land(s - 1, 1 - cur)

    @pl.when(s < MOE_NF)
    def _():
        xs = xg_scr[cur, 0:rows, :]
        hid = (_silu(_dot(xs, wg_ref[0].astype(BF16))) * _dot(xs, wu_ref[0].astype(BF16))).astype(BF16)
        hid_scr[:, pl.ds(pl.multiple_of(s * MOE_TF, MOE_TF), MOE_TF)] = hid
        issue(nxt, s)

    @pl.when(s == MOE_NF)
    def _():
        vcol_scr[...] = jnp.broadcast_to(val_ref[0], (LANES, rows)).T

    @pl.when(s >= MOE_NF)
    def _():
        y = _dot(hid_scr[...], wd_ref[0].astype(BF16))
        for mc in range(rows // MOE_MC):
            sl = slice(mc * MOE_MC, (mc + 1) * MOE_MC)
            y_ref[0, sl, :] = (y[sl] * vcol_scr[sl, 0:1] * gt2_ref[mc, 0]).astype(y_ref.dtype)


def _moe_ffn_call(idx_flat, hx2, vals, gt2, w_gate, w_up, w_down, cap):
    rows = vals.shape[2]
    rows_pad = MOE_NF * MOE_GC
    assert rows // MOE_MC == gt2.shape[0] and cap == MOE_MC and rows_pad >= rows and MOE_GC % MOE_GPAD == 0
    idx_flat = jnp.pad(idx_flat, (0, rows_pad - rows))
    gt2t = gt2.reshape(gt2.shape[0], MOE_NN, 1, MOE_TN)
    ph1 = lambda s: jnp.minimum(s, MOE_NF - 1)
    ph2 = lambda s: jnp.maximum(s - MOE_NF, 0)
    grid_spec = pltpu.PrefetchScalarGridSpec(
        num_scalar_prefetch=1,
        grid=(NE, MOE_NF + MOE_NN),
        in_specs=[pl.BlockSpec(memory_space=pl.ANY),
                  pl.BlockSpec((1, 1, rows), lambda e, s, idx: (e, 0, 0)),
                  pl.BlockSpec((gt2.shape[0], 1, 1, MOE_TN), lambda e, s, idx: (0, ph2(s), 0, 0)),
                  pl.BlockSpec((1, D, MOE_TF), lambda e, s, idx: (e, 0, ph1(s))),
                  pl.BlockSpec((1, D, MOE_TF), lambda e, s, idx: (e, 0, ph1(s))),
                  pl.BlockSpec((1, EFF, MOE_TN), lambda e, s, idx: (e, 0, ph2(s)))],
        out_specs=pl.BlockSpec((1, rows, MOE_TN), lambda e, s, idx: (e, 0, ph2(s))),
        scratch_shapes=[pltpu.VMEM((2, rows_pad, D), BF16),
                        pltpu.VMEM((rows, EFF), BF16),
                        pltpu.VMEM((MOE_GC, D), F32),
                        pltpu.VMEM((rows, LANES), F32),
                        pltpu.SemaphoreType.DMA(())],
    )
    return pl.pallas_call(
        functools.partial(_moe_ffn_kernel, rows=rows),
        grid_spec=grid_spec,
        out_shape=jax.ShapeDtypeStruct((NE, rows, D), BF16),
        compiler_params=_cparams(("arbitrary", "arbitrary")),
        name="moe_ffn",
    )(idx_flat, hx2, vals, gt2t, w_gate, w_up, w_down)


MOE_SC = 256


def _moe_scatter_kernel(idx_ref, y_ref, res_hbm, out_hbm, buf_scr, rsem, wsem, *, rows):
    del res_hbm
    e = pl.program_id(0)
    c = pl.program_id(1)
    nc = pl.num_programs(1)
    slot = c % 2

    def reads(chunk, sl):
        base = e * rows + chunk * MOE_SC
        for r in range(MOE_SC):
            pltpu.make_async_copy(out_hbm.at[pl.ds(idx_ref[base + r], 1), :],
                                  buf_scr.at[sl, pl.ds(r, 1), :], rsem.at[sl]).start(priority=r % 2)

    def writes(chunk, sl):
        base = e * rows + chunk * MOE_SC
        for r in range(MOE_SC):
            pltpu.make_async_copy(buf_scr.at[sl, pl.ds(r, 1), :],
                                  out_hbm.at[pl.ds(idx_ref[base + r], 1), :], wsem.at[sl]).start(priority=r % 2)

    def wait_all(sem, sl):
        pltpu.make_async_copy(out_hbm.at[pl.ds(0, MOE_SC), :], buf_scr.at[sl], sem.at[sl]).wait()

    @pl.when(c == 0)
    def _():
        reads(c, slot)

    wait_all(rsem, slot)

    @pl.when(c + 1 < nc)
    def _():
        @pl.when(c >= 1)
        def _():
            wait_all(wsem, 1 - slot)
        reads(c + 1, 1 - slot)

    buf_scr[slot] = buf_scr[slot] + y_ref[0].astype(F32)
    writes(c, slot)

    @pl.when(c == nc - 1)
    def _():
        wait_all(wsem, 1 - slot)
        wait_all(wsem, slot)


def _moe_scatter_call(idx_flat, y, x1):
    rows = y.shape[1]
    grid_spec = pltpu.PrefetchScalarGridSpec(
        num_scalar_prefetch=1,
        grid=(NE, rows // MOE_SC),
        in_specs=[pl.BlockSpec((1, MOE_SC, D), lambda e, c, idx: (e, c, 0)),
                  pl.BlockSpec(memory_space=pl.ANY)],
        out_specs=pl.BlockSpec(memory_space=pl.ANY),
        scratch_shapes=[pltpu.VMEM((2, MOE_SC, D), F32),
                        pltpu.SemaphoreType.DMA((2,)),
                        pltpu.SemaphoreType.DMA((2,))],
    )
    return pl.pallas_call(
        functools.partial(_moe_scatter_kernel, rows=rows),
        grid_spec=grid_spec,
        out_shape=jax.ShapeDtypeStruct(x1.shape, F32),
        input_output_aliases={2: 0},
        compiler_params=_cparams(("arbitrary", "arbitrary")),
        name="moe_scatter",
    )(idx_flat, y, x1)


def _final_kernel(x_ref, g_ref, o_ref):
    x = x_ref[...]
    ms = jnp.mean(x * x, axis=-1, keepdims=True)
    o_ref[...] = x * lax.rsqrt(ms + EPS) * g_ref[...]


def _final_call(x2d, g):
    m = x2d.shape[0]
    tm = 512
    return pl.pallas_call(
        _final_kernel,
        grid=(m // tm,),
        in_specs=[pl.BlockSpec((tm, D), lambda i: (i, 0)), pl.BlockSpec((1, D), lambda i: (0, 0))],
        out_specs=pl.BlockSpec((tm, D), lambda i: (i, 0)),
        out_shape=jax.ShapeDtypeStruct(x2d.shape, F32),
        compiler_params=_cparams(("parallel",)),
        name="final_norm",
    )(x2d, g)


def kernel(x, c, ctx, c_ctx, w_mod, b_mod, norm1_g, norm2_g, w_in, conv_w, a_log, dt_bias, gdn_norm_w, w_out,
           w_router, w_gate, w_up, w_down, norm_f):
    b, n, _ = x.shape
    nctx = ctx.shape[1]
    cap = 2 * n // NE
    i = 0

    cc = jnp.concatenate([c, c_ctx[None, :], jnp.zeros((8 - b - 1, D), F32)], axis=0)
    mod = _mod_call(cc, w_mod[i], b_mod[i][None, :])
    sh1, sc1, gt1, sh2, sc2, gt2 = [mod[:b, k * D:(k + 1) * D][:, None, :] for k in range(6)]
    sh1c = jnp.broadcast_to(mod[b:b + 1, 0:D][:, None, :], (b, 1, D))
    sc1c = jnp.broadcast_to(mod[b:b + 1, D:2 * D][:, None, :], (b, 1, D))

    nmain = FW + 4 * GW
    w_main = w_in[i][:, :nmain].astype(BF16)
    w_ab = jnp.pad(w_in[i][:, nmain:], ((0, 0), (0, LANES - 2 * N_DIR * NH))).astype(BF16)
    conv_w8 = jnp.pad(conv_w[i], ((0, 8 - CONV_K), (0, 0)))
    alog_row = jnp.pad(a_log[i].reshape(1, NU), ((0, 0), (0, LANES - NU)))
    dtb_row = jnp.pad(dt_bias[i].reshape(1, NU), ((0, 0), (0, LANES - NU)))
    g1 = norm1_g[i][None, :]

    p_ctx, ab_ctx = _inproj_call(ctx, g1, sh1c, sc1c, w_main, w_ab, conv_w8, nctx, 1, 3, nctx)
    _, _, s_ctx = _gdn_call(p_ctx, 0, ab_ctx, alog_row, dtb_row, jnp.zeros((b, NU, HD, HD), F32))

    p_main, ab = _inproj_call(x, g1, sh1, sc1, w_main, w_ab, conv_w8, 1024, 0, 5, GRID_W)
    o_f, o_b, _ = _gdn_call(p_main, 1, ab, alog_row, dtb_row, s_ctx)
    yf = _fourier_call(p_main)

    w_r = jnp.pad(w_router[i], ((0, 0), (0, LANES - NE)))
    w_r_hi = w_r.astype(BF16)
    w_r = jnp.concatenate([w_r_hi, (w_r - w_r_hi.astype(F32)).astype(BF16)], axis=1)
    x1, hx2, lg_t = _outproj_call(yf, o_f, o_b, p_main, x, w_out[i].astype(BF16), gdn_norm_w[i][None, :], gt1,
                                  norm2_g[i][None, :], sh2, sc2, w_r)

    idx_c, val_c = _topk_call(lg_t, cap)
    idx = jnp.transpose(idx_c[:, :, :NE], (2, 0, 1))
    val = jnp.transpose(val_c[:, :, :NE], (2, 0, 1))
    idx_flat = (idx + (jnp.arange(b, dtype=I32) * n)[None, :, None]).reshape(NE * b * cap)
    vals = val.reshape(NE, 1, b * cap)

    y = _moe_ffn_call(idx_flat, hx2.reshape(b * n, D), vals, gt2, w_gate[i], w_up[i], w_down[i], cap)
    out = _moe_scatter_call(idx_flat, y, x1.reshape(b * n, D))
    return _final_call(out, norm_f[None, :]).reshape(b, n, D)
```

```python
import functools
import math

import numpy as np
import jax
import jax.numpy as jnp
from jax import lax
from jax.experimental import pallas as pl
from jax.experimental.pallas import tpu as pltpu

F32 = jnp.float32
BF16 = jnp.bfloat16
I32 = jnp.int32

D = 2048
SEQ_N = 4096
GRID_W = 64
FW = 1024
FG = 8
FGD = 128
GW = 1024
NH = 8
HD = 128
CONV_K = 5
N_DIR = 2
NE = 16
EFF = 1536
EPS = 1e-6
GC = 128
NU = N_DIR * NH
LANES = 128
VMEM_LIMIT = 56 * 1024 * 1024


def _sigmoid(x):
    return 1.0 / (1.0 + jnp.exp(-x))


def _silu(x):
    return x * _sigmoid(x)


def _softplus(x):
    return jnp.maximum(x, 0.0) + jnp.log(1.0 + jnp.exp(-jnp.abs(x)))


def _dot(a, b):
    return jnp.dot(a, b, preferred_element_type=F32)


def _dot_nt(a, b):
    return lax.dot_general(a, b, (((1,), (1,)), ((), ())), preferred_element_type=F32)


def _bmm(a, b):
    return jnp.einsum('uij,ujk->uik', a.astype(BF16), b.astype(BF16), preferred_element_type=F32)


def _bmm_nt(a, b):
    return jnp.einsum('uik,ujk->uij', a, b, preferred_element_type=F32)


def _cparams(sem, vmem=VMEM_LIMIT):
    return pltpu.CompilerParams(dimension_semantics=sem, vmem_limit_bytes=vmem)


def _mod_kernel(c_ref, w_ref, b_ref, o_ref):
    s = _silu(c_ref[...]).astype(BF16)
    o_ref[...] = _dot(s, w_ref[...].astype(BF16)) + b_ref[...]


def _mod_call(cc, w_mod, b_mod):
    tn = 1024
    n = w_mod.shape[1]
    return pl.pallas_call(
        _mod_kernel,
        grid=(n // tn,),
        in_specs=[pl.BlockSpec((8, D), lambda j: (0, 0)),
                  pl.BlockSpec((D, tn), lambda j: (0, j)),
                  pl.BlockSpec((1, tn), lambda j: (0, j))],
        out_specs=pl.BlockSpec((8, tn), lambda j: (0, j)),
        out_shape=jax.ShapeDtypeStruct((8, n), F32),
        compiler_params=_cparams(("parallel",)),
        name="mod",
    )(cc, w_mod, b_mod)


INPROJ_SUB = 256


def _inproj_kernel(x_ref, g_ref, sh_ref, sc_ref, w_ref, wab_ref, cw_ref, o_ref, ab_ref, h_scr, *, col0, row_len):
    c = pl.program_id(2) + col0

    @pl.when(pl.program_id(2) == 0)
    def _():
        x = x_ref[0]
        ms = jnp.mean(x * x, axis=-1, keepdims=True)
        y = x * lax.rsqrt(ms + EPS) * g_ref[...]
        h = (y * (1.0 + sc_ref[0]) + sh_ref[0]).astype(BF16)
        h_scr[...] = h
        ab_ref[0] = _dot(h, wab_ref[...])

    is_qkv = jnp.logical_and(c >= 1, c <= 3)

    @pl.when(jnp.logical_not(is_qkv))
    def _():
        o_ref[0] = _dot(h_scr[...], w_ref[...]).astype(o_ref.dtype)

    @pl.when(is_qkv)
    def _():
        half = CONV_K // 2
        qscale = jnp.where(c == 1, HD ** -0.5, 1.0).astype(F32)

        def shifted(x, off):
            zero = jnp.zeros((abs(off), HD), F32)
            return (jnp.concatenate([x[off:], zero], axis=0) if off > 0
                    else jnp.concatenate([zero, x[:off]], axis=0))

        for rb in range(x_ref.shape[1] // INPROJ_SUB):
            p = _dot(h_scr[rb * INPROJ_SUB:(rb + 1) * INPROJ_SUB, :], w_ref[...])
            for g in range(INPROJ_SUB // row_len):
                rows = slice(g * row_len, (g + 1) * row_len)
                for h in range(NH):
                    cols = slice(h * HD, (h + 1) * HD)
                    x = p[rows, cols]
                    acc = x * cw_ref[half:half + 1, cols]
                    for tap in range(CONV_K):
                        off = tap - half
                        if off != 0:
                            acc = acc + shifted(x, off) * cw_ref[tap:tap + 1, cols]
                    y = _silu(acc)
                    ss = jnp.sum(y * y, axis=-1, keepdims=True)
                    nrm = jnp.where(c < 3, lax.rsqrt(ss + EPS) * qscale, 1.0)
                    o_ref[0, rb * INPROJ_SUB + g * row_len:rb * INPROJ_SUB + (g + 1) * row_len, cols] = (
                        y * nrm).astype(o_ref.dtype)


def _inproj_call(x, g, sh, sc, w_main, w_ab, conv_w8, tm, col0, ncol, row_len):
    b, t, _ = x.shape
    tn = 1024
    assert tm % INPROJ_SUB == 0 and INPROJ_SUB % row_len == 0
    return pl.pallas_call(
        functools.partial(_inproj_kernel, col0=col0, row_len=row_len),
        grid=(b, t // tm, ncol),
        in_specs=[pl.BlockSpec((1, tm, D), lambda bi, i, j: (bi, i, 0)),
                  pl.BlockSpec((1, D), lambda bi, i, j: (0, 0)),
                  pl.BlockSpec((1, 1, D), lambda bi, i, j: (bi, 0, 0)),
                  pl.BlockSpec((1, 1, D), lambda bi, i, j: (bi, 0, 0)),
                  pl.BlockSpec((D, tn), lambda bi, i, j: (0, j + col0)),
                  pl.BlockSpec((D, LANES), lambda bi, i, j: (0, 0)),
                  pl.BlockSpec((8, GW), lambda bi, i, j: (0, jnp.clip(j + col0 - 1, 0, 2)))],
        out_specs=[pl.BlockSpec((1, tm, tn), lambda bi, i, j: (bi, i, j)),
                   pl.BlockSpec((1, tm, LANES), lambda bi, i, j: (bi, i, 0))],
        out_shape=[jax.ShapeDtypeStruct((b, t, ncol * tn), BF16),
                   jax.ShapeDtypeStruct((b, t, LANES), F32)],
        scratch_shapes=[pltpu.VMEM((tm, D), BF16)],
        compiler_params=_cparams(("parallel", "parallel", "arbitrary")),
        name="inproj",
    )(x, g, sh, sc, w_main, w_ab, conv_w8)


def _gdn_kernel(qf_ref, kf_ref, vf_ref, abf_ref, qb_ref, kb_ref, vb_ref, abb_ref, alog_ref, dtb_ref, s0_ref,
                of_ref, ob_ref, sout_ref, s_scr):
    n = pl.program_id(1)

    @pl.when(n == 0)
    def _():
        s_scr[...] = s0_ref[0]

    row = lax.broadcasted_iota(I32, (GC, GC), 0)
    col = lax.broadcasted_iota(I32, (GC, GC), 1)
    eye = (row == col).astype(F32)
    m_parts, rhs_parts, qd_parts, kdt_parts, qk_parts, gt_parts = [], [], [], [], [], []
    for d in range(N_DIR):
        q_ref, k_ref, v_ref, ab_ref = ((qf_ref, kf_ref, vf_ref, abf_ref) if d == 0
                                       else (qb_ref, kb_ref, vb_ref, abb_ref))
        incl = (col <= row) if d == 0 else (col >= row)
        strict = (col < row) if d == 0 else (col > row)
        ab = ab_ref[0]
        g_all = -jnp.exp(alog_ref[...]) * _softplus(ab + dtb_ref[...])
        beta_all = _sigmoid(ab)
        gc_all = jnp.dot(incl.astype(F32), g_all, preferred_element_type=F32,
                         precision=lax.Precision.HIGHEST)
        gc_t = gc_all.T
        last = GC - 1 if d == 0 else 0
        c0 = d * NH
        cb = N_DIR * NH + d * NH
        heads = range(NH)
        gcol = jnp.stack([gc_all[:, c0 + h:c0 + h + 1] for h in heads])
        grow = jnp.stack([gc_t[c0 + h:c0 + h + 1, :] for h in heads])
        glast = jnp.stack([gc_all[last:last + 1, c0 + h:c0 + h + 1] for h in heads])
        beta = jnp.stack([beta_all[:, cb + h:cb + h + 1] for h in heads])
        q = jnp.stack([q_ref[0, :, h * HD:(h + 1) * HD] for h in heads]).astype(F32)
        k = jnp.stack([k_ref[0, :, h * HD:(h + 1) * HD] for h in heads])
        kf = k.astype(F32)
        v = jnp.stack([v_ref[0, :, h * HD:(h + 1) * HD] for h in heads]).astype(F32)
        decay = jnp.where(incl[None], jnp.exp(jnp.where(incl[None], gcol - grow, 0.0)), 0.0)
        egc = jnp.exp(gcol)
        kbeta = kf * beta
        a2 = _bmm_nt(jnp.concatenate([kbeta, q], axis=1).astype(BF16), k)
        m_parts.append(jnp.where(strict[None], a2[:, :GC] * decay, 0.0))
        rhs_parts.append(jnp.concatenate([v * beta, kbeta * egc], axis=2).astype(BF16))
        qd_parts.append((q * egc).astype(BF16))
        kdt_parts.append(jnp.swapaxes(kf * jnp.exp(glast - gcol), 1, 2).astype(BF16))
        qk_parts.append((a2[:, GC:] * decay).astype(BF16))
        gt_parts.append(jnp.exp(glast))

    cat = lambda parts: jnp.concatenate(parts, axis=0)
    m, rhs, qd, kdt, qk, gtot = (cat(t) for t in (m_parts, rhs_parts, qd_parts, kdt_parts, qk_parts, gt_parts))
    blk = lambda s: ((row >> int(math.log2(s))) == (col >> int(math.log2(s))))[None]
    mb = jnp.where(blk(8), m, 0.0)
    p = eye[None] - mb
    m2 = _bmm(mb, mb)
    p = p + _bmm(p, m2)
    p = p + _bmm(p, _bmm(m2, m2))
    s = 8
    while s < GC:
        cpart = jnp.where(jnp.logical_and(blk(2 * s), jnp.logical_not(blk(s))), m, 0.0)
        p = p - _bmm(p, _bmm(cpart, p))
        s *= 2
    uw = _bmm(p, rhs)

    st = s_scr[...]
    r = _bmm(jnp.concatenate([uw[:, :, HD:].astype(BF16), qd], axis=1), st)
    vb = (uw[:, :, :HD] - r[:, :GC]).astype(BF16)
    o = r[:, GC:] + _bmm(qk, vb)
    s_scr[...] = st * gtot + _bmm(kdt, vb)
    for un in range(NU):
        h = un % NH
        if un < NH:
            of_ref[0, :, h * HD:(h + 1) * HD] = o[un].astype(of_ref.dtype)
        else:
            ob_ref[0, :, h * HD:(h + 1) * HD] = o[un].astype(ob_ref.dtype)

    @pl.when(n == pl.num_programs(1) - 1)
    def _():
        sout_ref[0] = s_scr[...]


def _gdn_call(qkv, col0, ab, alog_row, dtb_row, s0):
    b, t, _ = qkv.shape
    nc = t // GC
    blk = lambda c, rev: pl.BlockSpec((1, GC, GW), (lambda bi, n: (bi, nc - 1 - n, c + col0)) if rev
                                      else (lambda bi, n: (bi, n, c + col0)))
    abblk = lambda rev: pl.BlockSpec((1, GC, LANES), (lambda bi, n: (bi, nc - 1 - n, 0)) if rev
                                     else (lambda bi, n: (bi, n, 0)))
    sblk = pl.BlockSpec((1, NU, HD, HD), lambda bi, n: (bi, 0, 0, 0))
    return pl.pallas_call(
        _gdn_kernel,
        grid=(b, nc),
        in_specs=[blk(0, False), blk(1, False), blk(2, False), abblk(False),
                  blk(0, True), blk(1, True), blk(2, True), abblk(True),
                  pl.BlockSpec((1, LANES), lambda bi, n: (0, 0)),
                  pl.BlockSpec((1, LANES), lambda bi, n: (0, 0)),
                  sblk],
        out_specs=[pl.BlockSpec((1, GC, GW), lambda bi, n: (bi, n, 0)),
                   pl.BlockSpec((1, GC, GW), lambda bi, n: (bi, nc - 1 - n, 0)),
                   sblk],
        out_shape=[jax.ShapeDtypeStruct((b, t, GW), BF16),
                   jax.ShapeDtypeStruct((b, t, GW), BF16),
                   jax.ShapeDtypeStruct((b, NU, HD, HD), F32)],
        scratch_shapes=[pltpu.VMEM((NU, HD, HD), F32)],
        compiler_params=_cparams(("parallel", "arbitrary")),
        name="gdn",
    )(qkv, qkv, qkv, ab, qkv, qkv, qkv, ab, alog_row, dtb_row, s0)


def _dft_tables():
    r = GRID_W
    a = 2.0 * np.pi * np.outer(np.arange(r), np.arange(r)) / r
    c64, s64 = np.cos(a), np.sin(a)
    ac = 2.0 * np.pi * np.outer(np.arange(FGD), np.arange(FGD)) / FGD
    cc, sc = np.cos(ac), np.sin(ac)
    at = (2.0 * np.pi * np.outer(np.arange(r), np.arange(r)) / (r * r)).reshape(r * r, 1)
    chan = np.concatenate([cc, -sc], axis=1)
    row_cs = np.concatenate([c64, s64], axis=0)
    col_re = np.concatenate([c64, s64], axis=1)
    as_bf16 = lambda t: jnp.asarray(t, F32).astype(BF16)
    return (as_bf16(chan), as_bf16(row_cs), as_bf16(col_re),
            jnp.asarray(np.cos(at), F32), jnp.asarray(np.sin(at), F32))


def _fourier_kernel(x_ref, chan_ref, rowcs_ref, colre_ref, twc_ref, tws_ref, o_ref, u_scr, v_scr, y_scr):
    r = GRID_W

    def swap(t):
        return jnp.swapaxes(t.reshape(r, r, FGD), 0, 1).reshape(r * r, FGD)

    u = _dot(x_ref[0], chan_ref[...])
    u_scr[0] = swap(u[:, :FGD])
    u_scr[1] = swap(u[:, FGD:])

    def over_n2(n1, carry):
        rows = pl.ds(pl.multiple_of(n1 * r, r), r)
        x = jnp.concatenate([u_scr[0, rows, :], u_scr[1, rows, :]], axis=1).astype(BF16)
        cs = _dot(rowcs_ref[...], x)
        vr = cs[:r, :FGD] + cs[r:, FGD:]
        vi = cs[:r, FGD:] - cs[r:, :FGD]
        tc = twc_ref[rows, :]
        ts = tws_ref[rows, :]
        v_scr[0, rows, :] = vr * tc + vi * ts
        v_scr[1, rows, :] = vi * tc - vr * ts
        return carry

    lax.fori_loop(0, r, over_n2, 0, unroll=16)
    u_scr[0] = swap(v_scr[0])
    u_scr[1] = swap(v_scr[1])

    def over_n1(k2, carry):
        rows = pl.ds(pl.multiple_of(k2 * r, r), r)
        st = jnp.concatenate([u_scr[0, rows, :], u_scr[1, rows, :]], axis=0).astype(BF16)
        y_scr[rows, :] = _dot(colre_ref[...], st) * (1.0 / math.sqrt(SEQ_N * FGD))
        return carry

    lax.fori_loop(0, r, over_n1, 0, unroll=16)
    o_ref[0] = swap(y_scr[...]).astype(o_ref.dtype)


def _fourier_call(p_main):
    b, n, _ = p_main.shape
    r = GRID_W
    chan, rowcs, colre, twc, tws = _dft_tables()
    twc = jnp.broadcast_to(twc, (n, LANES))
    tws = jnp.broadcast_to(tws, (n, LANES))
    const = lambda shape: pl.BlockSpec(shape, lambda bi, g: (0, 0))
    return pl.pallas_call(
        _fourier_kernel,
        grid=(b, FG),
        in_specs=[pl.BlockSpec((1, n, FGD), lambda bi, g: (bi, 0, g)),
                  const((FGD, 2 * FGD)), const((2 * r, r)), const((r, 2 * r)),
                  const((n, LANES)), const((n, LANES))],
        out_specs=pl.BlockSpec((1, n, FGD), lambda bi, g: (bi, 0, g)),
        out_shape=jax.ShapeDtypeStruct((b, n, FW), BF16),
        scratch_shapes=[pltpu.VMEM((2, n, FGD), F32), pltpu.VMEM((2, n, FGD), F32), pltpu.VMEM((n, FGD), F32)],
        compiler_params=_cparams(("parallel", "parallel")),
        name="fourier",
    )(p_main, chan, rowcs, colre, twc, tws)


def _outproj_kernel(yf_ref, of_ref, ob_ref, z_ref, x_ref, wout_ref, gnw_ref, gt1_ref, g2_ref, sh2_ref, sc2_ref,
                    wr_ref, x1_ref, hx_ref, lg_ref):
    sub = 128
    for rb in range(x_ref.shape[1] // sub):
        rows = slice(rb * sub, (rb + 1) * sub)
        parts = [yf_ref[0, rows, :]]
        for h in range(NH):
            cols = slice(h * HD, (h + 1) * HD)
            oh = of_ref[0, rows, cols].astype(F32) + ob_ref[0, rows, cols].astype(F32)
            ms = jnp.mean(oh * oh, axis=-1, keepdims=True)
            on = oh * lax.rsqrt(ms + EPS) * gnw_ref[...]
            parts.append((on * _silu(z_ref[0, rows, cols].astype(F32))).astype(BF16))
        mix = jnp.concatenate(parts, axis=1)
        x1 = x_ref[0, rows, :] + gt1_ref[0] * _dot(mix, wout_ref[...])
        x1_ref[0, rows, :] = x1
        ms = jnp.mean(x1 * x1, axis=-1, keepdims=True)
        hx = x1 * lax.rsqrt(ms + EPS) * g2_ref[...] * (1.0 + sc2_ref[0]) + sh2_ref[0]
        hx_ref[0, rows, :] = hx
        hx_hi = hx.astype(BF16)
        hx_lo = (hx - hx_hi.astype(F32)).astype(BF16)
        l2 = _dot(hx_hi, wr_ref[...])
        lg = l2[:, :LANES] + l2[:, LANES:] + _dot(hx_lo, wr_ref[:, :LANES])
        lg_ref[0, :, rows] = lg.T[:NE, :]


def _outproj_call(yf, o_f, o_b, p_main, x, w_out, gnw, gt1, g2, sh2, sc2, w_r):
    b, t, _ = x.shape
    tm = 256
    zcol = p_main.shape[2] // GW - 1
    row = lambda c: pl.BlockSpec((1, tm, c), lambda bi, i: (bi, i, 0))
    vec = pl.BlockSpec((1, 1, D), lambda bi, i: (bi, 0, 0))
    return pl.pallas_call(
        _outproj_kernel,
        grid=(b, t // tm),
        in_specs=[row(FW), row(GW), row(GW),
                  pl.BlockSpec((1, tm, GW), lambda bi, i: (bi, i, zcol)),
                  row(D),
                  pl.BlockSpec((D, D), lambda bi, i: (0, 0)),
                  pl.BlockSpec((1, HD), lambda bi, i: (0, 0)),
                  vec,
                  pl.BlockSpec((1, D), lambda bi, i: (0, 0)),
                  vec, vec,
                  pl.BlockSpec((D, 2 * LANES), lambda bi, i: (0, 0))],
        out_specs=[row(D), row(D), pl.BlockSpec((1, NE, tm), lambda bi, i: (bi, 0, i))],
        out_shape=[jax.ShapeDtypeStruct((b, t, D), F32),
                   jax.ShapeDtypeStruct((b, t, D), F32),
                   jax.ShapeDtypeStruct((b, NE, t), F32)],
        compiler_params=_cparams(("parallel", "parallel")),
        name="outproj",
    )(yf, o_f, o_b, p_main, x, w_out, gnw, gt1, g2, sh2, sc2, w_r)


def _lane_cumsum(x):
    n = x.shape[-1]
    lane = lax.broadcasted_iota(I32, x.shape, x.ndim - 1)
    sh = 1
    while sh < n:
        x = x + jnp.where(lane >= sh, pltpu.roll(x, shift=sh, axis=x.ndim - 1), 0.0)
        sh *= 2
    return x


def _topk_kernel(lg_ref, idx_ref, val_ref, tab_scr, rend_scr, *, cap, nt):
    nr = NE * nt
    lg = lg_ref[0]
    lg3 = lg.reshape(NE, nt, LANES)
    ex3 = jnp.exp(lg3 - jnp.max(lg3, axis=0, keepdims=True))
    aff = (ex3 / jnp.sum(ex3, axis=0, keepdims=True)).reshape(nr, LANES)

    rr = lax.broadcasted_iota(I32, (nr, nr), 0)
    rc = lax.broadcasted_iota(I32, (nr, nr), 1)
    ntb = int(math.log2(nt))
    same = (rr >> ntb) == (rc >> ntb)
    grp = same.astype(BF16)
    before = jnp.logical_and(same, rc < rr).astype(BF16)

    def per_expert_sum(mat, mask):
        return jnp.sum(_dot(mat, mask.astype(BF16)), axis=1, keepdims=True)

    def search(i, cur):
        cand = cur | (1 << (30 - i))
        cnt = per_expert_sum(grp, (aff >= pltpu.bitcast(cand, F32)).astype(F32))
        return jnp.where(cnt >= cap, cand, cur)

    thr = pltpu.bitcast(lax.fori_loop(0, 31, search, jnp.zeros((nr, 1), I32)), F32)
    gt = aff > thr
    eq = (aff == thr).astype(F32)

    def rank_of(m):
        return _lane_cumsum(m) + per_expert_sum(before, m)

    n_gt = per_expert_sum(grp, gt.astype(F32))
    sel = jnp.logical_or(gt, jnp.logical_and(eq > 0.0, rank_of(eq) <= cap - n_gt))
    selm = sel.astype(F32)
    rsel = jnp.where(sel, rank_of(selm), 0.0)
    rprev = per_expert_sum(before, selm)
    rend_scr[...] = jnp.concatenate([jnp.broadcast_to(rprev, (nr, LANES)),
                                     jnp.broadcast_to(rprev + jnp.sum(selm, axis=1, keepdims=True), (nr, LANES))],
                                    axis=1)
    hi = jnp.floor(rsel * (1.0 / 32.0))
    a1 = aff.astype(BF16)
    a2 = (aff - a1.astype(F32)).astype(BF16)
    a3 = (aff - a1.astype(F32) - a2.astype(F32)).astype(BF16)
    tab_scr[...] = jnp.concatenate([hi.astype(BF16), (rsel - 32.0 * hi).astype(BF16), a1, a2, a3], axis=1)
    idx_ref[0] = jnp.zeros((cap, LANES), I32)
    val_ref[0] = jnp.zeros((cap, LANES), F32)
    slot = (lax.broadcasted_iota(I32, (cap, 1), 0) + 1).astype(F32)
    lane = lax.broadcasted_iota(I32, (cap, LANES), 1)
    lanef = lane.astype(F32)
    pad = jnp.full((LANES - nt, 2 * LANES), 2.0 * cap, F32)

    def per_expert(e, carry):
        r0 = pl.multiple_of(e * nt, nt)
        rt = jnp.concatenate([rend_scr[pl.ds(r0, nt), :], pad], axis=0)
        prev_l = rt[:, :LANES].T[0:1, :]
        end_l = rt[:, LANES:].T[0:1, :]
        inrow = jnp.logical_and(prev_l < slot, slot <= end_l)
        tab = jnp.concatenate([tab_scr[pl.ds(r0, nt), :], jnp.zeros((LANES - nt, 5 * LANES), BF16)], axis=0)
        g = _dot(inrow.astype(BF16), tab)
        hit = (32.0 * g[:, :LANES] + g[:, LANES:2 * LANES]) == slot
        ag = g[:, 2 * LANES:3 * LANES] + g[:, 3 * LANES:4 * LANES] + g[:, 4 * LANES:]
        pos = (jnp.sum(jnp.where(inrow, lanef, 0.0), axis=1, keepdims=True) * float(LANES)
               + jnp.sum(jnp.where(hit, lanef, 0.0), axis=1, keepdims=True))
        val = jnp.sum(jnp.where(hit, ag, 0.0), axis=1, keepdims=True)
        idx_ref[0] = jnp.where(lane == e, pos.astype(I32), idx_ref[0])
        val_ref[0] = jnp.where(lane == e, val, val_ref[0])
        return carry

    lax.fori_loop(0, NE, per_expert, 0)


def _topk_call(lg_t, cap):
    b, _, t = lg_t.shape
    nt = t // LANES
    nr = NE * nt
    return pl.pallas_call(
        functools.partial(_topk_kernel, cap=cap, nt=nt),
        grid=(b,),
        in_specs=[pl.BlockSpec((1, nr, LANES), lambda bi: (bi, 0, 0))],
        out_specs=[pl.BlockSpec((1, cap, LANES), lambda bi: (bi, 0, 0)),
                   pl.BlockSpec((1, cap, LANES), lambda bi: (bi, 0, 0))],
        out_shape=[jax.ShapeDtypeStruct((b, cap, LANES), I32),
                   jax.ShapeDtypeStruct((b, cap, LANES), F32)],
        scratch_shapes=[pltpu.VMEM((nr, 5 * LANES), BF16), pltpu.VMEM((nr, 2 * LANES), F32)],
        compiler_params=_cparams(("parallel",)),
        name="topk",
    )(lg_t.reshape(b, nr, LANES))


MOE_TF = 256
MOE_TN = 256
MOE_MC = 512
MOE_NF = EFF // MOE_TF
MOE_NN = D // MOE_TN
MOE_GC = 352
MOE_GPAD = 32


def _moe_ffn_kernel(idx_ref, hx_hbm, val_ref, gt2_ref, wg_ref, wu_ref, wd_ref, y_ref,
                    xg_scr, hid_scr, stg_scr, vcol_scr, gsem, *, rows):
    e = pl.program_id(0)
    s = pl.program_id(1)
    cur = e % 2
    nxt = jnp.minimum(e + 1, pl.num_programs(0) - 1)

    def issue(expert, c):
        base = expert * rows + c * MOE_GC
        for r in range(MOE_GC):
            pltpu.make_async_copy(hx_hbm.at[pl.ds(idx_ref[base + r], 1), :],
                                  stg_scr.at[pl.ds(r, 1), :], gsem).start(priority=r % 2)

    def land(c, buf):
        pltpu.make_async_copy(hx_hbm.at[pl.ds(0, MOE_GC), :], stg_scr, gsem).wait()
        r0 = pl.multiple_of(c * MOE_GC, MOE_GPAD)
        xg_scr[buf, pl.ds(r0, MOE_GC), :] = stg_scr[...].astype(BF16)

    @pl.when(jnp.logical_and(e == 0, s == 0))
    def _():
        for c in range(MOE_NF):
            issue(0, c)
            land(c, 0)

    @pl.when(jnp.logical_and(s >= 1, s <= MOE_NF))
    def _():
        land(s - 1, 1 - cur)

    @pl.when(s < MOE_NF)
    def _():
        xs = xg_scr[cur, 0:rows, :]
        hid = (_silu(_dot(xs, wg_ref[0].astype(BF16))) * _dot(xs, wu_ref[0].astype(BF16))).astype(BF16)
        hid_scr[:, pl.ds(pl.multiple_of(s * MOE_TF, MOE_TF), MOE_TF)] = hid
        issue(nxt, s)

    @pl.when(s == MOE_NF)
    def _():
        vcol_scr[...] = jnp.broadcast_to(val_ref[0], (LANES, rows)).T

    @pl.when(s >= MOE_NF)
    def _():
        y = _dot(hid_scr[...], wd_ref[0].astype(BF16))
        for mc in range(rows // MOE_MC):
            sl = slice(mc * MOE_MC, (mc + 1) * MOE_MC)
            y_ref[0, sl, :] = (y[sl] * vcol_scr[sl, 0:1] * gt2_ref[mc, 0]).astype(y_ref.dtype)


def _moe_ffn_call(idx_flat, hx2, vals, gt2, w_gate, w_up, w_down, cap):
    rows = vals.shape[2]
    rows_pad = MOE_NF * MOE_GC
    assert rows // MOE_MC == gt2.shape[0] and cap == MOE_MC and rows_pad >= rows and MOE_GC % MOE_GPAD == 0
    idx_flat = jnp.pad(idx_flat, (0, rows_pad - rows))
    gt2t = gt2.reshape(gt2.shape[0], MOE_NN, 1, MOE_TN)
    ph1 = lambda s: jnp.minimum(s, MOE_NF - 1)
    ph2 = lambda s: jnp.maximum(s - MOE_NF, 0)
    grid_spec = pltpu.PrefetchScalarGridSpec(
        num_scalar_prefetch=1,
        grid=(NE, MOE_NF + MOE_NN),
        in_specs=[pl.BlockSpec(memory_space=pl.ANY),
                  pl.BlockSpec((1, 1, rows), lambda e, s, idx: (e, 0, 0)),
                  pl.BlockSpec((gt2.shape[0], 1, 1, MOE_TN), lambda e, s, idx: (0, ph2(s), 0, 0)),
                  pl.BlockSpec((1, D, MOE_TF), lambda e, s, idx: (e, 0, ph1(s))),
                  pl.BlockSpec((1, D, MOE_TF), lambda e, s, idx: (e, 0, ph1(s))),
                  pl.BlockSpec((1, EFF, MOE_TN), lambda e, s, idx: (e, 0, ph2(s)))],
        out_specs=pl.BlockSpec((1, rows, MOE_TN), lambda e, s, idx: (e, 0, ph2(s))),
        scratch_shapes=[pltpu.VMEM((2, rows_pad, D), BF16),
                        pltpu.VMEM((rows, EFF), BF16),
                        pltpu.VMEM((MOE_GC, D), F32),
                        pltpu.VMEM((rows, LANES), F32),
                        pltpu.SemaphoreType.DMA(())],
    )
    return pl.pallas_call(
        functools.partial(_moe_ffn_kernel, rows=rows),
        grid_spec=grid_spec,
        out_shape=jax.ShapeDtypeStruct((NE, rows, D), BF16),
        compiler_params=_cparams(("arbitrary", "arbitrary")),
        name="moe_ffn",
    )(idx_flat, hx2, vals, gt2t, w_gate, w_up, w_down)


MOE_SC = 256
MOE_SS = 4


def _moe_scatter_kernel(idx_ref, y_ref, res_hbm, out_hbm, buf_scr, rsem, wsem, *, rows):
    del res_hbm
    e = pl.program_id(0)
    st = pl.program_id(1)
    nst = pl.num_programs(1)
    c0 = st * MOE_SS

    def reads(chunk, k):
        base = e * rows + chunk * MOE_SC
        for r in range(MOE_SC):
            pltpu.make_async_copy(out_hbm.at[pl.ds(idx_ref[base + r], 1), :],
                                  buf_scr.at[k, pl.ds(r, 1), :], rsem.at[k]).start(priority=r % 2)

    def writes(chunk, k):
        base = e * rows + chunk * MOE_SC
        for r in range(MOE_SC):
            pltpu.make_async_copy(buf_scr.at[k, pl.ds(r, 1), :],
                                  out_hbm.at[pl.ds(idx_ref[base + r], 1), :], wsem.at[k]).start(priority=r % 2)

    def wait_all(sem, k):
        pltpu.make_async_copy(out_hbm.at[pl.ds(0, MOE_SC), :], buf_scr.at[k], sem.at[k]).wait()

    @pl.when(st == 0)
    def _():
        reads(c0, 0)
        reads(c0 + 1, 1)

    for k in range(MOE_SS):
        wait_all(rsem, k)
        buf_scr[k] = buf_scr[k] + y_ref[0, k * MOE_SC:(k + 1) * MOE_SC, :].astype(F32)
        writes(c0 + k, k)
        if k + 2 < MOE_SS:
            @pl.when(st > 0)
            def _():
                wait_all(wsem, k + 2)
            reads(c0 + k + 2, k + 2)
        else:
            @pl.when(st + 1 < nst)
            def _():
                wait_all(wsem, k + 2 - MOE_SS)
                reads(c0 + k + 2, k + 2 - MOE_SS)

    @pl.when(st == nst - 1)
    def _():
        for k in range(MOE_SS):
            wait_all(wsem, k)


def _moe_scatter_call(idx_flat, y, x1):
    rows = y.shape[1]
    grid_spec = pltpu.PrefetchScalarGridSpec(
        num_scalar_prefetch=1,
        grid=(NE, rows // (MOE_SC * MOE_SS)),
        in_specs=[pl.BlockSpec((1, MOE_SC * MOE_SS, D), lambda e, st, idx: (e, st, 0)),
                  pl.BlockSpec(memory_space=pl.ANY)],
        out_specs=pl.BlockSpec(memory_space=pl.ANY),
        scratch_shapes=[pltpu.VMEM((MOE_SS, MOE_SC, D), F32),
                        pltpu.SemaphoreType.DMA((MOE_SS,)),
                        pltpu.SemaphoreType.DMA((MOE_SS,))],
    )
    return pl.pallas_call(
        functools.partial(_moe_scatter_kernel, rows=rows),
        grid_spec=grid_spec,
        out_shape=jax.ShapeDtypeStruct(x1.shape, F32),
        input_output_aliases={2: 0},
        compiler_params=_cparams(("arbitrary", "arbitrary")),
        name="moe_scatter",
    )(idx_flat, y, x1)


def _final_kernel(x_ref, g_ref, o_ref):
    x = x_ref[...]
    ms = jnp.mean(x * x, axis=-1, keepdims=True)
    o_ref[...] = x * lax.rsqrt(ms + EPS) * g_ref[...]


def _final_call(x2d, g):
    m = x2d.shape[0]
    tm = 512
    return pl.pallas_call(
        _final_kernel,
        grid=(m // tm,),
        in_specs=[pl.BlockSpec((tm, D), lambda i: (i, 0)), pl.BlockSpec((1, D), lambda i: (0, 0))],
        out_specs=pl.BlockSpec((tm, D), lambda i: (i, 0)),
        out_shape=jax.ShapeDtypeStruct(x2d.shape, F32),
        compiler_params=_cparams(("parallel",)),
        name="final_norm",
    )(x2d, g)


def kernel(x, c, ctx, c_ctx, w_mod, b_mod, norm1_g, norm2_g, w_in, conv_w, a_log, dt_bias, gdn_norm_w, w_out,
           w_router, w_gate, w_up, w_down, norm_f):
    b, n, _ = x.shape
    nctx = ctx.shape[1]
    cap = 2 * n // NE
    i = 0

    cc = jnp.concatenate([c, c_ctx[None, :], jnp.zeros((8 - b - 1, D), F32)], axis=0)
    mod = _mod_call(cc, w_mod[i], b_mod[i][None, :])
    sh1, sc1, gt1, sh2, sc2, gt2 = [mod[:b, k * D:(k + 1) * D][:, None, :] for k in range(6)]
    sh1c = jnp.broadcast_to(mod[b:b + 1, 0:D][:, None, :], (b, 1, D))
    sc1c = jnp.broadcast_to(mod[b:b + 1, D:2 * D][:, None, :], (b, 1, D))

    nmain = FW + 4 * GW
    w_main = w_in[i][:, :nmain].astype(BF16)
    w_ab = jnp.pad(w_in[i][:, nmain:], ((0, 0), (0, LANES - 2 * N_DIR * NH))).astype(BF16)
    conv_w8 = jnp.pad(conv_w[i], ((0, 8 - CONV_K), (0, 0)))
    alog_row = jnp.pad(a_log[i].reshape(1, NU), ((0, 0), (0, LANES - NU)))
    dtb_row = jnp.pad(dt_bias[i].reshape(1, NU), ((0, 0), (0, LANES - NU)))
    g1 = norm1_g[i][None, :]

    p_ctx, ab_ctx = _inproj_call(ctx, g1, sh1c, sc1c, w_main, w_ab, conv_w8, nctx, 1, 3, nctx)
    _, _, s_ctx = _gdn_call(p_ctx, 0, ab_ctx, alog_row, dtb_row, jnp.zeros((b, NU, HD, HD), F32))

    p_main, ab = _inproj_call(x, g1, sh1, sc1, w_main, w_ab, conv_w8, 1024, 0, 5, GRID_W)
    o_f, o_b, _ = _gdn_call(p_main, 1, ab, alog_row, dtb_row, s_ctx)
    yf = _fourier_call(p_main)

    w_r = jnp.pad(w_router[i], ((0, 0), (0, LANES - NE)))
    w_r_hi = w_r.astype(BF16)
    w_r = jnp.concatenate([w_r_hi, (w_r - w_r_hi.astype(F32)).astype(BF16)], axis=1)
    x1, hx2, lg_t = _outproj_call(yf, o_f, o_b, p_main, x, w_out[i].astype(BF16), gdn_norm_w[i][None, :], gt1,
                                  norm2_g[i][None, :], sh2, sc2, w_r)

    idx_c, val_c = _topk_call(lg_t, cap)
    idx = jnp.transpose(idx_c[:, :, :NE], (2, 0, 1))
    val = jnp.transpose(val_c[:, :, :NE], (2, 0, 1))
    idx_flat = (idx + (jnp.arange(b, dtype=I32) * n)[None, :, None]).reshape(NE * b * cap)
    vals = val.reshape(NE, 1, b * cap)

    y = _moe_ffn_call(idx_flat, hx2.reshape(b * n, D), vals, gt2, w_gate[i], w_up[i], w_down[i], cap)
    out = _moe_scatter_call(idx_flat, y, x1.reshape(b * n, D))
    return _final_call(out, norm_f[None, :]).reshape(b, n, D)
```

```python
import functools
import math

import numpy as np
import jax
import jax.numpy as jnp
from jax import lax
from jax.experimental import pallas as pl
from jax.experimental.pallas import tpu as pltpu

F32 = jnp.float32
BF16 = jnp.bfloat16
I32 = jnp.int32

D = 2048
SEQ_N = 4096
GRID_W = 64
FW = 1024
FG = 8
FGD = 128
GW = 1024
NH = 8
HD = 128
CONV_K = 5
N_DIR = 2
NE = 16
EFF = 1536
EPS = 1e-6
GC = 128
NU = N_DIR * NH
LANES = 128
VMEM_LIMIT = 56 * 1024 * 1024


def _sigmoid(x):
    return 1.0 / (1.0 + jnp.exp(-x))


def _silu(x):
    return x * _sigmoid(x)


def _softplus(x):
    return jnp.maximum(x, 0.0) + jnp.log(1.0 + jnp.exp(-jnp.abs(x)))


def _dot(a, b):
    return jnp.dot(a, b, preferred_element_type=F32)


def _dot_nt(a, b):
    return lax.dot_general(a, b, (((1,), (1,)), ((), ())), preferred_element_type=F32)


def _bmm(a, b):
    return jnp.einsum('uij,ujk->uik', a.astype(BF16), b.astype(BF16), preferred_element_type=F32)


def _bmm_nt(a, b):
    return jnp.einsum('uik,ujk->uij', a, b, preferred_element_type=F32)


def _cparams(sem, vmem=VMEM_LIMIT):
    return pltpu.CompilerParams(dimension_semantics=sem, vmem_limit_bytes=vmem)


def _mod_kernel(c_ref, w_ref, b_ref, o_ref):
    s = _silu(c_ref[...]).astype(BF16)
    o_ref[...] = _dot(s, w_ref[...].astype(BF16)) + b_ref[...]


def _mod_call(cc, w_mod, b_mod):
    tn = 1024
    n = w_mod.shape[1]
    return pl.pallas_call(
        _mod_kernel,
        grid=(n // tn,),
        in_specs=[pl.BlockSpec((8, D), lambda j: (0, 0)),
                  pl.BlockSpec((D, tn), lambda j: (0, j)),
                  pl.BlockSpec((1, tn), lambda j: (0, j))],
        out_specs=pl.BlockSpec((8, tn), lambda j: (0, j)),
        out_shape=jax.ShapeDtypeStruct((8, n), F32),
        compiler_params=_cparams(("parallel",)),
        name="mod",
    )(cc, w_mod, b_mod)


def _wprep_kernel(w_ref, wm_ref, wab_ref):
    nmain = wm_ref.shape[1]
    ngate = w_ref.shape[2] - nmain
    wm_ref[...] = w_ref[0, :, :nmain].astype(BF16)
    wab_ref[...] = jnp.concatenate([w_ref[0, :, nmain:], jnp.zeros((w_ref.shape[1], LANES - ngate), F32)],
                                   axis=1).astype(BF16)


def _wprep_call(w_in, layer):
    _, d, c = w_in.shape
    nmain = FW + 4 * GW
    tr = 256
    return pl.pallas_call(
        _wprep_kernel,
        grid=(d // tr,),
        in_specs=[pl.BlockSpec((1, tr, c), lambda r: (layer, r, 0))],
        out_specs=[pl.BlockSpec((tr, nmain), lambda r: (r, 0)), pl.BlockSpec((tr, LANES), lambda r: (r, 0))],
        out_shape=[jax.ShapeDtypeStruct((d, nmain), BF16), jax.ShapeDtypeStruct((d, LANES), BF16)],
        compiler_params=_cparams(("parallel",)),
        name="wprep",
    )(w_in)


INPROJ_SUB = 256


def _inproj_kernel(x_ref, g_ref, sh_ref, sc_ref, w_ref, wab_ref, cw_ref, o_ref, ab_ref, h_scr, *, col0, row_len):
    c = pl.program_id(2) + col0

    @pl.when(pl.program_id(2) == 0)
    def _():
        x = x_ref[0]
        ms = jnp.mean(x * x, axis=-1, keepdims=True)
        y = x * lax.rsqrt(ms + EPS) * g_ref[...]
        h = (y * (1.0 + sc_ref[0]) + sh_ref[0]).astype(BF16)
        h_scr[...] = h
        ab_ref[0] = _dot(h, wab_ref[...])

    is_qkv = jnp.logical_and(c >= 1, c <= 3)

    @pl.when(jnp.logical_not(is_qkv))
    def _():
        o_ref[0] = _dot(h_scr[...], w_ref[...]).astype(o_ref.dtype)

    @pl.when(is_qkv)
    def _():
        half = CONV_K // 2
        qscale = jnp.where(c == 1, HD ** -0.5, 1.0).astype(F32)

        def shifted(x, off):
            zero = jnp.zeros((abs(off), HD), F32)
            return (jnp.concatenate([x[off:], zero], axis=0) if off > 0
                    else jnp.concatenate([zero, x[:off]], axis=0))

        for rb in range(x_ref.shape[1] // INPROJ_SUB):
            p = _dot(h_scr[rb * INPROJ_SUB:(rb + 1) * INPROJ_SUB, :], w_ref[...])
            for g in range(INPROJ_SUB // row_len):
                rows = slice(g * row_len, (g + 1) * row_len)
                for h in range(NH):
                    cols = slice(h * HD, (h + 1) * HD)
                    x = p[rows, cols]
                    acc = x * cw_ref[half:half + 1, cols]
                    for tap in range(CONV_K):
                        off = tap - half
                        if off != 0:
                            acc = acc + shifted(x, off) * cw_ref[tap:tap + 1, cols]
                    y = _silu(acc)
                    ss = jnp.sum(y * y, axis=-1, keepdims=True)
                    nrm = jnp.where(c < 3, lax.rsqrt(ss + EPS) * qscale, 1.0)
                    o_ref[0, rb * INPROJ_SUB + g * row_len:rb * INPROJ_SUB + (g + 1) * row_len, cols] = (
                        y * nrm).astype(o_ref.dtype)


def _inproj_call(x, g, sh, sc, w_main, w_ab, conv_w8, tm, col0, ncol, row_len):
    b, t, _ = x.shape
    tn = 1024
    assert tm % INPROJ_SUB == 0 and INPROJ_SUB % row_len == 0
    return pl.pallas_call(
        functools.partial(_inproj_kernel, col0=col0, row_len=row_len),
        grid=(b, t // tm, ncol),
        in_specs=[pl.BlockSpec((1, tm, D), lambda bi, i, j: (bi, i, 0)),
                  pl.BlockSpec((1, D), lambda bi, i, j: (0, 0)),
                  pl.BlockSpec((1, 1, D), lambda bi, i, j: (bi, 0, 0)),
                  pl.BlockSpec((1, 1, D), lambda bi, i, j: (bi, 0, 0)),
                  pl.BlockSpec((D, tn), lambda bi, i, j: (0, j + col0)),
                  pl.BlockSpec((D, LANES), lambda bi, i, j: (0, 0)),
                  pl.BlockSpec((8, GW), lambda bi, i, j: (0, jnp.clip(j + col0 - 1, 0, 2)))],
        out_specs=[pl.BlockSpec((1, tm, tn), lambda bi, i, j: (bi, i, j)),
                   pl.BlockSpec((1, tm, LANES), lambda bi, i, j: (bi, i, 0))],
        out_shape=[jax.ShapeDtypeStruct((b, t, ncol * tn), BF16),
                   jax.ShapeDtypeStruct((b, t, LANES), F32)],
        scratch_shapes=[pltpu.VMEM((tm, D), BF16)],
        compiler_params=_cparams(("parallel", "parallel", "arbitrary")),
        name="inproj",
    )(x, g, sh, sc, w_main, w_ab, conv_w8)


def _gdn_kernel(qf_ref, kf_ref, vf_ref, abf_ref, qb_ref, kb_ref, vb_ref, abb_ref, alog_ref, dtb_ref, s0_ref,
                of_ref, ob_ref, sout_ref, s_scr):
    n = pl.program_id(1)

    @pl.when(n == 0)
    def _():
        s_scr[...] = s0_ref[0]

    row = lax.broadcasted_iota(I32, (GC, GC), 0)
    col = lax.broadcasted_iota(I32, (GC, GC), 1)
    eye = (row == col).astype(F32)
    m_parts, rhs_parts, qd_parts, kdt_parts, qk_parts, gt_parts = [], [], [], [], [], []
    for d in range(N_DIR):
        q_ref, k_ref, v_ref, ab_ref = ((qf_ref, kf_ref, vf_ref, abf_ref) if d == 0
                                       else (qb_ref, kb_ref, vb_ref, abb_ref))
        incl = (col <= row) if d == 0 else (col >= row)
        strict = (col < row) if d == 0 else (col > row)
        ab = ab_ref[0]
        g_all = -jnp.exp(alog_ref[...]) * _softplus(ab + dtb_ref[...])
        beta_all = _sigmoid(ab)
        gc_all = jnp.dot(incl.astype(F32), g_all, preferred_element_type=F32,
                         precision=lax.Precision.HIGHEST)
        gc_t = gc_all.T
        last = GC - 1 if d == 0 else 0
        c0 = d * NH
        cb = N_DIR * NH + d * NH
        heads = range(NH)
        gcol = jnp.stack([gc_all[:, c0 + h:c0 + h + 1] for h in heads])
        grow = jnp.stack([gc_t[c0 + h:c0 + h + 1, :] for h in heads])
        glast = jnp.stack([gc_all[last:last + 1, c0 + h:c0 + h + 1] for h in heads])
        beta = jnp.stack([beta_all[:, cb + h:cb + h + 1] for h in heads])
        q = jnp.stack([q_ref[0, :, h * HD:(h + 1) * HD] for h in heads]).astype(F32)
        k = jnp.stack([k_ref[0, :, h * HD:(h + 1) * HD] for h in heads])
        kf = k.astype(F32)
        v = jnp.stack([v_ref[0, :, h * HD:(h + 1) * HD] for h in heads]).astype(F32)
        decay = jnp.where(incl[None], jnp.exp(jnp.where(incl[None], gcol - grow, 0.0)), 0.0)
        egc = jnp.exp(gcol)
        kbeta = kf * beta
        a2 = _bmm_nt(jnp.concatenate([kbeta, q], axis=1).astype(BF16), k)
        m_parts.append(jnp.where(strict[None], a2[:, :GC] * decay, 0.0))
        rhs_parts.append(jnp.concatenate([v * beta, kbeta * egc], axis=2).astype(BF16))
        qd_parts.append((q * egc).astype(BF16))
        kdt_parts.append(jnp.swapaxes(kf * jnp.exp(glast - gcol), 1, 2).astype(BF16))
        qk_parts.append((a2[:, GC:] * decay).astype(BF16))
        gt_parts.append(jnp.exp(glast))

    cat = lambda parts: jnp.concatenate(parts, axis=0)
    m, rhs, qd, kdt, qk, gtot = (cat(t) for t in (m_parts, rhs_parts, qd_parts, kdt_parts, qk_parts, gt_parts))
    blk = lambda s: ((row >> int(math.log2(s))) == (col >> int(math.log2(s))))[None]
    mb = jnp.where(blk(8), m, 0.0)
    p = eye[None] - mb
    m2 = _bmm(mb, mb)
    p = p + _bmm(p, m2)
    p = p + _bmm(p, _bmm(m2, m2))
    s = 8
    while s < GC:
        cpart = jnp.where(jnp.logical_and(blk(2 * s), jnp.logical_not(blk(s))), m, 0.0)
        p = p - _bmm(p, _bmm(cpart, p))
        s *= 2
    uw = _bmm(p, rhs)

    st = s_scr[...]
    r = _bmm(jnp.concatenate([uw[:, :, HD:].astype(BF16), qd], axis=1), st)
    vb = (uw[:, :, :HD] - r[:, :GC]).astype(BF16)
    o = r[:, GC:] + _bmm(qk, vb)
    s_scr[...] = st * gtot + _bmm(kdt, vb)
    for un in range(NU):
        h = un % NH
        if un < NH:
            of_ref[0, :, h * HD:(h + 1) * HD] = o[un].astype(of_ref.dtype)
        else:
            ob_ref[0, :, h * HD:(h + 1) * HD] = o[un].astype(ob_ref.dtype)

    @pl.when(n == pl.num_programs(1) - 1)
    def _():
        sout_ref[0] = s_scr[...]


def _gdn_call(qkv, col0, ab, alog_row, dtb_row, s0):
    b, t, _ = qkv.shape
    nc = t // GC
    blk = lambda c, rev: pl.BlockSpec((1, GC, GW), (lambda bi, n: (bi, nc - 1 - n, c + col0)) if rev
                                      else (lambda bi, n: (bi, n, c + col0)))
    abblk = lambda rev: pl.BlockSpec((1, GC, LANES), (lambda bi, n: (bi, nc - 1 - n, 0)) if rev
                                     else (lambda bi, n: (bi, n, 0)))
    sblk = pl.BlockSpec((1, NU, HD, HD), lambda bi, n: (bi, 0, 0, 0))
    return pl.pallas_call(
        _gdn_kernel,
        grid=(b, nc),
        in_specs=[blk(0, False), blk(1, False), blk(2, False), abblk(False),
                  blk(0, True), blk(1, True), blk(2, True), abblk(True),
                  pl.BlockSpec((1, LANES), lambda bi, n: (0, 0)),
                  pl.BlockSpec((1, LANES), lambda bi, n: (0, 0)),
                  sblk],
        out_specs=[pl.BlockSpec((1, GC, GW), lambda bi, n: (bi, n, 0)),
                   pl.BlockSpec((1, GC, GW), lambda bi, n: (bi, nc - 1 - n, 0)),
                   sblk],
        out_shape=[jax.ShapeDtypeStruct((b, t, GW), BF16),
                   jax.ShapeDtypeStruct((b, t, GW), BF16),
                   jax.ShapeDtypeStruct((b, NU, HD, HD), F32)],
        scratch_shapes=[pltpu.VMEM((NU, HD, HD), F32)],
        compiler_params=_cparams(("parallel", "arbitrary")),
        name="gdn",
    )(qkv, qkv, qkv, ab, qkv, qkv, qkv, ab, alog_row, dtb_row, s0)


def _dft_tables():
    r = GRID_W
    a = 2.0 * np.pi * np.outer(np.arange(r), np.arange(r)) / r
    c64, s64 = np.cos(a), np.sin(a)
    ac = 2.0 * np.pi * np.outer(np.arange(FGD), np.arange(FGD)) / FGD
    cc, sc = np.cos(ac), np.sin(ac)
    at = (2.0 * np.pi * np.outer(np.arange(r), np.arange(r)) / (r * r)).reshape(r * r, 1)
    chan = np.concatenate([cc, -sc], axis=1)
    row_cs = np.concatenate([c64, s64], axis=0)
    col_re = np.concatenate([c64, s64], axis=1)
    as_bf16 = lambda t: jnp.asarray(t, F32).astype(BF16)
    return (as_bf16(chan), as_bf16(row_cs), as_bf16(col_re),
            jnp.asarray(np.cos(at), F32), jnp.asarray(np.sin(at), F32))


def _fourier_kernel(x_ref, chan_ref, rowcs_ref, colre_ref, twc_ref, tws_ref, o_ref, u_scr, v_scr, y_scr):
    r = GRID_W

    def swap(t):
        return jnp.swapaxes(t.reshape(r, r, FGD), 0, 1).reshape(r * r, FGD)

    u = _dot(x_ref[0], chan_ref[...])
    u_scr[0] = swap(u[:, :FGD])
    u_scr[1] = swap(u[:, FGD:])

    def over_n2(n1, carry):
        rows = pl.ds(pl.multiple_of(n1 * r, r), r)
        x = jnp.concatenate([u_scr[0, rows, :], u_scr[1, rows, :]], axis=1).astype(BF16)
        cs = _dot(rowcs_ref[...], x)
        vr = cs[:r, :FGD] + cs[r:, FGD:]
        vi = cs[:r, FGD:] - cs[r:, :FGD]
        tc = twc_ref[rows, :]
        ts = tws_ref[rows, :]
        v_scr[0, rows, :] = vr * tc + vi * ts
        v_scr[1, rows, :] = vi * tc - vr * ts
        return carry

    lax.fori_loop(0, r, over_n2, 0, unroll=16)
    u_scr[0] = swap(v_scr[0])
    u_scr[1] = swap(v_scr[1])

    def over_n1(k2, carry):
        rows = pl.ds(pl.multiple_of(k2 * r, r), r)
        st = jnp.concatenate([u_scr[0, rows, :], u_scr[1, rows, :]], axis=0).astype(BF16)
        y_scr[rows, :] = _dot(colre_ref[...], st) * (1.0 / math.sqrt(SEQ_N * FGD))
        return carry

    lax.fori_loop(0, r, over_n1, 0, unroll=16)
    o_ref[0] = swap(y_scr[...]).astype(o_ref.dtype)


def _fourier_call(p_main):
    b, n, _ = p_main.shape
    r = GRID_W
    chan, rowcs, colre, twc, tws = _dft_tables()
    twc = jnp.broadcast_to(twc, (n, LANES))
    tws = jnp.broadcast_to(tws, (n, LANES))
    const = lambda shape: pl.BlockSpec(shape, lambda bi, g: (0, 0))
    return pl.pallas_call(
        _fourier_kernel,
        grid=(b, FG),
        in_specs=[pl.BlockSpec((1, n, FGD), lambda bi, g: (bi, 0, g)),
                  const((FGD, 2 * FGD)), const((2 * r, r)), const((r, 2 * r)),
                  const((n, LANES)), const((n, LANES))],
        out_specs=pl.BlockSpec((1, n, FGD), lambda bi, g: (bi, 0, g)),
        out_shape=jax.ShapeDtypeStruct((b, n, FW), BF16),
        scratch_shapes=[pltpu.VMEM((2, n, FGD), F32), pltpu.VMEM((2, n, FGD), F32), pltpu.VMEM((n, FGD), F32)],
        compiler_params=_cparams(("parallel", "parallel")),
        name="fourier",
    )(p_main, chan, rowcs, colre, twc, tws)


def _outproj_kernel(yf_ref, of_ref, ob_ref, z_ref, x_ref, wout_ref, gnw_ref, gt1_ref, g2_ref, sh2_ref, sc2_ref,
                    wr_ref, x1_ref, hx_ref, lg_ref):
    sub = 128
    for rb in range(x_ref.shape[1] // sub):
        rows = slice(rb * sub, (rb + 1) * sub)
        parts = [yf_ref[0, rows, :]]
        for h in range(NH):
            cols = slice(h * HD, (h + 1) * HD)
            oh = of_ref[0, rows, cols].astype(F32) + ob_ref[0, rows, cols].astype(F32)
            ms = jnp.mean(oh * oh, axis=-1, keepdims=True)
            on = oh * lax.rsqrt(ms + EPS) * gnw_ref[...]
            parts.append((on * _silu(z_ref[0, rows, cols].astype(F32))).astype(BF16))
        mix = jnp.concatenate(parts, axis=1)
        x1 = x_ref[0, rows, :] + gt1_ref[0] * _dot(mix, wout_ref[...])
        x1_ref[0, rows, :] = x1
        ms = jnp.mean(x1 * x1, axis=-1, keepdims=True)
        hx = x1 * lax.rsqrt(ms + EPS) * g2_ref[...] * (1.0 + sc2_ref[0]) + sh2_ref[0]
        hx_ref[0, rows, :] = hx
        hx_hi = hx.astype(BF16)
        hx_lo = (hx - hx_hi.astype(F32)).astype(BF16)
        l2 = _dot(hx_hi, wr_ref[...])
        lg = l2[:, :LANES] + l2[:, LANES:] + _dot(hx_lo, wr_ref[:, :LANES])
        lg_ref[0, :, rows] = lg.T[:NE, :]


def _outproj_call(yf, o_f, o_b, p_main, x, w_out, gnw, gt1, g2, sh2, sc2, w_r):
    b, t, _ = x.shape
    tm = 256
    zcol = p_main.shape[2] // GW - 1
    row = lambda c: pl.BlockSpec((1, tm, c), lambda bi, i: (bi, i, 0))
    vec = pl.BlockSpec((1, 1, D), lambda bi, i: (bi, 0, 0))
    return pl.pallas_call(
        _outproj_kernel,
        grid=(b, t // tm),
        in_specs=[row(FW), row(GW), row(GW),
                  pl.BlockSpec((1, tm, GW), lambda bi, i: (bi, i, zcol)),
                  row(D),
                  pl.BlockSpec((D, D), lambda bi, i: (0, 0)),
                  pl.BlockSpec((1, HD), lambda bi, i: (0, 0)),
                  vec,
                  pl.BlockSpec((1, D), lambda bi, i: (0, 0)),
                  vec, vec,
                  pl.BlockSpec((D, 2 * LANES), lambda bi, i: (0, 0))],
        out_specs=[row(D), row(D), pl.BlockSpec((1, NE, tm), lambda bi, i: (bi, 0, i))],
        out_shape=[jax.ShapeDtypeStruct((b, t, D), F32),
                   jax.ShapeDtypeStruct((b, t, D), F32),
                   jax.ShapeDtypeStruct((b, NE, t), F32)],
        compiler_params=_cparams(("parallel", "parallel")),
        name="outproj",
    )(yf, o_f, o_b, p_main, x, w_out, gnw, gt1, g2, sh2, sc2, w_r)


def _lane_cumsum(x):
    n = x.shape[-1]
    lane = lax.broadcasted_iota(I32, x.shape, x.ndim - 1)
    sh = 1
    while sh < n:
        x = x + jnp.where(lane >= sh, pltpu.roll(x, shift=sh, axis=x.ndim - 1), 0.0)
        sh *= 2
    return x


def _topk_kernel(lg_ref, idx_ref, val_ref, tab_scr, rend_scr, *, cap, nt):
    nr = NE * nt
    lg = lg_ref[0]
    lg3 = lg.reshape(NE, nt, LANES)
    ex3 = jnp.exp(lg3 - jnp.max(lg3, axis=0, keepdims=True))
    aff = (ex3 / jnp.sum(ex3, axis=0, keepdims=True)).reshape(nr, LANES)

    rr = lax.broadcasted_iota(I32, (nr, nr), 0)
    rc = lax.broadcasted_iota(I32, (nr, nr), 1)
    ntb = int(math.log2(nt))
    same = (rr >> ntb) == (rc >> ntb)
    grp = same.astype(BF16)
    before = jnp.logical_and(same, rc < rr).astype(BF16)

    def per_expert_sum(mat, mask):
        return jnp.sum(_dot(mat, mask.astype(BF16)), axis=1, keepdims=True)

    def search(i, cur):
        cand = cur | (1 << (30 - i))
        cnt = per_expert_sum(grp, (aff >= pltpu.bitcast(cand, F32)).astype(F32))
        return jnp.where(cnt >= cap, cand, cur)

    thr = pltpu.bitcast(lax.fori_loop(0, 31, search, jnp.zeros((nr, 1), I32)), F32)
    gt = aff > thr
    eq = (aff == thr).astype(F32)

    def rank_of(m):
        return _lane_cumsum(m) + per_expert_sum(before, m)

    n_gt = per_expert_sum(grp, gt.astype(F32))
    sel = jnp.logical_or(gt, jnp.logical_and(eq > 0.0, rank_of(eq) <= cap - n_gt))
    selm = sel.astype(F32)
    rsel = jnp.where(sel, rank_of(selm), 0.0)
    rprev = per_expert_sum(before, selm)
    rend_scr[...] = jnp.concatenate([jnp.broadcast_to(rprev, (nr, LANES)),
                                     jnp.broadcast_to(rprev + jnp.sum(selm, axis=1, keepdims=True), (nr, LANES))],
                                    axis=1)
    hi = jnp.floor(rsel * (1.0 / 32.0))
    a1 = aff.astype(BF16)
    a2 = (aff - a1.astype(F32)).astype(BF16)
    a3 = (aff - a1.astype(F32) - a2.astype(F32)).astype(BF16)
    tab_scr[...] = jnp.concatenate([hi.astype(BF16), (rsel - 32.0 * hi).astype(BF16), a1, a2, a3], axis=1)
    idx_ref[0] = jnp.zeros((cap, LANES), I32)
    val_ref[0] = jnp.zeros((cap, LANES), F32)
    slot = (lax.broadcasted_iota(I32, (cap, 1), 0) + 1).astype(F32)
    lane = lax.broadcasted_iota(I32, (cap, LANES), 1)
    lanef = lane.astype(F32)
    pad = jnp.full((LANES - nt, 2 * LANES), 2.0 * cap, F32)

    def per_expert(e, carry):
        r0 = pl.multiple_of(e * nt, nt)
        rt = jnp.concatenate([rend_scr[pl.ds(r0, nt), :], pad], axis=0)
        prev_l = rt[:, :LANES].T[0:1, :]
        end_l = rt[:, LANES:].T[0:1, :]
        inrow = jnp.logical_and(prev_l < slot, slot <= end_l)
        tab = jnp.concatenate([tab_scr[pl.ds(r0, nt), :], jnp.zeros((LANES - nt, 5 * LANES), BF16)], axis=0)
        g = _dot(inrow.astype(BF16), tab)
        hit = (32.0 * g[:, :LANES] + g[:, LANES:2 * LANES]) == slot
        ag = g[:, 2 * LANES:3 * LANES] + g[:, 3 * LANES:4 * LANES] + g[:, 4 * LANES:]
        pos = (jnp.sum(jnp.where(inrow, lanef, 0.0), axis=1, keepdims=True) * float(LANES)
               + jnp.sum(jnp.where(hit, lanef, 0.0), axis=1, keepdims=True))
        val = jnp.sum(jnp.where(hit, ag, 0.0), axis=1, keepdims=True)
        idx_ref[0] = jnp.where(lane == e, pos.astype(I32), idx_ref[0])
        val_ref[0] = jnp.where(lane == e, val, val_ref[0])
        return carry

    lax.fori_loop(0, NE, per_expert, 0)


def _topk_call(lg_t, cap):
    b, _, t = lg_t.shape
    nt = t // LANES
    nr = NE * nt
    return pl.pallas_call(
        functools.partial(_topk_kernel, cap=cap, nt=nt),
        grid=(b,),
        in_specs=[pl.BlockSpec((1, nr, LANES), lambda bi: (bi, 0, 0))],
        out_specs=[pl.BlockSpec((1, cap, LANES), lambda bi: (bi, 0, 0)),
                   pl.BlockSpec((1, cap, LANES), lambda bi: (bi, 0, 0))],
        out_shape=[jax.ShapeDtypeStruct((b, cap, LANES), I32),
                   jax.ShapeDtypeStruct((b, cap, LANES), F32)],
        scratch_shapes=[pltpu.VMEM((nr, 5 * LANES), BF16), pltpu.VMEM((nr, 2 * LANES), F32)],
        compiler_params=_cparams(("parallel",)),
        name="topk",
    )(lg_t.reshape(b, nr, LANES))


MOE_TF = 256
MOE_TN = 256
MOE_MC = 512
MOE_NF = EFF // MOE_TF
MOE_NN = D // MOE_TN
MOE_GC = 352
MOE_GPAD = 32


def _moe_ffn_kernel(idx_ref, hx_hbm, val_ref, gt2_ref, wg_ref, wu_ref, wd_ref, y_ref,
                    xg_scr, hid_scr, stg_scr, vcol_scr, gsem, *, rows):
    e = pl.program_id(0)
    s = pl.program_id(1)
    cur = e % 2
    nxt = jnp.minimum(e + 1, pl.num_programs(0) - 1)

    def issue(expert, c):
        base = expert * rows + c * MOE_GC
        for r in range(MOE_GC):
            pltpu.make_async_copy(hx_hbm.at[pl.ds(idx_ref[base + r], 1), :],
                                  stg_scr.at[pl.ds(r, 1), :], gsem).start(priority=r % 2)

    def land(c, buf):
        pltpu.make_async_copy(hx_hbm.at[pl.ds(0, MOE_GC), :], stg_scr, gsem).wait()
        r0 = pl.multiple_of(c * MOE_GC, MOE_GPAD)
        xg_scr[buf, pl.ds(r0, MOE_GC), :] = stg_scr[...].astype(BF16)

    @pl.when(jnp.logical_and(e == 0, s == 0))
    def _():
        for c in range(MOE_NF):
            issue(0, c)
            land(c, 0)

    @pl.when(jnp.logical_and(s >= 1, s <= MOE_NF))
    def _():
        land(s - 1, 1 - cur)

    @pl.when(s < MOE_NF)
    def _():
        xs = xg_scr[cur, 0:rows, :]
        hid = (_silu(_dot(xs, wg_ref[0].astype(BF16))) * _dot(xs, wu_ref[0].astype(BF16))).astype(BF16)
        hid_scr[:, pl.ds(pl.multiple_of(s * MOE_TF, MOE_TF), MOE_TF)] = hid
        issue(nxt, s)

    @pl.when(s == MOE_NF)
    def _():
        vcol_scr[...] = jnp.broadcast_to(val_ref[0], (LANES, rows)).T

    @pl.when(s >= MOE_NF)
    def _():
        y = _dot(hid_scr[...], wd_ref[0].astype(BF16))
        for mc in range(rows // MOE_MC):
            sl = slice(mc * MOE_MC, (mc + 1) * MOE_MC)
            y_ref[0, sl, :] = (y[sl] * vcol_scr[sl, 0:1] * gt2_ref[mc, 0]).astype(y_ref.dtype)


def _moe_ffn_call(idx_flat, hx2, vals, gt2, w_gate, w_up, w_down, cap):
    rows = vals.shape[2]
    rows_pad = MOE_NF * MOE_GC
    assert rows // MOE_MC == gt2.shape[0] and cap == MOE_MC and rows_pad >= rows and MOE_GC % MOE_GPAD == 0
    idx_flat = jnp.pad(idx_flat, (0, rows_pad - rows))
    gt2t = gt2.reshape(gt2.shape[0], MOE_NN, 1, MOE_TN)
    ph1 = lambda s: jnp.minimum(s, MOE_NF - 1)
    ph2 = lambda s: jnp.maximum(s - MOE_NF, 0)
    grid_spec = pltpu.PrefetchScalarGridSpec(
        num_scalar_prefetch=1,
        grid=(NE, MOE_NF + MOE_NN),
        in_specs=[pl.BlockSpec(memory_space=pl.ANY),
                  pl.BlockSpec((1, 1, rows), lambda e, s, idx: (e, 0, 0)),
                  pl.BlockSpec((gt2.shape[0], 1, 1, MOE_TN), lambda e, s, idx: (0, ph2(s), 0, 0)),
                  pl.BlockSpec((1, D, MOE_TF), lambda e, s, idx: (e, 0, ph1(s))),
                  pl.BlockSpec((1, D, MOE_TF), lambda e, s, idx: (e, 0, ph1(s))),
                  pl.BlockSpec((1, EFF, MOE_TN), lambda e, s, idx: (e, 0, ph2(s)))],
        out_specs=pl.BlockSpec((1, rows, MOE_TN), lambda e, s, idx: (e, 0, ph2(s))),
        scratch_shapes=[pltpu.VMEM((2, rows_pad, D), BF16),
                        pltpu.VMEM((rows, EFF), BF16),
                        pltpu.VMEM((MOE_GC, D), F32),
                        pltpu.VMEM((rows, LANES), F32),
                        pltpu.SemaphoreType.DMA(())],
    )
    return pl.pallas_call(
        functools.partial(_moe_ffn_kernel, rows=rows),
        grid_spec=grid_spec,
        out_shape=jax.ShapeDtypeStruct((NE, rows, D), BF16),
        compiler_params=_cparams(("arbitrary", "arbitrary")),
        name="moe_ffn",
    )(idx_flat, hx2, vals, gt2t, w_gate, w_up, w_down)


MOE_SC = 256
MOE_SS = 4


def _moe_scatter_kernel(idx_ref, y_ref, res_hbm, out_hbm, buf_scr, rsem, wsem, *, rows):
    del res_hbm
    e = pl.program_id(0)
    st = pl.program_id(1)
    nst = pl.num_programs(1)
    c0 = st * MOE_SS

    def reads(chunk, k):
        base = e * rows + chunk * MOE_SC
        for r in range(MOE_SC):
            pltpu.make_async_copy(out_hbm.at[pl.ds(idx_ref[base + r], 1), :],
                                  buf_scr.at[k, pl.ds(r, 1), :], rsem.at[k]).start(priority=r % 2)

    def writes(chunk, k):
        base = e * rows + chunk * MOE_SC
        for r in range(MOE_SC):
            pltpu.make_async_copy(buf_scr.at[k, pl.ds(r, 1), :],
                                  out_hbm.at[pl.ds(idx_ref[base + r], 1), :], wsem.at[k]).start(priority=r % 2)

    def wait_all(sem, k):
        pltpu.make_async_copy(out_hbm.at[pl.ds(0, MOE_SC), :], buf_scr.at[k], sem.at[k]).wait()

    @pl.when(st == 0)
    def _():
        reads(c0, 0)
        reads(c0 + 1, 1)

    for k in range(MOE_SS):
        wait_all(rsem, k)
        buf_scr[k] = buf_scr[k] + y_ref[0, k * MOE_SC:(k + 1) * MOE_SC, :].astype(F32)
        writes(c0 + k, k)
        if k + 2 < MOE_SS:
            @pl.when(st > 0)
            def _():
                wait_all(wsem, k + 2)
            reads(c0 + k + 2, k + 2)
        else:
            @pl.when(st + 1 < nst)
            def _():
                wait_all(wsem, k + 2 - MOE_SS)
                reads(c0 + k + 2, k + 2 - MOE_SS)

    @pl.when(st == nst - 1)
    def _():
        for k in range(MOE_SS):
            wait_all(wsem, k)


def _moe_scatter_call(idx_flat, y, x1):
    rows = y.shape[1]
    grid_spec = pltpu.PrefetchScalarGridSpec(
        num_scalar_prefetch=1,
        grid=(NE, rows // (MOE_SC * MOE_SS)),
        in_specs=[pl.BlockSpec((1, MOE_SC * MOE_SS, D), lambda e, st, idx: (e, st, 0)),
                  pl.BlockSpec(memory_space=pl.ANY)],
        out_specs=pl.BlockSpec(memory_space=pl.ANY),
        scratch_shapes=[pltpu.VMEM((MOE_SS, MOE_SC, D), F32),
                        pltpu.SemaphoreType.DMA((MOE_SS,)),
                        pltpu.SemaphoreType.DMA((MOE_SS,))],
    )
    return pl.pallas_call(
        functools.partial(_moe_scatter_kernel, rows=rows),
        grid_spec=grid_spec,
        out_shape=jax.ShapeDtypeStruct(x1.shape, F32),
        input_output_aliases={2: 0},
        compiler_params=_cparams(("arbitrary", "arbitrary")),
        name="moe_scatter",
    )(idx_flat, y, x1)


def _final_kernel(x_ref, g_ref, o_ref):
    x = x_ref[...]
    ms = jnp.mean(x * x, axis=-1, keepdims=True)
    o_ref[...] = x * lax.rsqrt(ms + EPS) * g_ref[...]


def _final_call(x2d, g):
    m = x2d.shape[0]
    tm = 512
    return pl.pallas_call(
        _final_kernel,
        grid=(m // tm,),
        in_specs=[pl.BlockSpec((tm, D), lambda i: (i, 0)), pl.BlockSpec((1, D), lambda i: (0, 0))],
        out_specs=pl.BlockSpec((tm, D), lambda i: (i, 0)),
        out_shape=jax.ShapeDtypeStruct(x2d.shape, F32),
        compiler_params=_cparams(("parallel",)),
        name="final_norm",
    )(x2d, g)


def kernel(x, c, ctx, c_ctx, w_mod, b_mod, norm1_g, norm2_g, w_in, conv_w, a_log, dt_bias, gdn_norm_w, w_out,
           w_router, w_gate, w_up, w_down, norm_f):
    b, n, _ = x.shape
    nctx = ctx.shape[1]
    cap = 2 * n // NE
    i = 0

    cc = jnp.concatenate([c, c_ctx[None, :], jnp.zeros((8 - b - 1, D), F32)], axis=0)
    mod = _mod_call(cc, w_mod[i], b_mod[i][None, :])
    sh1, sc1, gt1, sh2, sc2, gt2 = [mod[:b, k * D:(k + 1) * D][:, None, :] for k in range(6)]
    sh1c = mod[b:b + 1, 0:D][:, None, :]
    sc1c = mod[b:b + 1, D:2 * D][:, None, :]

    w_main, w_ab = _wprep_call(w_in, i)
    conv_w8 = jnp.pad(conv_w[i], ((0, 8 - CONV_K), (0, 0)))
    alog_row = jnp.pad(a_log[i].reshape(1, NU), ((0, 0), (0, LANES - NU)))
    dtb_row = jnp.pad(dt_bias[i].reshape(1, NU), ((0, 0), (0, LANES - NU)))
    g1 = norm1_g[i][None, :]

    p_ctx, ab_ctx = _inproj_call(ctx.reshape(1, b * nctx, D), g1, sh1c, sc1c, w_main, w_ab, conv_w8,
                                 b * nctx, 1, 3, nctx)
    _, _, s_ctx = _gdn_call(p_ctx.reshape(b, nctx, 3 * GW), 0, ab_ctx.reshape(b, nctx, LANES), alog_row, dtb_row,
                            jnp.zeros((b, NU, HD, HD), F32))

    p_main, ab = _inproj_call(x, g1, sh1, sc1, w_main, w_ab, conv_w8, 1024, 0, 5, GRID_W)
    o_f, o_b, _ = _gdn_call(p_main, 1, ab, alog_row, dtb_row, s_ctx)
    yf = _fourier_call(p_main)

    w_r = jnp.pad(w_router[i], ((0, 0), (0, LANES - NE)))
    w_r_hi = w_r.astype(BF16)
    w_r = jnp.concatenate([w_r_hi, (w_r - w_r_hi.astype(F32)).astype(BF16)], axis=1)
    x1, hx2, lg_t = _outproj_call(yf, o_f, o_b, p_main, x, w_out[i].astype(BF16), gdn_norm_w[i][None, :], gt1,
                                  norm2_g[i][None, :], sh2, sc2, w_r)

    idx_c, val_c = _topk_call(lg_t, cap)
    idx = jnp.transpose(idx_c[:, :, :NE], (2, 0, 1))
    val = jnp.transpose(val_c[:, :, :NE], (2, 0, 1))
    idx_flat = (idx + (jnp.arange(b, dtype=I32) * n)[None, :, None]).reshape(NE * b * cap)
    vals = val.reshape(NE, 1, b * cap)

    y = _moe_ffn_call(idx_flat, hx2.reshape(b * n, D), vals, gt2, w_gate[i], w_up[i], w_down[i], cap)
    out = _moe_scatter_call(idx_flat, y, x1.reshape(b * n, D))
    return _final_call(out, norm_f[None, :]).reshape(b, n, D)
```

```python
import functools
import math

import numpy as np
import jax
import jax.numpy as jnp
from jax import lax
from jax.experimental import pallas as pl
from jax.experimental.pallas import tpu as pltpu

F32 = jnp.float32
BF16 = jnp.bfloat16
I32 = jnp.int32

D = 2048
SEQ_N = 4096
GRID_W = 64
FW = 1024
FG = 8
FGD = 128
GW = 1024
NH = 8
HD = 128
CONV_K = 5
N_DIR = 2
NE = 16
EFF = 1536
EPS = 1e-6
GC = 128
NU = N_DIR * NH
LANES = 128
VMEM_LIMIT = 56 * 1024 * 1024


def _sigmoid(x):
    return 1.0 / (1.0 + jnp.exp(-x))


def _silu(x):
    return x * _sigmoid(x)


def _softplus(x):
    return jnp.maximum(x, 0.0) + jnp.log(1.0 + jnp.exp(-jnp.abs(x)))


def _dot(a, b):
    return jnp.dot(a, b, preferred_element_type=F32)


def _bmm(a, b):
    return jnp.einsum('uij,ujk->uik', a.astype(BF16), b.astype(BF16), preferred_element_type=F32)


def _bmm_nt(a, b):
    return jnp.einsum('uik,ujk->uij', a, b, preferred_element_type=F32)


def _cparams(sem, vmem=VMEM_LIMIT):
    return pltpu.CompilerParams(dimension_semantics=sem, vmem_limit_bytes=vmem)


def _mod_kernel(c_ref, w_ref, b_ref, o_ref):
    s = _silu(c_ref[...]).astype(BF16)
    o_ref[...] = _dot(s, w_ref[...].astype(BF16)) + b_ref[...]


def _mod_call(cc, w_mod, b_mod):
    tn = 1024
    n = w_mod.shape[1]
    return pl.pallas_call(
        _mod_kernel,
        grid=(n // tn,),
        in_specs=[pl.BlockSpec((8, D), lambda j: (0, 0)),
                  pl.BlockSpec((D, tn), lambda j: (0, j)),
                  pl.BlockSpec((1, tn), lambda j: (0, j))],
        out_specs=pl.BlockSpec((8, tn), lambda j: (0, j)),
        out_shape=jax.ShapeDtypeStruct((8, n), F32),
        compiler_params=_cparams(("parallel",)),
        name="mod",
    )(cc, w_mod, b_mod)


def _wprep_kernel(w_ref, wm_ref, wab_ref):
    nmain = wm_ref.shape[1]
    ngate = w_ref.shape[2] - nmain
    wm_ref[...] = w_ref[0, :, :nmain].astype(BF16)
    wab_ref[...] = jnp.concatenate([w_ref[0, :, nmain:], jnp.zeros((w_ref.shape[1], LANES - ngate), F32)],
                                   axis=1).astype(BF16)


def _wprep_call(w_in, layer):
    _, d, c = w_in.shape
    nmain = FW + 4 * GW
    tr = 256
    return pl.pallas_call(
        _wprep_kernel,
        grid=(d // tr,),
        in_specs=[pl.BlockSpec((1, tr, c), lambda r: (layer, r, 0))],
        out_specs=[pl.BlockSpec((tr, nmain), lambda r: (r, 0)), pl.BlockSpec((tr, LANES), lambda r: (r, 0))],
        out_shape=[jax.ShapeDtypeStruct((d, nmain), BF16), jax.ShapeDtypeStruct((d, LANES), BF16)],
        compiler_params=_cparams(("parallel",)),
        name="wprep",
    )(w_in)


INPROJ_SUB = 256


def _inproj_kernel(x_ref, g_ref, sh_ref, sc_ref, w_ref, wab_ref, cw_ref, o_ref, ab_ref, h_scr, *, col0, row_len):
    c = pl.program_id(2) + col0

    @pl.when(pl.program_id(2) == 0)
    def _():
        x = x_ref[0]
        ms = jnp.mean(x * x, axis=-1, keepdims=True)
        y = x * lax.rsqrt(ms + EPS) * g_ref[...]
        h = (y * (1.0 + sc_ref[0]) + sh_ref[0]).astype(BF16)
        h_scr[...] = h
        ab_ref[0] = _dot(h, wab_ref[...])

    is_qkv = jnp.logical_and(c >= 1, c <= 3)

    @pl.when(jnp.logical_not(is_qkv))
    def _():
        o_ref[0] = _dot(h_scr[...], w_ref[...]).astype(o_ref.dtype)

    @pl.when(is_qkv)
    def _():
        half = CONV_K // 2
        qscale = jnp.where(c == 1, HD ** -0.5, 1.0).astype(F32)

        def shifted(x, off):
            zero = jnp.zeros((abs(off), HD), F32)
            return (jnp.concatenate([x[off:], zero], axis=0) if off > 0
                    else jnp.concatenate([zero, x[:off]], axis=0))

        for rb in range(x_ref.shape[1] // INPROJ_SUB):
            p = _dot(h_scr[rb * INPROJ_SUB:(rb + 1) * INPROJ_SUB, :], w_ref[...])
            for g in range(INPROJ_SUB // row_len):
                rows = slice(g * row_len, (g + 1) * row_len)
                for h in range(NH):
                    cols = slice(h * HD, (h + 1) * HD)
                    x = p[rows, cols]
                    acc = x * cw_ref[half:half + 1, cols]
                    for tap in range(CONV_K):
                        off = tap - half
                        if off != 0:
                            acc = acc + shifted(x, off) * cw_ref[tap:tap + 1, cols]
                    y = _silu(acc)
                    ss = jnp.sum(y * y, axis=-1, keepdims=True)
                    nrm = jnp.where(c < 3, lax.rsqrt(ss + EPS) * qscale, 1.0)
                    o_ref[0, rb * INPROJ_SUB + g * row_len:rb * INPROJ_SUB + (g + 1) * row_len, cols] = (
                        y * nrm).astype(o_ref.dtype)


def _inproj_call(x, g, sh, sc, w_main, w_ab, conv_w8, tm, col0, ncol, row_len):
    b, t, _ = x.shape
    tn = 1024
    assert tm % INPROJ_SUB == 0 and INPROJ_SUB % row_len == 0
    return pl.pallas_call(
        functools.partial(_inproj_kernel, col0=col0, row_len=row_len),
        grid=(b, t // tm, ncol),
        in_specs=[pl.BlockSpec((1, tm, D), lambda bi, i, j: (bi, i, 0)),
                  pl.BlockSpec((1, D), lambda bi, i, j: (0, 0)),
                  pl.BlockSpec((1, 1, D), lambda bi, i, j: (bi, 0, 0)),
                  pl.BlockSpec((1, 1, D), lambda bi, i, j: (bi, 0, 0)),
                  pl.BlockSpec((D, tn), lambda bi, i, j: (0, j + col0)),
                  pl.BlockSpec((D, LANES), lambda bi, i, j: (0, 0)),
                  pl.BlockSpec((8, GW), lambda bi, i, j: (0, jnp.clip(j + col0 - 1, 0, 2)))],
        out_specs=[pl.BlockSpec((1, tm, tn), lambda bi, i, j: (bi, i, j)),
                   pl.BlockSpec((1, tm, LANES), lambda bi, i, j: (bi, i, 0))],
        out_shape=[jax.ShapeDtypeStruct((b, t, ncol * tn), BF16),
                   jax.ShapeDtypeStruct((b, t, LANES), F32)],
        scratch_shapes=[pltpu.VMEM((tm, D), BF16)],
        compiler_params=_cparams(("parallel", "parallel", "arbitrary")),
        name="inproj",
    )(x, g, sh, sc, w_main, w_ab, conv_w8)


def _gdn_kernel(qf_ref, kf_ref, vf_ref, abf_ref, qb_ref, kb_ref, vb_ref, abb_ref, alog_ref, dtb_ref, s0_ref,
                of_ref, ob_ref, sout_ref, s_scr):
    n = pl.program_id(1)

    @pl.when(n == 0)
    def _():
        s_scr[...] = s0_ref[0]

    row = lax.broadcasted_iota(I32, (GC, GC), 0)
    col = lax.broadcasted_iota(I32, (GC, GC), 1)
    eye = (row == col).astype(F32)
    m_parts, rhs_parts, qd_parts, kdt_parts, qk_parts, gt_parts = [], [], [], [], [], []
    for d in range(N_DIR):
        q_ref, k_ref, v_ref, ab_ref = ((qf_ref, kf_ref, vf_ref, abf_ref) if d == 0
                                       else (qb_ref, kb_ref, vb_ref, abb_ref))
        incl = (col <= row) if d == 0 else (col >= row)
        strict = (col < row) if d == 0 else (col > row)
        ab = ab_ref[0]
        g_all = -jnp.exp(alog_ref[...]) * _softplus(ab + dtb_ref[...])
        beta_all = _sigmoid(ab)
        gc_all = jnp.dot(incl.astype(F32), g_all, preferred_element_type=F32,
                         precision=lax.Precision.HIGHEST)
        gc_t = gc_all.T
        last = GC - 1 if d == 0 else 0
        c0 = d * NH
        cb = N_DIR * NH + d * NH
        heads = range(NH)
        gcol = jnp.stack([gc_all[:, c0 + h:c0 + h + 1] for h in heads])
        grow = jnp.stack([gc_t[c0 + h:c0 + h + 1, :] for h in heads])
        glast = jnp.stack([gc_all[last:last + 1, c0 + h:c0 + h + 1] for h in heads])
        beta = jnp.stack([beta_all[:, cb + h:cb + h + 1] for h in heads])
        q = jnp.stack([q_ref[0, :, h * HD:(h + 1) * HD] for h in heads]).astype(F32)
        k = jnp.stack([k_ref[0, :, h * HD:(h + 1) * HD] for h in heads])
        kf = k.astype(F32)
        v = jnp.stack([v_ref[0, :, h * HD:(h + 1) * HD] for h in heads]).astype(F32)
        decay = jnp.where(incl[None], jnp.exp(jnp.where(incl[None], gcol - grow, 0.0)), 0.0)
        egc = jnp.exp(gcol)
        kbeta = kf * beta
        a2 = _bmm_nt(jnp.concatenate([kbeta, q], axis=1).astype(BF16), k)
        m_parts.append(jnp.where(strict[None], a2[:, :GC] * decay, 0.0))
        rhs_parts.append(jnp.concatenate([v * beta, kbeta * egc], axis=2).astype(BF16))
        qd_parts.append((q * egc).astype(BF16))
        kdt_parts.append(jnp.swapaxes(kf * jnp.exp(glast - gcol), 1, 2).astype(BF16))
        qk_parts.append((a2[:, GC:] * decay).astype(BF16))
        gt_parts.append(jnp.exp(glast))

    cat = lambda parts: jnp.concatenate(parts, axis=0)
    m, rhs, qd, kdt, qk, gtot = (cat(t) for t in (m_parts, rhs_parts, qd_parts, kdt_parts, qk_parts, gt_parts))
    blk = lambda s: ((row >> int(math.log2(s))) == (col >> int(math.log2(s))))[None]
    mb = jnp.where(blk(8), m, 0.0)
    p = eye[None] - mb
    m2 = _bmm(mb, mb)
    p = p + _bmm(p, m2)
    p = p + _bmm(p, _bmm(m2, m2))
    s = 8
    while s < GC:
        cpart = jnp.where(jnp.logical_and(blk(2 * s), jnp.logical_not(blk(s))), m, 0.0)
        p = p - _bmm(p, _bmm(cpart, p))
        s *= 2
    uw = _bmm(p, rhs)

    st = s_scr[...]
    r = _bmm(jnp.concatenate([uw[:, :, HD:].astype(BF16), qd], axis=1), st)
    vb = (uw[:, :, :HD] - r[:, :GC]).astype(BF16)
    o = r[:, GC:] + _bmm(qk, vb)
    s_scr[...] = st * gtot + _bmm(kdt, vb)
    for un in range(NU):
        h = un % NH
        if un < NH:
            of_ref[0, :, h * HD:(h + 1) * HD] = o[un].astype(of_ref.dtype)
        else:
            ob_ref[0, :, h * HD:(h + 1) * HD] = o[un].astype(ob_ref.dtype)

    @pl.when(n == pl.num_programs(1) - 1)
    def _():
        sout_ref[0] = s_scr[...]


def _gdn_call(qkv, col0, ab, alog_row, dtb_row, s0):
    b, t, _ = qkv.shape
    nc = t // GC
    blk = lambda c, rev: pl.BlockSpec((1, GC, GW), (lambda bi, n: (bi, nc - 1 - n, c + col0)) if rev
                                      else (lambda bi, n: (bi, n, c + col0)))
    abblk = lambda rev: pl.BlockSpec((1, GC, LANES), (lambda bi, n: (bi, nc - 1 - n, 0)) if rev
                                     else (lambda bi, n: (bi, n, 0)))
    sblk = pl.BlockSpec((1, NU, HD, HD), lambda bi, n: (bi, 0, 0, 0))
    return pl.pallas_call(
        _gdn_kernel,
        grid=(b, nc),
        in_specs=[blk(0, False), blk(1, False), blk(2, False), abblk(False),
                  blk(0, True), blk(1, True), blk(2, True), abblk(True),
                  pl.BlockSpec((1, LANES), lambda bi, n: (0, 0)),
                  pl.BlockSpec((1, LANES), lambda bi, n: (0, 0)),
                  sblk],
        out_specs=[pl.BlockSpec((1, GC, GW), lambda bi, n: (bi, n, 0)),
                   pl.BlockSpec((1, GC, GW), lambda bi, n: (bi, nc - 1 - n, 0)),
                   sblk],
        out_shape=[jax.ShapeDtypeStruct((b, t, GW), BF16),
                   jax.ShapeDtypeStruct((b, t, GW), BF16),
                   jax.ShapeDtypeStruct((b, NU, HD, HD), F32)],
        scratch_shapes=[pltpu.VMEM((NU, HD, HD), F32)],
        compiler_params=_cparams(("parallel", "arbitrary")),
        name="gdn",
    )(qkv, qkv, qkv, ab, qkv, qkv, qkv, ab, alog_row, dtb_row, s0)


def _dft_tables():
    r = GRID_W
    a = 2.0 * np.pi * np.outer(np.arange(r), np.arange(r)) / r
    c64, s64 = np.cos(a), np.sin(a)
    ac = 2.0 * np.pi * np.outer(np.arange(FGD), np.arange(FGD)) / FGD
    cc, sc = np.cos(ac), np.sin(ac)
    at = (2.0 * np.pi * np.outer(np.arange(r), np.arange(r)) / (r * r)).reshape(r * r, 1)
    chan = np.concatenate([cc, -sc], axis=1)
    row_cs = np.concatenate([c64, s64], axis=0)
    col_re = np.concatenate([c64, s64], axis=1)
    as_bf16 = lambda t: jnp.asarray(t, F32).astype(BF16)
    return (as_bf16(chan), as_bf16(row_cs), as_bf16(col_re),
            jnp.asarray(np.cos(at), F32), jnp.asarray(np.sin(at), F32))


def _fourier_kernel(x_ref, chan_ref, rowcs_ref, colre_ref, twc_ref, tws_ref, o_ref, u_scr, v_scr, y_scr):
    r = GRID_W

    def swap(t):
        return jnp.swapaxes(t.reshape(r, r, FGD), 0, 1).reshape(r * r, FGD)

    u = _dot(x_ref[0], chan_ref[...])
    u_scr[0] = swap(u[:, :FGD])
    u_scr[1] = swap(u[:, FGD:])

    def over_n2(n1, carry):
        rows = pl.ds(pl.multiple_of(n1 * r, r), r)
        x = jnp.concatenate([u_scr[0, rows, :], u_scr[1, rows, :]], axis=1).astype(BF16)
        cs = _dot(rowcs_ref[...], x)
        vr = cs[:r, :FGD] + cs[r:, FGD:]
        vi = cs[:r, FGD:] - cs[r:, :FGD]
        tc = twc_ref[rows, :]
        ts = tws_ref[rows, :]
        v_scr[0, rows, :] = vr * tc + vi * ts
        v_scr[1, rows, :] = vi * tc - vr * ts
        return carry

    lax.fori_loop(0, r, over_n2, 0, unroll=16)
    u_scr[0] = swap(v_scr[0])
    u_scr[1] = swap(v_scr[1])

    def over_n1(k2, carry):
        rows = pl.ds(pl.multiple_of(k2 * r, r), r)
        st = jnp.concatenate([u_scr[0, rows, :], u_scr[1, rows, :]], axis=0).astype(BF16)
        y_scr[rows, :] = _dot(colre_ref[...], st) * (1.0 / math.sqrt(SEQ_N * FGD))
        return carry

    lax.fori_loop(0, r, over_n1, 0, unroll=16)
    o_ref[0] = swap(y_scr[...]).astype(o_ref.dtype)


def _fourier_call(p_main):
    b, n, _ = p_main.shape
    r = GRID_W
    chan, rowcs, colre, twc, tws = _dft_tables()
    twc = jnp.broadcast_to(twc, (n, LANES))
    tws = jnp.broadcast_to(tws, (n, LANES))
    const = lambda shape: pl.BlockSpec(shape, lambda bi, g: (0, 0))
    return pl.pallas_call(
        _fourier_kernel,
        grid=(b, FG),
        in_specs=[pl.BlockSpec((1, n, FGD), lambda bi, g: (bi, 0, g)),
                  const((FGD, 2 * FGD)), const((2 * r, r)), const((r, 2 * r)),
                  const((n, LANES)), const((n, LANES))],
        out_specs=pl.BlockSpec((1, n, FGD), lambda bi, g: (bi, 0, g)),
        out_shape=jax.ShapeDtypeStruct((b, n, FW), BF16),
        scratch_shapes=[pltpu.VMEM((2, n, FGD), F32), pltpu.VMEM((2, n, FGD), F32), pltpu.VMEM((n, FGD), F32)],
        compiler_params=_cparams(("parallel", "parallel")),
        name="fourier",
    )(p_main, chan, rowcs, colre, twc, tws)


def _outproj_kernel(yf_ref, of_ref, ob_ref, z_ref, x_ref, wout_ref, gnw_ref, gt1_ref, g2_ref, sh2_ref, sc2_ref,
                    wr_ref, x1_ref, hx_ref, lg_ref):
    sub = 128
    for rb in range(x_ref.shape[1] // sub):
        rows = slice(rb * sub, (rb + 1) * sub)
        parts = [yf_ref[0, rows, :]]
        for h in range(NH):
            cols = slice(h * HD, (h + 1) * HD)
            oh = of_ref[0, rows, cols].astype(F32) + ob_ref[0, rows, cols].astype(F32)
            ms = jnp.mean(oh * oh, axis=-1, keepdims=True)
            on = oh * lax.rsqrt(ms + EPS) * gnw_ref[...]
            parts.append((on * _silu(z_ref[0, rows, cols].astype(F32))).astype(BF16))
        mix = jnp.concatenate(parts, axis=1)
        x1 = x_ref[0, rows, :] + gt1_ref[0] * _dot(mix, wout_ref[...])
        x1_ref[0, rows, :] = x1
        ms = jnp.mean(x1 * x1, axis=-1, keepdims=True)
        hx = x1 * lax.rsqrt(ms + EPS) * g2_ref[...] * (1.0 + sc2_ref[0]) + sh2_ref[0]
        hx_ref[0, rows, :] = hx
        hx_hi = hx.astype(BF16)
        hx_lo = (hx - hx_hi.astype(F32)).astype(BF16)
        l2 = _dot(hx_hi, wr_ref[...])
        lg = l2[:, :LANES] + l2[:, LANES:] + _dot(hx_lo, wr_ref[:, :LANES])
        lg_ref[0, :, rows] = lg.T[:NE, :]


def _outproj_call(yf, o_f, o_b, p_main, x, w_out, gnw, gt1, g2, sh2, sc2, w_r):
    b, t, _ = x.shape
    tm = 256
    zcol = p_main.shape[2] // GW - 1
    row = lambda c: pl.BlockSpec((1, tm, c), lambda bi, i: (bi, i, 0))
    vec = pl.BlockSpec((1, 1, D), lambda bi, i: (bi, 0, 0))
    return pl.pallas_call(
        _outproj_kernel,
        grid=(b, t // tm),
        in_specs=[row(FW), row(GW), row(GW),
                  pl.BlockSpec((1, tm, GW), lambda bi, i: (bi, i, zcol)),
                  row(D),
                  pl.BlockSpec((D, D), lambda bi, i: (0, 0)),
                  pl.BlockSpec((1, HD), lambda bi, i: (0, 0)),
                  vec,
                  pl.BlockSpec((1, D), lambda bi, i: (0, 0)),
                  vec, vec,
                  pl.BlockSpec((D, 2 * LANES), lambda bi, i: (0, 0))],
        out_specs=[row(D), row(D), pl.BlockSpec((1, NE, tm), lambda bi, i: (bi, 0, i))],
        out_shape=[jax.ShapeDtypeStruct((b, t, D), F32),
                   jax.ShapeDtypeStruct((b, t, D), F32),
                   jax.ShapeDtypeStruct((b, NE, t), F32)],
        compiler_params=_cparams(("parallel", "parallel")),
        name="outproj",
    )(yf, o_f, o_b, p_main, x, w_out, gnw, gt1, g2, sh2, sc2, w_r)


def _lane_cumsum(x):
    n = x.shape[-1]
    lane = lax.broadcasted_iota(I32, x.shape, x.ndim - 1)
    sh = 1
    while sh < n:
        x = x + jnp.where(lane >= sh, pltpu.roll(x, shift=sh, axis=x.ndim - 1), 0.0)
        sh *= 2
    return x


def _topk_kernel(lg_ref, idx_ref, val_ref, tab_scr, rend_scr, *, cap, nt):
    nr = NE * nt
    lg = lg_ref[0]
    lg3 = lg.reshape(NE, nt, LANES)
    ex3 = jnp.exp(lg3 - jnp.max(lg3, axis=0, keepdims=True))
    aff = (ex3 / jnp.sum(ex3, axis=0, keepdims=True)).reshape(nr, LANES)

    rr = lax.broadcasted_iota(I32, (nr, nr), 0)
    rc = lax.broadcasted_iota(I32, (nr, nr), 1)
    ntb = int(math.log2(nt))
    same = (rr >> ntb) == (rc >> ntb)
    grp = same.astype(BF16)
    before = jnp.logical_and(same, rc < rr).astype(BF16)

    def per_expert_sum(mat, mask):
        return jnp.sum(_dot(mat, mask.astype(BF16)), axis=1, keepdims=True)

    def search(i, cur):
        cand = cur | (1 << (30 - i))
        cnt = per_expert_sum(grp, (aff >= pltpu.bitcast(cand, F32)).astype(F32))
        return jnp.where(cnt >= cap, cand, cur)

    thr = pltpu.bitcast(lax.fori_loop(0, 31, search, jnp.zeros((nr, 1), I32)), F32)
    gt = aff > thr
    eq = (aff == thr).astype(F32)

    def rank_of(m):
        return _lane_cumsum(m) + per_expert_sum(before, m)

    n_gt = per_expert_sum(grp, gt.astype(F32))
    sel = jnp.logical_or(gt, jnp.logical_and(eq > 0.0, rank_of(eq) <= cap - n_gt))
    selm = sel.astype(F32)
    rsel = jnp.where(sel, rank_of(selm), 0.0)
    rprev = per_expert_sum(before, selm)
    rend_scr[...] = jnp.concatenate([jnp.broadcast_to(rprev, (nr, LANES)),
                                     jnp.broadcast_to(rprev + jnp.sum(selm, axis=1, keepdims=True), (nr, LANES))],
                                    axis=1)
    hi = jnp.floor(rsel * (1.0 / 32.0))
    a1 = aff.astype(BF16)
    a2 = (aff - a1.astype(F32)).astype(BF16)
    a3 = (aff - a1.astype(F32) - a2.astype(F32)).astype(BF16)
    tab_scr[...] = jnp.concatenate([hi.astype(BF16), (rsel - 32.0 * hi).astype(BF16), a1, a2, a3], axis=1)
    idx_ref[0] = jnp.zeros((cap, LANES), I32)
    val_ref[0] = jnp.zeros((cap, LANES), F32)
    slot = (lax.broadcasted_iota(I32, (cap, 1), 0) + 1).astype(F32)
    lane = lax.broadcasted_iota(I32, (cap, LANES), 1)
    lanef = lane.astype(F32)
    pad = jnp.full((LANES - nt, 2 * LANES), 2.0 * cap, F32)

    def per_expert(e, carry):
        r0 = pl.multiple_of(e * nt, nt)
        rt = jnp.concatenate([rend_scr[pl.ds(r0, nt), :], pad], axis=0)
        prev_l = rt[:, :LANES].T[0:1, :]
        end_l = rt[:, LANES:].T[0:1, :]
        inrow = jnp.logical_and(prev_l < slot, slot <= end_l)
        tab = jnp.concatenate([tab_scr[pl.ds(r0, nt), :], jnp.zeros((LANES - nt, 5 * LANES), BF16)], axis=0)
        g = _dot(inrow.astype(BF16), tab)
        hit = (32.0 * g[:, :LANES] + g[:, LANES:2 * LANES]) == slot
        ag = g[:, 2 * LANES:3 * LANES] + g[:, 3 * LANES:4 * LANES] + g[:, 4 * LANES:]
        pos = (jnp.sum(jnp.where(inrow, lanef, 0.0), axis=1, keepdims=True) * float(LANES)
               + jnp.sum(jnp.where(hit, lanef, 0.0), axis=1, keepdims=True))
        val = jnp.sum(jnp.where(hit, ag, 0.0), axis=1, keepdims=True)
        idx_ref[0] = jnp.where(lane == e, pos.astype(I32), idx_ref[0])
        val_ref[0] = jnp.where(lane == e, val, val_ref[0])
        return carry

    lax.fori_loop(0, NE, per_expert, 0)


def _topk_call(lg_t, cap):
    b, _, t = lg_t.shape
    nt = t // LANES
    nr = NE * nt
    return pl.pallas_call(
        functools.partial(_topk_kernel, cap=cap, nt=nt),
        grid=(b,),
        in_specs=[pl.BlockSpec((1, nr, LANES), lambda bi: (bi, 0, 0))],
        out_specs=[pl.BlockSpec((1, cap, LANES), lambda bi: (bi, 0, 0)),
                   pl.BlockSpec((1, cap, LANES), lambda bi: (bi, 0, 0))],
        out_shape=[jax.ShapeDtypeStruct((b, cap, LANES), I32),
                   jax.ShapeDtypeStruct((b, cap, LANES), F32)],
        scratch_shapes=[pltpu.VMEM((nr, 5 * LANES), BF16), pltpu.VMEM((nr, 2 * LANES), F32)],
        compiler_params=_cparams(("parallel",)),
        name="topk",
    )(lg_t.reshape(b, nr, LANES))


MOE_TF = 256
MOE_TN = 256
MOE_MC = 512
MOE_NF = EFF // MOE_TF
MOE_NN = D // MOE_TN
MOE_GC = 352
MOE_GPAD = 32


def _moe_ffn_kernel(idx_ref, hx_hbm, val_ref, gt2_ref, wg_ref, wu_ref, wd_ref, y_ref,
                    xg_scr, hid_scr, stg_scr, vcol_scr, gsem, *, rows):
    e = pl.program_id(0)
    s = pl.program_id(1)
    cur = e % 2
    nxt = jnp.minimum(e + 1, pl.num_programs(0) - 1)

    def issue(expert, c):
        base = expert * rows + c * MOE_GC
        for r in range(MOE_GC):
            pltpu.make_async_copy(hx_hbm.at[pl.ds(idx_ref[base + r], 1), :],
                                  stg_scr.at[pl.ds(r, 1), :], gsem).start(priority=r % 2)

    def land(c, buf):
        pltpu.make_async_copy(hx_hbm.at[pl.ds(0, MOE_GC), :], stg_scr, gsem).wait()
        r0 = pl.multiple_of(c * MOE_GC, MOE_GPAD)
        xg_scr[buf, pl.ds(r0, MOE_GC), :] = stg_scr[...].astype(BF16)

    @pl.when(jnp.logical_and(e == 0, s == 0))
    def _():
        for c in range(MOE_NF):
            issue(0, c)
            land(c, 0)

    @pl.when(jnp.logical_and(s >= 1, s <= MOE_NF))
    def _():
        land(s - 1, 1 - cur)

    @pl.when(s < MOE_NF)
    def _():
        xs = xg_scr[cur, 0:rows, :]
        hid = (_silu(_dot(xs, wg_ref[0].astype(BF16))) * _dot(xs, wu_ref[0].astype(BF16))).astype(BF16)
        hid_scr[:, pl.ds(pl.multiple_of(s * MOE_TF, MOE_TF), MOE_TF)] = hid
        issue(nxt, s)

    @pl.when(s == MOE_NF)
    def _():
        vcol_scr[...] = jnp.broadcast_to(val_ref[0], (LANES, rows)).T

    @pl.when(s >= MOE_NF)
    def _():
        y = _dot(hid_scr[...], wd_ref[0].astype(BF16))
        for mc in range(rows // MOE_MC):
            sl = slice(mc * MOE_MC, (mc + 1) * MOE_MC)
            y_ref[0, sl, :] = (y[sl] * vcol_scr[sl, 0:1] * gt2_ref[mc, 0]).astype(y_ref.dtype)


def _moe_ffn_call(idx_flat, hx2, vals, gt2, w_gate, w_up, w_down, cap):
    rows = vals.shape[2]
    rows_pad = MOE_NF * MOE_GC
    assert rows // MOE_MC == gt2.shape[0] and cap == MOE_MC and rows_pad >= rows and MOE_GC % MOE_GPAD == 0
    idx_flat = jnp.pad(idx_flat, (0, rows_pad - rows))
    gt2t = gt2.reshape(gt2.shape[0], MOE_NN, 1, MOE_TN)
    ph1 = lambda s: jnp.minimum(s, MOE_NF - 1)
    ph2 = lambda s: jnp.maximum(s - MOE_NF, 0)
    grid_spec = pltpu.PrefetchScalarGridSpec(
        num_scalar_prefetch=1,
        grid=(NE, MOE_NF + MOE_NN),
        in_specs=[pl.BlockSpec(memory_space=pl.ANY),
                  pl.BlockSpec((1, 1, rows), lambda e, s, idx: (e, 0, 0)),
                  pl.BlockSpec((gt2.shape[0], 1, 1, MOE_TN), lambda e, s, idx: (0, ph2(s), 0, 0)),
                  pl.BlockSpec((1, D, MOE_TF), lambda e, s, idx: (e, 0, ph1(s))),
                  pl.BlockSpec((1, D, MOE_TF), lambda e, s, idx: (e, 0, ph1(s))),
                  pl.BlockSpec((1, EFF, MOE_TN), lambda e, s, idx: (e, 0, ph2(s)))],
        out_specs=pl.BlockSpec((1, rows, MOE_TN), lambda e, s, idx: (e, 0, ph2(s))),
        scratch_shapes=[pltpu.VMEM((2, rows_pad, D), BF16),
                        pltpu.VMEM((rows, EFF), BF16),
                        pltpu.VMEM((MOE_GC, D), F32),
                        pltpu.VMEM((rows, LANES), F32),
                        pltpu.SemaphoreType.DMA(())],
    )
    return pl.pallas_call(
        functools.partial(_moe_ffn_kernel, rows=rows),
        grid_spec=grid_spec,
        out_shape=jax.ShapeDtypeStruct((NE, rows, D), BF16),
        compiler_params=_cparams(("arbitrary", "arbitrary")),
        name="moe_ffn",
    )(idx_flat, hx2, vals, gt2t, w_gate, w_up, w_down)


MOE_SC = 256
MOE_SS = 8


def _moe_scatter_kernel(idx_ref, y_ref, res_hbm, out_hbm, buf_scr, rsem, wsem, *, rows):
    del res_hbm
    e = pl.program_id(0)
    st = pl.program_id(1)
    nst = pl.num_programs(1)
    c0 = st * MOE_SS

    def reads(chunk, k):
        base = e * rows + chunk * MOE_SC
        for r in range(MOE_SC):
            pltpu.make_async_copy(out_hbm.at[pl.ds(idx_ref[base + r], 1), :],
                                  buf_scr.at[k, pl.ds(r, 1), :], rsem.at[k]).start(priority=r % 2)

    def writes(chunk, k):
        base = e * rows + chunk * MOE_SC
        for r in range(MOE_SC):
            pltpu.make_async_copy(buf_scr.at[k, pl.ds(r, 1), :],
                                  out_hbm.at[pl.ds(idx_ref[base + r], 1), :], wsem.at[k]).start(priority=r % 2)

    def wait_all(sem, k):
        pltpu.make_async_copy(out_hbm.at[pl.ds(0, MOE_SC), :], buf_scr.at[k], sem.at[k]).wait()

    @pl.when(st == 0)
    def _():
        reads(c0, 0)
        reads(c0 + 1, 1)

    for k in range(MOE_SS):
        wait_all(rsem, k)
        buf_scr[k] = buf_scr[k] + y_ref[0, k * MOE_SC:(k + 1) * MOE_SC, :].astype(F32)
        writes(c0 + k, k)
        if k + 2 < MOE_SS:
            @pl.when(st > 0)
            def _():
                wait_all(wsem, k + 2)
            reads(c0 + k + 2, k + 2)
        else:
            @pl.when(st + 1 < nst)
            def _():
                wait_all(wsem, k + 2 - MOE_SS)
                reads(c0 + k + 2, k + 2 - MOE_SS)

    @pl.when(st == nst - 1)
    def _():
        for k in range(MOE_SS):
            wait_all(wsem, k)


def _moe_scatter_call(idx_flat, y, x1):
    rows = y.shape[1]
    grid_spec = pltpu.PrefetchScalarGridSpec(
        num_scalar_prefetch=1,
        grid=(NE, rows // (MOE_SC * MOE_SS)),
        in_specs=[pl.BlockSpec((1, MOE_SC * MOE_SS, D), lambda e, st, idx: (e, st, 0)),
                  pl.BlockSpec(memory_space=pl.ANY)],
        out_specs=pl.BlockSpec(memory_space=pl.ANY),
        scratch_shapes=[pltpu.VMEM((MOE_SS, MOE_SC, D), F32),
                        pltpu.SemaphoreType.DMA((MOE_SS,)),
                        pltpu.SemaphoreType.DMA((MOE_SS,))],
    )
    return pl.pallas_call(
        functools.partial(_moe_scatter_kernel, rows=rows),
        grid_spec=grid_spec,
        out_shape=jax.ShapeDtypeStruct(x1.shape, F32),
        input_output_aliases={2: 0},
        compiler_params=_cparams(("arbitrary", "arbitrary")),
        name="moe_scatter",
    )(idx_flat, y, x1)


def _final_kernel(x_ref, g_ref, o_ref):
    x = x_ref[...]
    ms = jnp.mean(x * x, axis=-1, keepdims=True)
    o_ref[...] = x * lax.rsqrt(ms + EPS) * g_ref[...]


def _final_call(x2d, g):
    m = x2d.shape[0]
    tm = 1024
    return pl.pallas_call(
        _final_kernel,
        grid=(m // tm,),
        in_specs=[pl.BlockSpec((tm, D), lambda i: (i, 0)), pl.BlockSpec((1, D), lambda i: (0, 0))],
        out_specs=pl.BlockSpec((tm, D), lambda i: (i, 0)),
        out_shape=jax.ShapeDtypeStruct(x2d.shape, F32),
        compiler_params=_cparams(("parallel",)),
        name="final_norm",
    )(x2d, g)


def kernel(x, c, ctx, c_ctx, w_mod, b_mod, norm1_g, norm2_g, w_in, conv_w, a_log, dt_bias, gdn_norm_w, w_out,
           w_router, w_gate, w_up, w_down, norm_f):
    b, n, _ = x.shape
    nctx = ctx.shape[1]
    cap = 2 * n // NE
    i = 0

    cc = jnp.concatenate([c, c_ctx[None, :], jnp.zeros((8 - b - 1, D), F32)], axis=0)
    mod = _mod_call(cc, w_mod[i], b_mod[i][None, :])
    sh1, sc1, gt1, sh2, sc2, gt2 = [mod[:b, k * D:(k + 1) * D][:, None, :] for k in range(6)]
    sh1c = mod[b:b + 1, 0:D][:, None, :]
    sc1c = mod[b:b + 1, D:2 * D][:, None, :]

    w_main, w_ab = _wprep_call(w_in, i)
    conv_w8 = jnp.pad(conv_w[i], ((0, 8 - CONV_K), (0, 0)))
    alog_row = jnp.pad(a_log[i].reshape(1, NU), ((0, 0), (0, LANES - NU)))
    dtb_row = jnp.pad(dt_bias[i].reshape(1, NU), ((0, 0), (0, LANES - NU)))
    g1 = norm1_g[i][None, :]

    p_ctx, ab_ctx = _inproj_call(ctx.reshape(1, b * nctx, D), g1, sh1c, sc1c, w_main, w_ab, conv_w8,
                                 b * nctx, 1, 3, nctx)
    _, _, s_ctx = _gdn_call(p_ctx.reshape(b, nctx, 3 * GW), 0, ab_ctx.reshape(b, nctx, LANES), alog_row, dtb_row,
                            jnp.zeros((b, NU, HD, HD), F32))

    p_main, ab = _inproj_call(x, g1, sh1, sc1, w_main, w_ab, conv_w8, 1024, 0, 5, GRID_W)
    o_f, o_b, _ = _gdn_call(p_main, 1, ab, alog_row, dtb_row, s_ctx)
    yf = _fourier_call(p_main)

    w_r = jnp.pad(w_router[i], ((0, 0), (0, LANES - NE)))
    w_r_hi = w_r.astype(BF16)
    w_r = jnp.concatenate([w_r_hi, (w_r - w_r_hi.astype(F32)).astype(BF16)], axis=1)
    x1, hx2, lg_t = _outproj_call(yf, o_f, o_b, p_main, x, w_out[i].astype(BF16), gdn_norm_w[i][None, :], gt1,
                                  norm2_g[i][None, :], sh2, sc2, w_r)

    idx_c, val_c = _topk_call(lg_t, cap)
    idx = jnp.transpose(idx_c[:, :, :NE], (2, 0, 1))
    val = jnp.transpose(val_c[:, :, :NE], (2, 0, 1))
    idx_flat = (idx + (jnp.arange(b, dtype=I32) * n)[None, :, None]).reshape(NE * b * cap)
    vals = val.reshape(NE, 1, b * cap)

    y = _moe_ffn_call(idx_flat, hx2.reshape(b * n, D), vals, gt2, w_gate[i], w_up[i], w_down[i], cap)
    out = _moe_scatter_call(idx_flat, y, x1.reshape(b * n, D))
    return _final_call(out, norm_f[None, :]).reshape(b, n, D)
```

```python
import functools
import math

import numpy as np
import jax
import jax.numpy as jnp
from jax import lax
from jax.experimental import pallas as pl
from jax.experimental.pallas import tpu as pltpu

F32 = jnp.float32
BF16 = jnp.bfloat16
I32 = jnp.int32

D = 2048
SEQ_N = 4096
GRID_W = 64
FW = 1024
FG = 8
FGD = 128
GW = 1024
NH = 8
HD = 128
CONV_K = 5
N_DIR = 2
NE = 16
EFF = 1536
EPS = 1e-6
GC = 128
NU = N_DIR * NH
LANES = 128
VMEM_LIMIT = 56 * 1024 * 1024


def _sigmoid(x):
    return 1.0 / (1.0 + jnp.exp(-x))


def _silu(x):
    return x * _sigmoid(x)


def _softplus(x):
    return jnp.maximum(x, 0.0) + jnp.log(1.0 + jnp.exp(-jnp.abs(x)))


def _dot(a, b):
    return jnp.dot(a, b, preferred_element_type=F32)


def _bmm(a, b):
    return jnp.einsum('uij,ujk->uik', a.astype(BF16), b.astype(BF16), preferred_element_type=F32)


def _bmm_nt(a, b):
    return jnp.einsum('uik,ujk->uij', a, b, preferred_element_type=F32)


def _cparams(sem, vmem=VMEM_LIMIT):
    return pltpu.CompilerParams(dimension_semantics=sem, vmem_limit_bytes=vmem)


def _mod_kernel(c_ref, w_ref, b_ref, o_ref):
    s = _silu(c_ref[...]).astype(BF16)
    o_ref[...] = _dot(s, w_ref[...].astype(BF16)) + b_ref[...]


def _mod_call(cc, w_mod, b_mod):
    tn = 1024
    n = w_mod.shape[1]
    return pl.pallas_call(
        _mod_kernel,
        grid=(n // tn,),
        in_specs=[pl.BlockSpec((8, D), lambda j: (0, 0)),
                  pl.BlockSpec((D, tn), lambda j: (0, j)),
                  pl.BlockSpec((1, tn), lambda j: (0, j))],
        out_specs=pl.BlockSpec((8, tn), lambda j: (0, j)),
        out_shape=jax.ShapeDtypeStruct((8, n), F32),
        compiler_params=_cparams(("parallel",)),
        name="mod",
    )(cc, w_mod, b_mod)


def _wprep_kernel(w_ref, wm_ref, wab_ref):
    nmain = wm_ref.shape[1]
    ngate = w_ref.shape[2] - nmain
    wm_ref[...] = w_ref[0, :, :nmain].astype(BF16)
    wab_ref[...] = jnp.concatenate([w_ref[0, :, nmain:], jnp.zeros((w_ref.shape[1], LANES - ngate), F32)],
                                   axis=1).astype(BF16)


def _wprep_call(w_in, layer):
    _, d, c = w_in.shape
    nmain = FW + 4 * GW
    tr = 256
    return pl.pallas_call(
        _wprep_kernel,
        grid=(d // tr,),
        in_specs=[pl.BlockSpec((1, tr, c), lambda r: (layer, r, 0))],
        out_specs=[pl.BlockSpec((tr, nmain), lambda r: (r, 0)), pl.BlockSpec((tr, LANES), lambda r: (r, 0))],
        out_shape=[jax.ShapeDtypeStruct((d, nmain), BF16), jax.ShapeDtypeStruct((d, LANES), BF16)],
        compiler_params=_cparams(("parallel",)),
        name="wprep",
    )(w_in)


INPROJ_SUB = 256


def _inproj_kernel(x_ref, g_ref, sh_ref, sc_ref, w_ref, wab_ref, cw_ref, o_ref, ab_ref, h_scr, *, col0, row_len):
    c = pl.program_id(2) + col0

    @pl.when(pl.program_id(2) == 0)
    def _():
        x = x_ref[0]
        ms = jnp.mean(x * x, axis=-1, keepdims=True)
        y = x * lax.rsqrt(ms + EPS) * g_ref[...]
        h = (y * (1.0 + sc_ref[0]) + sh_ref[0]).astype(BF16)
        h_scr[...] = h
        ab_ref[0] = _dot(h, wab_ref[...])

    is_qkv = jnp.logical_and(c >= 1, c <= 3)

    @pl.when(jnp.logical_not(is_qkv))
    def _():
        o_ref[0] = _dot(h_scr[...], w_ref[...]).astype(o_ref.dtype)

    @pl.when(is_qkv)
    def _():
        half = CONV_K // 2
        qscale = jnp.where(c == 1, HD ** -0.5, 1.0).astype(F32)

        def shifted(x, off):
            zero = jnp.zeros((abs(off), HD), F32)
            return (jnp.concatenate([x[off:], zero], axis=0) if off > 0
                    else jnp.concatenate([zero, x[:off]], axis=0))

        for rb in range(x_ref.shape[1] // INPROJ_SUB):
            p = _dot(h_scr[rb * INPROJ_SUB:(rb + 1) * INPROJ_SUB, :], w_ref[...])
            for g in range(INPROJ_SUB // row_len):
                rows = slice(g * row_len, (g + 1) * row_len)
                for h in range(NH):
                    cols = slice(h * HD, (h + 1) * HD)
                    x = p[rows, cols]
                    acc = x * cw_ref[half:half + 1, cols]
                    for tap in range(CONV_K):
                        off = tap - half
                        if off != 0:
                            acc = acc + shifted(x, off) * cw_ref[tap:tap + 1, cols]
                    y = _silu(acc)
                    ss = jnp.sum(y * y, axis=-1, keepdims=True)
                    nrm = jnp.where(c < 3, lax.rsqrt(ss + EPS) * qscale, 1.0)
                    o_ref[0, rb * INPROJ_SUB + g * row_len:rb * INPROJ_SUB + (g + 1) * row_len, cols] = (
                        y * nrm).astype(o_ref.dtype)


def _inproj_call(x, g, sh, sc, w_main, w_ab, conv_w8, tm, col0, ncol, row_len):
    b, t, _ = x.shape
    tn = 1024
    assert tm % INPROJ_SUB == 0 and INPROJ_SUB % row_len == 0
    return pl.pallas_call(
        functools.partial(_inproj_kernel, col0=col0, row_len=row_len),
        grid=(b, t // tm, ncol),
        in_specs=[pl.BlockSpec((1, tm, D), lambda bi, i, j: (bi, i, 0)),
                  pl.BlockSpec((1, D), lambda bi, i, j: (0, 0)),
                  pl.BlockSpec((1, 1, D), lambda bi, i, j: (bi, 0, 0)),
                  pl.BlockSpec((1, 1, D), lambda bi, i, j: (bi, 0, 0)),
                  pl.BlockSpec((D, tn), lambda bi, i, j: (0, j + col0)),
                  pl.BlockSpec((D, LANES), lambda bi, i, j: (0, 0)),
                  pl.BlockSpec((8, GW), lambda bi, i, j: (0, jnp.clip(j + col0 - 1, 0, 2)))],
        out_specs=[pl.BlockSpec((1, tm, tn), lambda bi, i, j: (bi, i, j)),
                   pl.BlockSpec((1, tm, LANES), lambda bi, i, j: (bi, i, 0))],
        out_shape=[jax.ShapeDtypeStruct((b, t, ncol * tn), BF16),
                   jax.ShapeDtypeStruct((b, t, LANES), F32)],
        scratch_shapes=[pltpu.VMEM((tm, D), BF16)],
        compiler_params=_cparams(("parallel", "parallel", "arbitrary")),
        name="inproj",
    )(x, g, sh, sc, w_main, w_ab, conv_w8)


def _gdn_kernel(qf_ref, kf_ref, vf_ref, abf_ref, qb_ref, kb_ref, vb_ref, abb_ref, alog_ref, dtb_ref, s0_ref,
                of_ref, ob_ref, sout_ref, s_scr):
    n = pl.program_id(1)

    @pl.when(n == 0)
    def _():
        s_scr[...] = s0_ref[0]

    row = lax.broadcasted_iota(I32, (GC, GC), 0)
    col = lax.broadcasted_iota(I32, (GC, GC), 1)
    eye = (row == col).astype(F32)
    m_parts, rhs_parts, qd_parts, kdt_parts, qk_parts, gt_parts = [], [], [], [], [], []
    for d in range(N_DIR):
        q_ref, k_ref, v_ref, ab_ref = ((qf_ref, kf_ref, vf_ref, abf_ref) if d == 0
                                       else (qb_ref, kb_ref, vb_ref, abb_ref))
        incl = (col <= row) if d == 0 else (col >= row)
        strict = (col < row) if d == 0 else (col > row)
        ab = ab_ref[0]
        g_all = -jnp.exp(alog_ref[...]) * _softplus(ab + dtb_ref[...])
        beta_all = _sigmoid(ab)
        gc_all = jnp.dot(incl.astype(F32), g_all, preferred_element_type=F32,
                         precision=lax.Precision.HIGHEST)
        gc_t = gc_all.T
        last = GC - 1 if d == 0 else 0
        c0 = d * NH
        cb = N_DIR * NH + d * NH
        heads = range(NH)
        gcol = jnp.stack([gc_all[:, c0 + h:c0 + h + 1] for h in heads])
        grow = jnp.stack([gc_t[c0 + h:c0 + h + 1, :] for h in heads])
        glast = jnp.stack([gc_all[last:last + 1, c0 + h:c0 + h + 1] for h in heads])
        beta = jnp.stack([beta_all[:, cb + h:cb + h + 1] for h in heads])
        q = jnp.stack([q_ref[0, :, h * HD:(h + 1) * HD] for h in heads]).astype(F32)
        k = jnp.stack([k_ref[0, :, h * HD:(h + 1) * HD] for h in heads])
        kf = k.astype(F32)
        v = jnp.stack([v_ref[0, :, h * HD:(h + 1) * HD] for h in heads]).astype(F32)
        decay = jnp.where(incl[None], jnp.exp(jnp.where(incl[None], gcol - grow, 0.0)), 0.0)
        egc = jnp.exp(gcol)
        kbeta = kf * beta
        a2 = _bmm_nt(jnp.concatenate([kbeta, q], axis=1).astype(BF16), k)
        m_parts.append(jnp.where(strict[None], a2[:, :GC] * decay, 0.0))
        rhs_parts.append(jnp.concatenate([v * beta, kbeta * egc], axis=2).astype(BF16))
        qd_parts.append((q * egc).astype(BF16))
        kdt_parts.append(jnp.swapaxes(kf * jnp.exp(glast - gcol), 1, 2).astype(BF16))
        qk_parts.append((a2[:, GC:] * decay).astype(BF16))
        gt_parts.append(jnp.exp(glast))

    cat = lambda parts: jnp.concatenate(parts, axis=0)
    m, rhs, qd, kdt, qk, gtot = (cat(t) for t in (m_parts, rhs_parts, qd_parts, kdt_parts, qk_parts, gt_parts))
    blk = lambda s: ((row >> int(math.log2(s))) == (col >> int(math.log2(s))))[None]
    mb = jnp.where(blk(8), m, 0.0)
    p = eye[None] - mb
    m2 = _bmm(mb, mb)
    p = p + _bmm(p, m2)
    p = p + _bmm(p, _bmm(m2, m2))
    s = 8
    while s < GC:
        cpart = jnp.where(jnp.logical_and(blk(2 * s), jnp.logical_not(blk(s))), m, 0.0)
        p = p - _bmm(p, _bmm(cpart, p))
        s *= 2
    uw = _bmm(p, rhs)

    st = s_scr[...]
    r = _bmm(jnp.concatenate([uw[:, :, HD:].astype(BF16), qd], axis=1), st)
    vb = (uw[:, :, :HD] - r[:, :GC]).astype(BF16)
    o = r[:, GC:] + _bmm(qk, vb)
    s_scr[...] = st * gtot + _bmm(kdt, vb)
    for un in range(NU):
        h = un % NH
        if un < NH:
            of_ref[0, :, h * HD:(h + 1) * HD] = o[un].astype(of_ref.dtype)
        else:
            ob_ref[0, :, h * HD:(h + 1) * HD] = o[un].astype(ob_ref.dtype)

    @pl.when(n == pl.num_programs(1) - 1)
    def _():
        sout_ref[0] = s_scr[...]


def _gdn_call(qkv, col0, ab, alog_row, dtb_row, s0):
    b, t, _ = qkv.shape
    nc = t // GC
    blk = lambda c, rev: pl.BlockSpec((1, GC, GW), (lambda bi, n: (bi, nc - 1 - n, c + col0)) if rev
                                      else (lambda bi, n: (bi, n, c + col0)))
    abblk = lambda rev: pl.BlockSpec((1, GC, LANES), (lambda bi, n: (bi, nc - 1 - n, 0)) if rev
                                     else (lambda bi, n: (bi, n, 0)))
    sblk = pl.BlockSpec((1, NU, HD, HD), lambda bi, n: (bi, 0, 0, 0))
    return pl.pallas_call(
        _gdn_kernel,
        grid=(b, nc),
        in_specs=[blk(0, False), blk(1, False), blk(2, False), abblk(False),
                  blk(0, True), blk(1, True), blk(2, True), abblk(True),
                  pl.BlockSpec((1, LANES), lambda bi, n: (0, 0)),
                  pl.BlockSpec((1, LANES), lambda bi, n: (0, 0)),
                  sblk],
        out_specs=[pl.BlockSpec((1, GC, GW), lambda bi, n: (bi, n, 0)),
                   pl.BlockSpec((1, GC, GW), lambda bi, n: (bi, nc - 1 - n, 0)),
                   sblk],
        out_shape=[jax.ShapeDtypeStruct((b, t, GW), BF16),
                   jax.ShapeDtypeStruct((b, t, GW), BF16),
                   jax.ShapeDtypeStruct((b, NU, HD, HD), F32)],
        scratch_shapes=[pltpu.VMEM((NU, HD, HD), F32)],
        compiler_params=_cparams(("parallel", "arbitrary")),
        name="gdn",
    )(qkv, qkv, qkv, ab, qkv, qkv, qkv, ab, alog_row, dtb_row, s0)


def _dft_tables():
    r = GRID_W
    a = 2.0 * np.pi * np.outer(np.arange(r), np.arange(r)) / r
    c64, s64 = np.cos(a), np.sin(a)
    ac = 2.0 * np.pi * np.outer(np.arange(FGD), np.arange(FGD)) / FGD
    cc, sc = np.cos(ac), np.sin(ac)
    at = (2.0 * np.pi * np.outer(np.arange(r), np.arange(r)) / (r * r)).reshape(r * r, 1)
    chan = np.concatenate([cc, -sc], axis=1)
    row_cs = np.concatenate([c64, s64], axis=0)
    col_re = np.concatenate([c64, s64], axis=1)
    as_bf16 = lambda t: jnp.asarray(t, F32).astype(BF16)
    return (as_bf16(chan), as_bf16(row_cs), as_bf16(col_re),
            jnp.asarray(np.cos(at), F32), jnp.asarray(np.sin(at), F32))


def _fourier_kernel(x_ref, chan_ref, rowcs_ref, colre_ref, twc_ref, tws_ref, o_ref, u_scr, v_scr, y_scr):
    r = GRID_W

    def swap(t):
        return jnp.swapaxes(t.reshape(r, r, FGD), 0, 1).reshape(r * r, FGD)

    u = _dot(x_ref[0], chan_ref[...])
    u_scr[0] = swap(u[:, :FGD])
    u_scr[1] = swap(u[:, FGD:])

    def over_n2(n1, carry):
        rows = pl.ds(pl.multiple_of(n1 * r, r), r)
        x = jnp.concatenate([u_scr[0, rows, :], u_scr[1, rows, :]], axis=1).astype(BF16)
        cs = _dot(rowcs_ref[...], x)
        vr = cs[:r, :FGD] + cs[r:, FGD:]
        vi = cs[:r, FGD:] - cs[r:, :FGD]
        tc = twc_ref[rows, :]
        ts = tws_ref[rows, :]
        v_scr[0, rows, :] = vr * tc + vi * ts
        v_scr[1, rows, :] = vi * tc - vr * ts
        return carry

    lax.fori_loop(0, r, over_n2, 0, unroll=16)
    u_scr[0] = swap(v_scr[0])
    u_scr[1] = swap(v_scr[1])

    def over_n1(k2, carry):
        rows = pl.ds(pl.multiple_of(k2 * r, r), r)
        st = jnp.concatenate([u_scr[0, rows, :], u_scr[1, rows, :]], axis=0).astype(BF16)
        y_scr[rows, :] = _dot(colre_ref[...], st) * (1.0 / math.sqrt(SEQ_N * FGD))
        return carry

    lax.fori_loop(0, r, over_n1, 0, unroll=16)
    o_ref[0] = swap(y_scr[...]).astype(o_ref.dtype)


def _fourier_call(p_main):
    b, n, _ = p_main.shape
    r = GRID_W
    chan, rowcs, colre, twc, tws = _dft_tables()
    twc = jnp.broadcast_to(twc, (n, LANES))
    tws = jnp.broadcast_to(tws, (n, LANES))
    const = lambda shape: pl.BlockSpec(shape, lambda bi, g: (0, 0))
    return pl.pallas_call(
        _fourier_kernel,
        grid=(b, FG),
        in_specs=[pl.BlockSpec((1, n, FGD), lambda bi, g: (bi, 0, g)),
                  const((FGD, 2 * FGD)), const((2 * r, r)), const((r, 2 * r)),
                  const((n, LANES)), const((n, LANES))],
        out_specs=pl.BlockSpec((1, n, FGD), lambda bi, g: (bi, 0, g)),
        out_shape=jax.ShapeDtypeStruct((b, n, FW), BF16),
        scratch_shapes=[pltpu.VMEM((2, n, FGD), F32), pltpu.VMEM((2, n, FGD), F32), pltpu.VMEM((n, FGD), F32)],
        compiler_params=_cparams(("parallel", "parallel")),
        name="fourier",
    )(p_main, chan, rowcs, colre, twc, tws)


OUTPROJ_TM = 512
OUTPROJ_SUB = 256


def _outproj_kernel(yf_ref, of_ref, ob_ref, z_ref, x_ref, wout_ref, gnw_ref, gt1_ref, g2_ref, sh2_ref, sc2_ref,
                    wr_ref, x1_ref, hx_ref, lg_ref):
    sub = OUTPROJ_SUB
    for rb in range(x_ref.shape[1] // sub):
        rows = slice(rb * sub, (rb + 1) * sub)
        parts = [yf_ref[0, rows, :]]
        for h in range(NH):
            cols = slice(h * HD, (h + 1) * HD)
            oh = of_ref[0, rows, cols].astype(F32) + ob_ref[0, rows, cols].astype(F32)
            ms = jnp.mean(oh * oh, axis=-1, keepdims=True)
            on = oh * lax.rsqrt(ms + EPS) * gnw_ref[...]
            parts.append((on * _silu(z_ref[0, rows, cols].astype(F32))).astype(BF16))
        mix = jnp.concatenate(parts, axis=1)
        x1 = x_ref[0, rows, :] + gt1_ref[0] * _dot(mix, wout_ref[...])
        x1_ref[0, rows, :] = x1
        ms = jnp.mean(x1 * x1, axis=-1, keepdims=True)
        hx = x1 * lax.rsqrt(ms + EPS) * g2_ref[...] * (1.0 + sc2_ref[0]) + sh2_ref[0]
        hx_ref[0, rows, :] = hx
        hx_hi = hx.astype(BF16)
        hx_lo = (hx - hx_hi.astype(F32)).astype(BF16)
        l2 = _dot(hx_hi, wr_ref[...])
        lg = l2[:, :LANES] + l2[:, LANES:] + _dot(hx_lo, wr_ref[:, :LANES])
        lg_ref[0, :, rows] = lg.T[:NE, :]


def _outproj_call(yf, o_f, o_b, p_main, x, w_out, gnw, gt1, g2, sh2, sc2, w_r):
    b, t, _ = x.shape
    tm = OUTPROJ_TM
    zcol = p_main.shape[2] // GW - 1
    row = lambda c: pl.BlockSpec((1, tm, c), lambda bi, i: (bi, i, 0))
    vec = pl.BlockSpec((1, 1, D), lambda bi, i: (bi, 0, 0))
    once = pl.Buffered(1)
    return pl.pallas_call(
        _outproj_kernel,
        grid=(b, t // tm),
        in_specs=[row(FW), row(GW), row(GW),
                  pl.BlockSpec((1, tm, GW), lambda bi, i: (bi, i, zcol)),
                  row(D),
                  pl.BlockSpec((D, D), lambda bi, i: (0, 0), pipeline_mode=once),
                  pl.BlockSpec((1, HD), lambda bi, i: (0, 0)),
                  vec,
                  pl.BlockSpec((1, D), lambda bi, i: (0, 0)),
                  vec, vec,
                  pl.BlockSpec((D, 2 * LANES), lambda bi, i: (0, 0), pipeline_mode=once)],
        out_specs=[row(D), row(D), pl.BlockSpec((1, NE, tm), lambda bi, i: (bi, 0, i))],
        out_shape=[jax.ShapeDtypeStruct((b, t, D), F32),
                   jax.ShapeDtypeStruct((b, t, D), F32),
                   jax.ShapeDtypeStruct((b, NE, t), F32)],
        compiler_params=_cparams(("parallel", "parallel")),
        name="outproj",
    )(yf, o_f, o_b, p_main, x, w_out, gnw, gt1, g2, sh2, sc2, w_r)


def _lane_cumsum(x):
    n = x.shape[-1]
    lane = lax.broadcasted_iota(I32, x.shape, x.ndim - 1)
    sh = 1
    while sh < n:
        x = x + jnp.where(lane >= sh, pltpu.roll(x, shift=sh, axis=x.ndim - 1), 0.0)
        sh *= 2
    return x


def _topk_kernel(lg_ref, idx_ref, val_ref, tab_scr, rend_scr, *, cap, nt):
    nr = NE * nt
    lg = lg_ref[0]
    lg3 = lg.reshape(NE, nt, LANES)
    ex3 = jnp.exp(lg3 - jnp.max(lg3, axis=0, keepdims=True))
    aff = (ex3 / jnp.sum(ex3, axis=0, keepdims=True)).reshape(nr, LANES)

    rr = lax.broadcasted_iota(I32, (nr, nr), 0)
    rc = lax.broadcasted_iota(I32, (nr, nr), 1)
    ntb = int(math.log2(nt))
    same = (rr >> ntb) == (rc >> ntb)
    grp = same.astype(BF16)
    before = jnp.logical_and(same, rc < rr).astype(BF16)

    def per_expert_sum(mat, mask):
        return jnp.sum(_dot(mat, mask.astype(BF16)), axis=1, keepdims=True)

    def search(i, cur):
        cand = cur | (1 << (30 - i))
        cnt = per_expert_sum(grp, (aff >= pltpu.bitcast(cand, F32)).astype(F32))
        return jnp.where(cnt >= cap, cand, cur)

    thr = pltpu.bitcast(lax.fori_loop(0, 31, search, jnp.zeros((nr, 1), I32)), F32)
    gt = aff > thr
    eq = (aff == thr).astype(F32)

    def rank_of(m):
        return _lane_cumsum(m) + per_expert_sum(before, m)

    n_gt = per_expert_sum(grp, gt.astype(F32))
    sel = jnp.logical_or(gt, jnp.logical_and(eq > 0.0, rank_of(eq) <= cap - n_gt))
    selm = sel.astype(F32)
    rsel = jnp.where(sel, rank_of(selm), 0.0)
    rprev = per_expert_sum(before, selm)
    rend_scr[...] = jnp.concatenate([jnp.broadcast_to(rprev, (nr, LANES)),
                                     jnp.broadcast_to(rprev + jnp.sum(selm, axis=1, keepdims=True), (nr, LANES))],
                                    axis=1)
    hi = jnp.floor(rsel * (1.0 / 32.0))
    a1 = aff.astype(BF16)
    a2 = (aff - a1.astype(F32)).astype(BF16)
    a3 = (aff - a1.astype(F32) - a2.astype(F32)).astype(BF16)
    tab_scr[...] = jnp.concatenate([hi.astype(BF16), (rsel - 32.0 * hi).astype(BF16), a1, a2, a3], axis=1)
    idx_ref[0] = jnp.zeros((cap, LANES), I32)
    val_ref[0] = jnp.zeros((cap, LANES), F32)
    slot = (lax.broadcasted_iota(I32, (cap, 1), 0) + 1).astype(F32)
    lane = lax.broadcasted_iota(I32, (cap, LANES), 1)
    lanef = lane.astype(F32)
    pad = jnp.full((LANES - nt, 2 * LANES), 2.0 * cap, F32)

    def per_expert(e, carry):
        r0 = pl.multiple_of(e * nt, nt)
        rt = jnp.concatenate([rend_scr[pl.ds(r0, nt), :], pad], axis=0)
        prev_l = rt[:, :LANES].T[0:1, :]
        end_l = rt[:, LANES:].T[0:1, :]
        inrow = jnp.logical_and(prev_l < slot, slot <= end_l)
        tab = jnp.concatenate([tab_scr[pl.ds(r0, nt), :], jnp.zeros((LANES - nt, 5 * LANES), BF16)], axis=0)
        g = _dot(inrow.astype(BF16), tab)
        hit = (32.0 * g[:, :LANES] + g[:, LANES:2 * LANES]) == slot
        ag = g[:, 2 * LANES:3 * LANES] + g[:, 3 * LANES:4 * LANES] + g[:, 4 * LANES:]
        pos = (jnp.sum(jnp.where(inrow, lanef, 0.0), axis=1, keepdims=True) * float(LANES)
               + jnp.sum(jnp.where(hit, lanef, 0.0), axis=1, keepdims=True))
        val = jnp.sum(jnp.where(hit, ag, 0.0), axis=1, keepdims=True)
        idx_ref[0] = jnp.where(lane == e, pos.astype(I32), idx_ref[0])
        val_ref[0] = jnp.where(lane == e, val, val_ref[0])
        return carry

    lax.fori_loop(0, NE, per_expert, 0)


def _topk_call(lg_t, cap):
    b, _, t = lg_t.shape
    nt = t // LANES
    nr = NE * nt
    return pl.pallas_call(
        functools.partial(_topk_kernel, cap=cap, nt=nt),
        grid=(b,),
        in_specs=[pl.BlockSpec((1, nr, LANES), lambda bi: (bi, 0, 0))],
        out_specs=[pl.BlockSpec((1, cap, LANES), lambda bi: (bi, 0, 0)),
                   pl.BlockSpec((1, cap, LANES), lambda bi: (bi, 0, 0))],
        out_shape=[jax.ShapeDtypeStruct((b, cap, LANES), I32),
                   jax.ShapeDtypeStruct((b, cap, LANES), F32)],
        scratch_shapes=[pltpu.VMEM((nr, 5 * LANES), BF16), pltpu.VMEM((nr, 2 * LANES), F32)],
        compiler_params=_cparams(("parallel",)),
        name="topk",
    )(lg_t.reshape(b, nr, LANES))


MOE_TF = 256
MOE_TN = 256
MOE_MC = 512
MOE_NF = EFF // MOE_TF
MOE_NN = D // MOE_TN
MOE_GC = 352
MOE_GPAD = 32


def _moe_ffn_kernel(idx_ref, hx_hbm, val_ref, gt2_ref, wg_ref, wu_ref, wd_ref, y_ref,
                    xg_scr, hid_scr, stg_scr, vcol_scr, gsem, *, rows):
    e = pl.program_id(0)
    s = pl.program_id(1)
    cur = e % 2
    nxt = jnp.minimum(e + 1, pl.num_programs(0) - 1)

    def issue(expert, c):
        base = expert * rows + c * MOE_GC
        for r in range(MOE_GC):
            pltpu.make_async_copy(hx_hbm.at[pl.ds(idx_ref[base + r], 1), :],
                                  stg_scr.at[pl.ds(r, 1), :], gsem).start(priority=r % 2)

    def land(c, buf):
        pltpu.make_async_copy(hx_hbm.at[pl.ds(0, MOE_GC), :], stg_scr, gsem).wait()
        r0 = pl.multiple_of(c * MOE_GC, MOE_GPAD)
        xg_scr[buf, pl.ds(r0, MOE_GC), :] = stg_scr[...].astype(BF16)

    @pl.when(jnp.logical_and(e == 0, s == 0))
    def _():
        for c in range(MOE_NF):
            issue(0, c)
            land(c, 0)

    @pl.when(jnp.logical_and(s >= 1, s <= MOE_NF))
    def _():
        land(s - 1, 1 - cur)

    @pl.when(s < MOE_NF)
    def _():
        xs = xg_scr[cur, 0:rows, :]
        hid = (_silu(_dot(xs, wg_ref[0].astype(BF16))) * _dot(xs, wu_ref[0].astype(BF16))).astype(BF16)
        hid_scr[:, pl.ds(pl.multiple_of(s * MOE_TF, MOE_TF), MOE_TF)] = hid
        issue(nxt, s)

    @pl.when(s == MOE_NF)
    def _():
        vcol_scr[...] = jnp.broadcast_to(val_ref[0], (LANES, rows)).T

    @pl.when(s >= MOE_NF)
    def _():
        y = _dot(hid_scr[...], wd_ref[0].astype(BF16))
        for mc in range(rows // MOE_MC):
            sl = slice(mc * MOE_MC, (mc + 1) * MOE_MC)
            y_ref[0, sl, :] = (y[sl] * vcol_scr[sl, 0:1] * gt2_ref[mc, 0]).astype(y_ref.dtype)


def _moe_ffn_call(idx_flat, hx2, vals, gt2, w_gate, w_up, w_down, cap):
    rows = vals.shape[2]
    rows_pad = MOE_NF * MOE_GC
    assert rows // MOE_MC == gt2.shape[0] and cap == MOE_MC and rows_pad >= rows and MOE_GC % MOE_GPAD == 0
    idx_flat = jnp.pad(idx_flat, (0, rows_pad - rows))
    gt2t = gt2.reshape(gt2.shape[0], MOE_NN, 1, MOE_TN)
    ph1 = lambda s: jnp.minimum(s, MOE_NF - 1)
    ph2 = lambda s: jnp.maximum(s - MOE_NF, 0)
    grid_spec = pltpu.PrefetchScalarGridSpec(
        num_scalar_prefetch=1,
        grid=(NE, MOE_NF + MOE_NN),
        in_specs=[pl.BlockSpec(memory_space=pl.ANY),
                  pl.BlockSpec((1, 1, rows), lambda e, s, idx: (e, 0, 0)),
                  pl.BlockSpec((gt2.shape[0], 1, 1, MOE_TN), lambda e, s, idx: (0, ph2(s), 0, 0)),
                  pl.BlockSpec((1, D, MOE_TF), lambda e, s, idx: (e, 0, ph1(s))),
                  pl.BlockSpec((1, D, MOE_TF), lambda e, s, idx: (e, 0, ph1(s))),
                  pl.BlockSpec((1, EFF, MOE_TN), lambda e, s, idx: (e, 0, ph2(s)))],
        out_specs=pl.BlockSpec((1, rows, MOE_TN), lambda e, s, idx: (e, 0, ph2(s))),
        scratch_shapes=[pltpu.VMEM((2, rows_pad, D), BF16),
                        pltpu.VMEM((rows, EFF), BF16),
                        pltpu.VMEM((MOE_GC, D), F32),
                        pltpu.VMEM((rows, LANES), F32),
                        pltpu.SemaphoreType.DMA(())],
    )
    return pl.pallas_call(
        functools.partial(_moe_ffn_kernel, rows=rows),
        grid_spec=grid_spec,
        out_shape=jax.ShapeDtypeStruct((NE, rows, D), BF16),
        compiler_params=_cparams(("arbitrary", "arbitrary")),
        name="moe_ffn",
    )(idx_flat, hx2, vals, gt2t, w_gate, w_up, w_down)


MOE_SC = 256
MOE_SS = 4


def _moe_scatter_kernel(idx_ref, y_ref, res_hbm, out_hbm, buf_scr, rsem, wsem, *, rows):
    del res_hbm
    e = pl.program_id(0)
    st = pl.program_id(1)
    nst = pl.num_programs(1)
    c0 = st * MOE_SS

    def reads(chunk, k):
        base = e * rows + chunk * MOE_SC
        for r in range(MOE_SC):
            pltpu.make_async_copy(out_hbm.at[pl.ds(idx_ref[base + r], 1), :],
                                  buf_scr.at[k, pl.ds(r, 1), :], rsem.at[k]).start(priority=r % 2)

    def writes(chunk, k):
        base = e * rows + chunk * MOE_SC
        for r in range(MOE_SC):
            pltpu.make_async_copy(buf_scr.at[k, pl.ds(r, 1), :],
                                  out_hbm.at[pl.ds(idx_ref[base + r], 1), :], wsem.at[k]).start(priority=r % 2)

    def wait_all(sem, k):
        pltpu.make_async_copy(out_hbm.at[pl.ds(0, MOE_SC), :], buf_scr.at[k], sem.at[k]).wait()

    @pl.when(st == 0)
    def _():
        reads(c0, 0)
        reads(c0 + 1, 1)

    for k in range(MOE_SS):
        wait_all(rsem, k)
        buf_scr[k] = buf_scr[k] + y_ref[0, k * MOE_SC:(k + 1) * MOE_SC, :].astype(F32)
        writes(c0 + k, k)
        if k + 2 < MOE_SS:
            @pl.when(st > 0)
            def _():
                wait_all(wsem, k + 2)
            reads(c0 + k + 2, k + 2)
        else:
            @pl.when(st + 1 < nst)
            def _():
                wait_all(wsem, k + 2 - MOE_SS)
                reads(c0 + k + 2, k + 2 - MOE_SS)

    @pl.when(st == nst - 1)
    def _():
        for k in range(MOE_SS):
            wait_all(wsem, k)


def _moe_scatter_call(idx_flat, y, x1):
    rows = y.shape[1]
    grid_spec = pltpu.PrefetchScalarGridSpec(
        num_scalar_prefetch=1,
        grid=(NE, rows // (MOE_SC * MOE_SS)),
        in_specs=[pl.BlockSpec((1, MOE_SC * MOE_SS, D), lambda e, st, idx: (e, st, 0)),
                  pl.BlockSpec(memory_space=pl.ANY)],
        out_specs=pl.BlockSpec(memory_space=pl.ANY),
        scratch_shapes=[pltpu.VMEM((MOE_SS, MOE_SC, D), F32),
                        pltpu.SemaphoreType.DMA((MOE_SS,)),
                        pltpu.SemaphoreType.DMA((MOE_SS,))],
    )
    return pl.pallas_call(
        functools.partial(_moe_scatter_kernel, rows=rows),
        grid_spec=grid_spec,
        out_shape=jax.ShapeDtypeStruct(x1.shape, F32),
        input_output_aliases={2: 0},
        compiler_params=_cparams(("arbitrary", "arbitrary")),
        name="moe_scatter",
    )(idx_flat, y, x1)


def _final_kernel(x_ref, g_ref, o_ref):
    x = x_ref[...]
    ms = jnp.mean(x * x, axis=-1, keepdims=True)
    o_ref[...] = x * lax.rsqrt(ms + EPS) * g_ref[...]


def _final_call(x2d, g):
    m = x2d.shape[0]
    tm = 1024
    return pl.pallas_call(
        _final_kernel,
        grid=(m // tm,),
        in_specs=[pl.BlockSpec((tm, D), lambda i: (i, 0)), pl.BlockSpec((1, D), lambda i: (0, 0))],
        out_specs=pl.BlockSpec((tm, D), lambda i: (i, 0)),
        out_shape=jax.ShapeDtypeStruct(x2d.shape, F32),
        compiler_params=_cparams(("parallel",)),
        name="final_norm",
    )(x2d, g)


def kernel(x, c, ctx, c_ctx, w_mod, b_mod, norm1_g, norm2_g, w_in, conv_w, a_log, dt_bias, gdn_norm_w, w_out,
           w_router, w_gate, w_up, w_down, norm_f):
    b, n, _ = x.shape
    nctx = ctx.shape[1]
    cap = 2 * n // NE
    i = 0

    cc = jnp.concatenate([c, c_ctx[None, :], jnp.zeros((8 - b - 1, D), F32)], axis=0)
    mod = _mod_call(cc, w_mod[i], b_mod[i][None, :])
    sh1, sc1, gt1, sh2, sc2, gt2 = [mod[:b, k * D:(k + 1) * D][:, None, :] for k in range(6)]
    sh1c = mod[b:b + 1, 0:D][:, None, :]
    sc1c = mod[b:b + 1, D:2 * D][:, None, :]

    w_main, w_ab = _wprep_call(w_in, i)
    conv_w8 = jnp.pad(conv_w[i], ((0, 8 - CONV_K), (0, 0)))
    alog_row = jnp.pad(a_log[i].reshape(1, NU), ((0, 0), (0, LANES - NU)))
    dtb_row = jnp.pad(dt_bias[i].reshape(1, NU), ((0, 0), (0, LANES - NU)))
    g1 = norm1_g[i][None, :]

    p_ctx, ab_ctx = _inproj_call(ctx.reshape(1, b * nctx, D), g1, sh1c, sc1c, w_main, w_ab, conv_w8,
                                 b * nctx, 1, 3, nctx)
    _, _, s_ctx = _gdn_call(p_ctx.reshape(b, nctx, 3 * GW), 0, ab_ctx.reshape(b, nctx, LANES), alog_row, dtb_row,
                            jnp.zeros((b, NU, HD, HD), F32))

    p_main, ab = _inproj_call(x, g1, sh1, sc1, w_main, w_ab, conv_w8, 1024, 0, 5, GRID_W)
    o_f, o_b, _ = _gdn_call(p_main, 1, ab, alog_row, dtb_row, s_ctx)
    yf = _fourier_call(p_main)

    w_r = jnp.pad(w_router[i], ((0, 0), (0, LANES - NE)))
    w_r_hi = w_r.astype(BF16)
    w_r = jnp.concatenate([w_r_hi, (w_r - w_r_hi.astype(F32)).astype(BF16)], axis=1)
    x1, hx2, lg_t = _outproj_call(yf, o_f, o_b, p_main, x, w_out[i].astype(BF16), gdn_norm_w[i][None, :], gt1,
                                  norm2_g[i][None, :], sh2, sc2, w_r)

    idx_c, val_c = _topk_call(lg_t, cap)
    idx = jnp.transpose(idx_c[:, :, :NE], (2, 0, 1))
    val = jnp.transpose(val_c[:, :, :NE], (2, 0, 1))
    idx_flat = (idx + (jnp.arange(b, dtype=I32) * n)[None, :, None]).reshape(NE * b * cap)
    vals = val.reshape(NE, 1, b * cap)

    y = _moe_ffn_call(idx_flat, hx2.reshape(b * n, D), vals, gt2, w_gate[i], w_up[i], w_down[i], cap)
    out = _moe_scatter_call(idx_flat, y, x1.reshape(b * n, D))
    return _final_call(out, norm_f[None, :]).reshape(b, n, D)
```

```python
import functools
import math

import numpy as np
import jax
import jax.numpy as jnp
from jax import lax
from jax.experimental import pallas as pl
from jax.experimental.pallas import tpu as pltpu

F32 = jnp.float32
BF16 = jnp.bfloat16
I32 = jnp.int32

D = 2048
SEQ_N = 4096
GRID_W = 64
FW = 1024
FG = 8
FGD = 128
GW = 1024
NH = 8
HD = 128
CONV_K = 5
N_DIR = 2
NE = 16
EFF = 1536
EPS = 1e-6
GC = 128
NU = N_DIR * NH
LANES = 128
VMEM_LIMIT = 56 * 1024 * 1024


def _sigmoid(x):
    return 1.0 / (1.0 + jnp.exp(-x))


def _silu(x):
    return x * _sigmoid(x)


def _softplus(x):
    return jnp.maximum(x, 0.0) + jnp.log(1.0 + jnp.exp(-jnp.abs(x)))


def _dot(a, b):
    return jnp.dot(a, b, preferred_element_type=F32)


def _bmm(a, b):
    return jnp.einsum('uij,ujk->uik', a.astype(BF16), b.astype(BF16), preferred_element_type=F32)


def _bmm_nt(a, b):
    return jnp.einsum('uik,ujk->uij', a, b, preferred_element_type=F32)


def _cparams(sem, vmem=VMEM_LIMIT):
    return pltpu.CompilerParams(dimension_semantics=sem, vmem_limit_bytes=vmem)


def _mod_kernel(c_ref, w_ref, b_ref, o_ref):
    s = _silu(c_ref[...]).astype(BF16)
    o_ref[...] = _dot(s, w_ref[...].astype(BF16)) + b_ref[...]


def _mod_call(cc, w_mod, b_mod):
    tn = 1024
    n = w_mod.shape[1]
    return pl.pallas_call(
        _mod_kernel,
        grid=(n // tn,),
        in_specs=[pl.BlockSpec((8, D), lambda j: (0, 0)),
                  pl.BlockSpec((D, tn), lambda j: (0, j)),
                  pl.BlockSpec((1, tn), lambda j: (0, j))],
        out_specs=pl.BlockSpec((8, tn), lambda j: (0, j)),
        out_shape=jax.ShapeDtypeStruct((8, n), F32),
        compiler_params=_cparams(("parallel",)),
        name="mod",
    )(cc, w_mod, b_mod)


def _wprep_kernel(w_ref, wm_ref, wab_ref):
    nmain = wm_ref.shape[1]
    ngate = w_ref.shape[2] - nmain
    wm_ref[...] = w_ref[0, :, :nmain].astype(BF16)
    wab_ref[...] = jnp.concatenate([w_ref[0, :, nmain:], jnp.zeros((w_ref.shape[1], LANES - ngate), F32)],
                                   axis=1).astype(BF16)


def _wprep_call(w_in, layer):
    _, d, c = w_in.shape
    nmain = FW + 4 * GW
    tr = 256
    return pl.pallas_call(
        _wprep_kernel,
        grid=(d // tr,),
        in_specs=[pl.BlockSpec((1, tr, c), lambda r: (layer, r, 0))],
        out_specs=[pl.BlockSpec((tr, nmain), lambda r: (r, 0)), pl.BlockSpec((tr, LANES), lambda r: (r, 0))],
        out_shape=[jax.ShapeDtypeStruct((d, nmain), BF16), jax.ShapeDtypeStruct((d, LANES), BF16)],
        compiler_params=_cparams(("parallel",)),
        name="wprep",
    )(w_in)


INPROJ_SUB = 256


def _inproj_kernel(x_ref, g_ref, sh_ref, sc_ref, w_ref, wab_ref, cw_ref, o_ref, ab_ref, h_scr, *, col0, row_len):
    c = pl.program_id(2) + col0

    @pl.when(pl.program_id(2) == 0)
    def _():
        x = x_ref[0]
        ms = jnp.mean(x * x, axis=-1, keepdims=True)
        y = x * lax.rsqrt(ms + EPS) * g_ref[...]
        h = (y * (1.0 + sc_ref[0]) + sh_ref[0]).astype(BF16)
        h_scr[...] = h
        ab_ref[0] = _dot(h, wab_ref[...])

    is_qkv = jnp.logical_and(c >= 1, c <= 3)

    @pl.when(jnp.logical_not(is_qkv))
    def _():
        o_ref[0] = _dot(h_scr[...], w_ref[...]).astype(o_ref.dtype)

    @pl.when(is_qkv)
    def _():
        half = CONV_K // 2
        qscale = jnp.where(c == 1, HD ** -0.5, 1.0).astype(F32)

        def shifted(x, off):
            zero = jnp.zeros((abs(off), HD), F32)
            return (jnp.concatenate([x[off:], zero], axis=0) if off > 0
                    else jnp.concatenate([zero, x[:off]], axis=0))

        for rb in range(x_ref.shape[1] // INPROJ_SUB):
            p = _dot(h_scr[rb * INPROJ_SUB:(rb + 1) * INPROJ_SUB, :], w_ref[...])
            for g in range(INPROJ_SUB // row_len):
                rows = slice(g * row_len, (g + 1) * row_len)
                for h in range(NH):
                    cols = slice(h * HD, (h + 1) * HD)
                    x = p[rows, cols]
                    acc = x * cw_ref[half:half + 1, cols]
                    for tap in range(CONV_K):
                        off = tap - half
                        if off != 0:
                            acc = acc + shifted(x, off) * cw_ref[tap:tap + 1, cols]
                    y = _silu(acc)
                    ss = jnp.sum(y * y, axis=-1, keepdims=True)
                    nrm = jnp.where(c < 3, lax.rsqrt(ss + EPS) * qscale, 1.0)
                    o_ref[0, rb * INPROJ_SUB + g * row_len:rb * INPROJ_SUB + (g + 1) * row_len, cols] = (
                        y * nrm).astype(o_ref.dtype)


def _inproj_call(x, g, sh, sc, w_main, w_ab, conv_w8, tm, col0, ncol, row_len):
    b, t, _ = x.shape
    tn = 1024
    assert tm % INPROJ_SUB == 0 and INPROJ_SUB % row_len == 0
    return pl.pallas_call(
        functools.partial(_inproj_kernel, col0=col0, row_len=row_len),
        grid=(b, t // tm, ncol),
        in_specs=[pl.BlockSpec((1, tm, D), lambda bi, i, j: (bi, i, 0)),
                  pl.BlockSpec((1, D), lambda bi, i, j: (0, 0)),
                  pl.BlockSpec((1, 1, D), lambda bi, i, j: (bi, 0, 0)),
                  pl.BlockSpec((1, 1, D), lambda bi, i, j: (bi, 0, 0)),
                  pl.BlockSpec((D, tn), lambda bi, i, j: (0, j + col0)),
                  pl.BlockSpec((D, LANES), lambda bi, i, j: (0, 0)),
                  pl.BlockSpec((8, GW), lambda bi, i, j: (0, jnp.clip(j + col0 - 1, 0, 2)))],
        out_specs=[pl.BlockSpec((1, tm, tn), lambda bi, i, j: (bi, i, j)),
                   pl.BlockSpec((1, tm, LANES), lambda bi, i, j: (bi, i, 0))],
        out_shape=[jax.ShapeDtypeStruct((b, t, ncol * tn), BF16),
                   jax.ShapeDtypeStruct((b, t, LANES), F32)],
        scratch_shapes=[pltpu.VMEM((tm, D), BF16)],
        compiler_params=_cparams(("parallel", "parallel", "arbitrary")),
        name="inproj",
    )(x, g, sh, sc, w_main, w_ab, conv_w8)


def _gdn_kernel(qf_ref, kf_ref, vf_ref, abf_ref, qb_ref, kb_ref, vb_ref, abb_ref, alog_ref, dtb_ref, s0_ref,
                of_ref, ob_ref, sout_ref, s_scr):
    n = pl.program_id(1)

    @pl.when(n == 0)
    def _():
        s_scr[...] = s0_ref[0]

    row = lax.broadcasted_iota(I32, (GC, GC), 0)
    col = lax.broadcasted_iota(I32, (GC, GC), 1)
    eye = (row == col).astype(F32)
    m_parts, rhs_parts, qd_parts, kdt_parts, qk_parts, gt_parts = [], [], [], [], [], []
    for d in range(N_DIR):
        q_ref, k_ref, v_ref, ab_ref = ((qf_ref, kf_ref, vf_ref, abf_ref) if d == 0
                                       else (qb_ref, kb_ref, vb_ref, abb_ref))
        incl = (col <= row) if d == 0 else (col >= row)
        strict = (col < row) if d == 0 else (col > row)
        ab = ab_ref[0]
        g_all = -jnp.exp(alog_ref[...]) * _softplus(ab + dtb_ref[...])
        beta_all = _sigmoid(ab)
        gc_all = jnp.dot(incl.astype(F32), g_all, preferred_element_type=F32,
                         precision=lax.Precision.HIGHEST)
        gc_t = gc_all.T
        last = GC - 1 if d == 0 else 0
        c0 = d * NH
        cb = N_DIR * NH + d * NH
        heads = range(NH)
        gcol = jnp.stack([gc_all[:, c0 + h:c0 + h + 1] for h in heads])
        grow = jnp.stack([gc_t[c0 + h:c0 + h + 1, :] for h in heads])
        glast = jnp.stack([gc_all[last:last + 1, c0 + h:c0 + h + 1] for h in heads])
        beta = jnp.stack([beta_all[:, cb + h:cb + h + 1] for h in heads])
        q = jnp.stack([q_ref[0, :, h * HD:(h + 1) * HD] for h in heads]).astype(F32)
        k = jnp.stack([k_ref[0, :, h * HD:(h + 1) * HD] for h in heads])
        kf = k.astype(F32)
        v = jnp.stack([v_ref[0, :, h * HD:(h + 1) * HD] for h in heads]).astype(F32)
        decay = jnp.where(incl[None], jnp.exp(jnp.where(incl[None], gcol - grow, 0.0)), 0.0)
        egc = jnp.exp(gcol)
        kbeta = kf * beta
        a2 = _bmm_nt(jnp.concatenate([kbeta, q], axis=1).astype(BF16), k)
        m_parts.append(jnp.where(strict[None], a2[:, :GC] * decay, 0.0))
        rhs_parts.append(jnp.concatenate([v * beta, kbeta * egc], axis=2).astype(BF16))
        qd_parts.append((q * egc).astype(BF16))
        kdt_parts.append(jnp.swapaxes(kf * jnp.exp(glast - gcol), 1, 2).astype(BF16))
        qk_parts.append((a2[:, GC:] * decay).astype(BF16))
        gt_parts.append(jnp.exp(glast))

    cat = lambda parts: jnp.concatenate(parts, axis=0)
    m, rhs, qd, kdt, qk, gtot = (cat(t) for t in (m_parts, rhs_parts, qd_parts, kdt_parts, qk_parts, gt_parts))
    blk = lambda s: ((row >> int(math.log2(s))) == (col >> int(math.log2(s))))[None]
    mb = jnp.where(blk(8), m, 0.0)
    p = eye[None] - mb
    m2 = _bmm(mb, mb)
    p = p + _bmm(p, m2)
    p = p + _bmm(p, _bmm(m2, m2))
    s = 8
    while s < GC:
        cpart = jnp.where(jnp.logical_and(blk(2 * s), jnp.logical_not(blk(s))), m, 0.0)
        p = p - _bmm(p, _bmm(cpart, p))
        s *= 2
    uw = _bmm(p, rhs)

    st = s_scr[...]
    r = _bmm(jnp.concatenate([uw[:, :, HD:].astype(BF16), qd], axis=1), st)
    vb = (uw[:, :, :HD] - r[:, :GC]).astype(BF16)
    o = r[:, GC:] + _bmm(qk, vb)
    s_scr[...] = st * gtot + _bmm(kdt, vb)
    for un in range(NU):
        h = un % NH
        if un < NH:
            of_ref[0, :, h * HD:(h + 1) * HD] = o[un].astype(of_ref.dtype)
        else:
            ob_ref[0, :, h * HD:(h + 1) * HD] = o[un].astype(ob_ref.dtype)

    @pl.when(n == pl.num_programs(1) - 1)
    def _():
        sout_ref[0] = s_scr[...]


def _gdn_call(qkv, col0, ab, alog_row, dtb_row, s0):
    b, t, _ = qkv.shape
    nc = t // GC
    blk = lambda c, rev: pl.BlockSpec((1, GC, GW), (lambda bi, n: (bi, nc - 1 - n, c + col0)) if rev
                                      else (lambda bi, n: (bi, n, c + col0)))
    abblk = lambda rev: pl.BlockSpec((1, GC, LANES), (lambda bi, n: (bi, nc - 1 - n, 0)) if rev
                                     else (lambda bi, n: (bi, n, 0)))
    sblk = pl.BlockSpec((1, NU, HD, HD), lambda bi, n: (bi, 0, 0, 0))
    return pl.pallas_call(
        _gdn_kernel,
        grid=(b, nc),
        in_specs=[blk(0, False), blk(1, False), blk(2, False), abblk(False),
                  blk(0, True), blk(1, True), blk(2, True), abblk(True),
                  pl.BlockSpec((1, LANES), lambda bi, n: (0, 0)),
                  pl.BlockSpec((1, LANES), lambda bi, n: (0, 0)),
                  sblk],
        out_specs=[pl.BlockSpec((1, GC, GW), lambda bi, n: (bi, n, 0)),
                   pl.BlockSpec((1, GC, GW), lambda bi, n: (bi, nc - 1 - n, 0)),
                   sblk],
        out_shape=[jax.ShapeDtypeStruct((b, t, GW), BF16),
                   jax.ShapeDtypeStruct((b, t, GW), BF16),
                   jax.ShapeDtypeStruct((b, NU, HD, HD), F32)],
        scratch_shapes=[pltpu.VMEM((NU, HD, HD), F32)],
        compiler_params=_cparams(("parallel", "arbitrary")),
        name="gdn",
    )(qkv, qkv, qkv, ab, qkv, qkv, qkv, ab, alog_row, dtb_row, s0)


def _dft_tables():
    r = GRID_W
    a = 2.0 * np.pi * np.outer(np.arange(r), np.arange(r)) / r
    c64, s64 = np.cos(a), np.sin(a)
    ac = 2.0 * np.pi * np.outer(np.arange(FGD), np.arange(FGD)) / FGD
    cc, sc = np.cos(ac), np.sin(ac)
    at = (2.0 * np.pi * np.outer(np.arange(r), np.arange(r)) / (r * r)).reshape(r * r, 1)
    chan = np.concatenate([cc, -sc], axis=1)
    row_cs = np.concatenate([c64, s64], axis=0)
    col_re = np.concatenate([c64, s64], axis=1)
    as_bf16 = lambda t: jnp.asarray(t, F32).astype(BF16)
    return (as_bf16(chan), as_bf16(row_cs), as_bf16(col_re),
            jnp.asarray(np.cos(at), F32), jnp.asarray(np.sin(at), F32))


def _fourier_kernel(x_ref, chan_ref, rowcs_ref, colre_ref, twc_ref, tws_ref, o_ref, u_scr, v_scr, y_scr):
    r = GRID_W

    def swap(t):
        return jnp.swapaxes(t.reshape(r, r, FGD), 0, 1).reshape(r * r, FGD)

    u = _dot(x_ref[0], chan_ref[...])
    u_scr[0] = swap(u[:, :FGD])
    u_scr[1] = swap(u[:, FGD:])

    def over_n2(n1, carry):
        rows = pl.ds(pl.multiple_of(n1 * r, r), r)
        x = jnp.concatenate([u_scr[0, rows, :], u_scr[1, rows, :]], axis=1).astype(BF16)
        cs = _dot(rowcs_ref[...], x)
        vr = cs[:r, :FGD] + cs[r:, FGD:]
        vi = cs[:r, FGD:] - cs[r:, :FGD]
        tc = twc_ref[rows, :]
        ts = tws_ref[rows, :]
        v_scr[0, rows, :] = vr * tc + vi * ts
        v_scr[1, rows, :] = vi * tc - vr * ts
        return carry

    lax.fori_loop(0, r, over_n2, 0, unroll=16)
    u_scr[0] = swap(v_scr[0])
    u_scr[1] = swap(v_scr[1])

    def over_n1(k2, carry):
        rows = pl.ds(pl.multiple_of(k2 * r, r), r)
        st = jnp.concatenate([u_scr[0, rows, :], u_scr[1, rows, :]], axis=0).astype(BF16)
        y_scr[rows, :] = _dot(colre_ref[...], st) * (1.0 / math.sqrt(SEQ_N * FGD))
        return carry

    lax.fori_loop(0, r, over_n1, 0, unroll=16)
    o_ref[0] = swap(y_scr[...]).astype(o_ref.dtype)


def _fourier_call(p_main):
    b, n, _ = p_main.shape
    r = GRID_W
    chan, rowcs, colre, twc, tws = _dft_tables()
    twc = jnp.broadcast_to(twc, (n, LANES))
    tws = jnp.broadcast_to(tws, (n, LANES))
    const = lambda shape: pl.BlockSpec(shape, lambda bi, g: (0, 0))
    return pl.pallas_call(
        _fourier_kernel,
        grid=(b, FG),
        in_specs=[pl.BlockSpec((1, n, FGD), lambda bi, g: (bi, 0, g)),
                  const((FGD, 2 * FGD)), const((2 * r, r)), const((r, 2 * r)),
                  const((n, LANES)), const((n, LANES))],
        out_specs=pl.BlockSpec((1, n, FGD), lambda bi, g: (bi, 0, g)),
        out_shape=jax.ShapeDtypeStruct((b, n, FW), BF16),
        scratch_shapes=[pltpu.VMEM((2, n, FGD), F32), pltpu.VMEM((2, n, FGD), F32), pltpu.VMEM((n, FGD), F32)],
        compiler_params=_cparams(("parallel", "parallel")),
        name="fourier",
    )(p_main, chan, rowcs, colre, twc, tws)


OUTPROJ_TM = 512
OUTPROJ_SUB = 256


def _outproj_kernel(yf_ref, of_ref, ob_ref, z_ref, x_ref, wout_ref, gnw_ref, gt1_ref, g2_ref, sh2_ref, sc2_ref,
                    wr_ref, x1_ref, hx_ref, lg_ref):
    sub = OUTPROJ_SUB
    for rb in range(x_ref.shape[1] // sub):
        rows = slice(rb * sub, (rb + 1) * sub)
        parts = [yf_ref[0, rows, :]]
        for h in range(NH):
            cols = slice(h * HD, (h + 1) * HD)
            oh = of_ref[0, rows, cols].astype(F32) + ob_ref[0, rows, cols].astype(F32)
            ms = jnp.mean(oh * oh, axis=-1, keepdims=True)
            on = oh * lax.rsqrt(ms + EPS) * gnw_ref[...]
            parts.append((on * _silu(z_ref[0, rows, cols].astype(F32))).astype(BF16))
        mix = jnp.concatenate(parts, axis=1)
        x1 = x_ref[0, rows, :] + gt1_ref[0] * _dot(mix, wout_ref[...])
        x1_ref[0, rows, :] = x1
        ms = jnp.mean(x1 * x1, axis=-1, keepdims=True)
        hx = x1 * lax.rsqrt(ms + EPS) * g2_ref[...] * (1.0 + sc2_ref[0]) + sh2_ref[0]
        hx_ref[0, rows, :] = hx
        hx_hi = hx.astype(BF16)
        hx_lo = (hx - hx_hi.astype(F32)).astype(BF16)
        l2 = _dot(hx_hi, wr_ref[...])
        lg = l2[:, :LANES] + l2[:, LANES:] + _dot(hx_lo, wr_ref[:, :LANES])
        lg_ref[0, :, rows] = lg.T[:NE, :]


def _outproj_call(yf, o_f, o_b, p_main, x, w_out, gnw, gt1, g2, sh2, sc2, w_r):
    b, t, _ = x.shape
    tm = OUTPROJ_TM
    zcol = p_main.shape[2] // GW - 1
    row = lambda c: pl.BlockSpec((1, tm, c), lambda bi, i: (bi, i, 0))
    vec = pl.BlockSpec((1, 1, D), lambda bi, i: (bi, 0, 0))
    once = pl.Buffered(1)
    return pl.pallas_call(
        _outproj_kernel,
        grid=(b, t // tm),
        in_specs=[row(FW), row(GW), row(GW),
                  pl.BlockSpec((1, tm, GW), lambda bi, i: (bi, i, zcol)),
                  row(D),
                  pl.BlockSpec((D, D), lambda bi, i: (0, 0), pipeline_mode=once),
                  pl.BlockSpec((1, HD), lambda bi, i: (0, 0)),
                  vec,
                  pl.BlockSpec((1, D), lambda bi, i: (0, 0)),
                  vec, vec,
                  pl.BlockSpec((D, 2 * LANES), lambda bi, i: (0, 0), pipeline_mode=once)],
        out_specs=[row(D), row(D), pl.BlockSpec((1, NE, tm), lambda bi, i: (bi, 0, i))],
        out_shape=[jax.ShapeDtypeStruct((b, t, D), F32),
                   jax.ShapeDtypeStruct((b, t, D), F32),
                   jax.ShapeDtypeStruct((b, NE, t), F32)],
        compiler_params=_cparams(("parallel", "parallel")),
        name="outproj",
    )(yf, o_f, o_b, p_main, x, w_out, gnw, gt1, g2, sh2, sc2, w_r)


def _lane_cumsum(x):
    n = x.shape[-1]
    lane = lax.broadcasted_iota(I32, x.shape, x.ndim - 1)
    sh = 1
    while sh < n:
        x = x + jnp.where(lane >= sh, pltpu.roll(x, shift=sh, axis=x.ndim - 1), 0.0)
        sh *= 2
    return x


def _topk_kernel(lg_ref, idx_ref, val_ref, tab_scr, rend_scr, *, cap, nt):
    nr = NE * nt
    lg = lg_ref[0]
    lg3 = lg.reshape(NE, nt, LANES)
    ex3 = jnp.exp(lg3 - jnp.max(lg3, axis=0, keepdims=True))
    aff = (ex3 / jnp.sum(ex3, axis=0, keepdims=True)).reshape(nr, LANES)

    rr = lax.broadcasted_iota(I32, (nr, nr), 0)
    rc = lax.broadcasted_iota(I32, (nr, nr), 1)
    ntb = int(math.log2(nt))
    same = (rr >> ntb) == (rc >> ntb)
    grp = same.astype(BF16)
    before = jnp.logical_and(same, rc < rr).astype(BF16)

    def per_expert_sum(mat, mask):
        return jnp.sum(_dot(mat, mask.astype(BF16)), axis=1, keepdims=True)

    def search(i, cur):
        cand = cur | (1 << (30 - i))
        cnt = per_expert_sum(grp, (aff >= pltpu.bitcast(cand, F32)).astype(F32))
        return jnp.where(cnt >= cap, cand, cur)

    thr = pltpu.bitcast(lax.fori_loop(0, 31, search, jnp.zeros((nr, 1), I32)), F32)
    gt = aff > thr
    eq = (aff == thr).astype(F32)

    def rank_of(m):
        return _lane_cumsum(m) + per_expert_sum(before, m)

    n_gt = per_expert_sum(grp, gt.astype(F32))
    sel = jnp.logical_or(gt, jnp.logical_and(eq > 0.0, rank_of(eq) <= cap - n_gt))
    selm = sel.astype(F32)
    rsel = jnp.where(sel, rank_of(selm), 0.0)
    rprev = per_expert_sum(before, selm)
    rend_scr[...] = jnp.concatenate([jnp.broadcast_to(rprev, (nr, LANES)),
                                     jnp.broadcast_to(rprev + jnp.sum(selm, axis=1, keepdims=True), (nr, LANES))],
                                    axis=1)
    hi = jnp.floor(rsel * (1.0 / 32.0))
    a1 = aff.astype(BF16)
    a2 = (aff - a1.astype(F32)).astype(BF16)
    a3 = (aff - a1.astype(F32) - a2.astype(F32)).astype(BF16)
    tab_scr[...] = jnp.concatenate([hi.astype(BF16), (rsel - 32.0 * hi).astype(BF16), a1, a2, a3], axis=1)
    idx_ref[0] = jnp.zeros((cap, LANES), I32)
    val_ref[0] = jnp.zeros((cap, LANES), F32)
    slot = (lax.broadcasted_iota(I32, (cap, 1), 0) + 1).astype(F32)
    lane = lax.broadcasted_iota(I32, (cap, LANES), 1)
    lanef = lane.astype(F32)
    pad = jnp.full((LANES - nt, 2 * LANES), 2.0 * cap, F32)

    def per_expert(e, carry):
        r0 = pl.multiple_of(e * nt, nt)
        rt = jnp.concatenate([rend_scr[pl.ds(r0, nt), :], pad], axis=0)
        prev_l = rt[:, :LANES].T[0:1, :]
        end_l = rt[:, LANES:].T[0:1, :]
        inrow = jnp.logical_and(prev_l < slot, slot <= end_l)
        tab = jnp.concatenate([tab_scr[pl.ds(r0, nt), :], jnp.zeros((LANES - nt, 5 * LANES), BF16)], axis=0)
        g = _dot(inrow.astype(BF16), tab)
        hit = (32.0 * g[:, :LANES] + g[:, LANES:2 * LANES]) == slot
        ag = g[:, 2 * LANES:3 * LANES] + g[:, 3 * LANES:4 * LANES] + g[:, 4 * LANES:]
        pos = (jnp.sum(jnp.where(inrow, lanef, 0.0), axis=1, keepdims=True) * float(LANES)
               + jnp.sum(jnp.where(hit, lanef, 0.0), axis=1, keepdims=True))
        val = jnp.sum(jnp.where(hit, ag, 0.0), axis=1, keepdims=True)
        idx_ref[0] = jnp.where(lane == e, pos.astype(I32), idx_ref[0])
        val_ref[0] = jnp.where(lane == e, val, val_ref[0])
        return carry

    lax.fori_loop(0, NE, per_expert, 0)


def _topk_call(lg_t, cap):
    b, _, t = lg_t.shape
    nt = t // LANES
    nr = NE * nt
    return pl.pallas_call(
        functools.partial(_topk_kernel, cap=cap, nt=nt),
        grid=(b,),
        in_specs=[pl.BlockSpec((1, nr, LANES), lambda bi: (bi, 0, 0))],
        out_specs=[pl.BlockSpec((1, cap, LANES), lambda bi: (bi, 0, 0)),
                   pl.BlockSpec((1, cap, LANES), lambda bi: (bi, 0, 0))],
        out_shape=[jax.ShapeDtypeStruct((b, cap, LANES), I32),
                   jax.ShapeDtypeStruct((b, cap, LANES), F32)],
        scratch_shapes=[pltpu.VMEM((nr, 5 * LANES), BF16), pltpu.VMEM((nr, 2 * LANES), F32)],
        compiler_params=_cparams(("parallel",)),
        name="topk",
    )(lg_t.reshape(b, nr, LANES))


MOE_TF = 256
MOE_TN = 512
MOE_MC = 512
MOE_NF = EFF // MOE_TF
MOE_NN = D // MOE_TN
MOE_GC = 352
MOE_GPAD = 32


def _moe_ffn_kernel(idx_ref, hx_hbm, val_ref, gt2_ref, wg_ref, wu_ref, wd_ref, y_ref,
                    xg_scr, hid_scr, stg_scr, vcol_scr, gsem, *, rows):
    e = pl.program_id(0)
    s = pl.program_id(1)
    cur = e % 2
    nxt = jnp.minimum(e + 1, pl.num_programs(0) - 1)

    def issue(expert, c):
        base = expert * rows + c * MOE_GC
        for r in range(MOE_GC):
            pltpu.make_async_copy(hx_hbm.at[pl.ds(idx_ref[base + r], 1), :],
                                  stg_scr.at[pl.ds(r, 1), :], gsem).start(priority=r % 2)

    def land(c, buf):
        pltpu.make_async_copy(hx_hbm.at[pl.ds(0, MOE_GC), :], stg_scr, gsem).wait()
        r0 = pl.multiple_of(c * MOE_GC, MOE_GPAD)
        xg_scr[buf, pl.ds(r0, MOE_GC), :] = stg_scr[...].astype(BF16)

    @pl.when(jnp.logical_and(e == 0, s == 0))
    def _():
        for c in range(MOE_NF):
            issue(0, c)
            land(c, 0)

    @pl.when(jnp.logical_and(s >= 1, s <= MOE_NF))
    def _():
        land(s - 1, 1 - cur)

    @pl.when(s < MOE_NF)
    def _():
        xs = xg_scr[cur, 0:rows, :]
        hid = (_silu(_dot(xs, wg_ref[0].astype(BF16))) * _dot(xs, wu_ref[0].astype(BF16))).astype(BF16)
        hid_scr[:, pl.ds(pl.multiple_of(s * MOE_TF, MOE_TF), MOE_TF)] = hid
        issue(nxt, s)

    @pl.when(s == MOE_NF)
    def _():
        vcol_scr[...] = jnp.broadcast_to(val_ref[0], (LANES, rows)).T

    @pl.when(s >= MOE_NF)
    def _():
        y = _dot(hid_scr[...], wd_ref[0].astype(BF16))
        for mc in range(rows // MOE_MC):
            sl = slice(mc * MOE_MC, (mc + 1) * MOE_MC)
            y_ref[0, sl, :] = (y[sl] * vcol_scr[sl, 0:1] * gt2_ref[mc, 0]).astype(y_ref.dtype)


def _moe_ffn_call(idx_flat, hx2, vals, gt2, w_gate, w_up, w_down, cap):
    rows = vals.shape[2]
    rows_pad = MOE_NF * MOE_GC
    assert rows // MOE_MC == gt2.shape[0] and cap == MOE_MC and rows_pad >= rows and MOE_GC % MOE_GPAD == 0
    idx_flat = jnp.pad(idx_flat, (0, rows_pad - rows))
    gt2t = gt2.reshape(gt2.shape[0], MOE_NN, 1, MOE_TN)
    ph1 = lambda s: jnp.minimum(s, MOE_NF - 1)
    ph2 = lambda s: jnp.maximum(s - MOE_NF, 0)
    grid_spec = pltpu.PrefetchScalarGridSpec(
        num_scalar_prefetch=1,
        grid=(NE, MOE_NF + MOE_NN),
        in_specs=[pl.BlockSpec(memory_space=pl.ANY),
                  pl.BlockSpec((1, 1, rows), lambda e, s, idx: (e, 0, 0)),
                  pl.BlockSpec((gt2.shape[0], 1, 1, MOE_TN), lambda e, s, idx: (0, ph2(s), 0, 0)),
                  pl.BlockSpec((1, D, MOE_TF), lambda e, s, idx: (e, 0, ph1(s))),
                  pl.BlockSpec((1, D, MOE_TF), lambda e, s, idx: (e, 0, ph1(s))),
                  pl.BlockSpec((1, EFF, MOE_TN), lambda e, s, idx: (e, 0, ph2(s)))],
        out_specs=pl.BlockSpec((1, rows, MOE_TN), lambda e, s, idx: (e, 0, ph2(s))),
        scratch_shapes=[pltpu.VMEM((2, rows_pad, D), BF16),
                        pltpu.VMEM((rows, EFF), BF16),
                        pltpu.VMEM((MOE_GC, D), F32),
                        pltpu.VMEM((rows, LANES), F32),
                        pltpu.SemaphoreType.DMA(())],
    )
    return pl.pallas_call(
        functools.partial(_moe_ffn_kernel, rows=rows),
        grid_spec=grid_spec,
        out_shape=jax.ShapeDtypeStruct((NE, rows, D), BF16),
        compiler_params=_cparams(("arbitrary", "arbitrary")),
        name="moe_ffn",
    )(idx_flat, hx2, vals, gt2t, w_gate, w_up, w_down)


MOE_SC = 256
MOE_SS = 4


def _moe_scatter_kernel(idx_ref, y_ref, res_hbm, out_hbm, buf_scr, rsem, wsem, *, rows):
    del res_hbm
    e = pl.program_id(0)
    st = pl.program_id(1)
    nst = pl.num_programs(1)
    c0 = st * MOE_SS

    def reads(chunk, k):
        base = e * rows + chunk * MOE_SC
        for r in range(MOE_SC):
            pltpu.make_async_copy(out_hbm.at[pl.ds(idx_ref[base + r], 1), :],
                                  buf_scr.at[k, pl.ds(r, 1), :], rsem.at[k]).start(priority=r % 2)

    def writes(chunk, k):
        base = e * rows + chunk * MOE_SC
        for r in range(MOE_SC):
            pltpu.make_async_copy(buf_scr.at[k, pl.ds(r, 1), :],
                                  out_hbm.at[pl.ds(idx_ref[base + r], 1), :], wsem.at[k]).start(priority=r % 2)

    def wait_all(sem, k):
        pltpu.make_async_copy(out_hbm.at[pl.ds(0, MOE_SC), :], buf_scr.at[k], sem.at[k]).wait()

    @pl.when(st == 0)
    def _():
        reads(c0, 0)
        reads(c0 + 1, 1)

    for k in range(MOE_SS):
        wait_all(rsem, k)
        buf_scr[k] = buf_scr[k] + y_ref[0, k * MOE_SC:(k + 1) * MOE_SC, :].astype(F32)
        writes(c0 + k, k)
        if k + 2 < MOE_SS:
            @pl.when(st > 0)
            def _():
                wait_all(wsem, k + 2)
            reads(c0 + k + 2, k + 2)
        else:
            @pl.when(st + 1 < nst)
            def _():
                wait_all(wsem, k + 2 - MOE_SS)
                reads(c0 + k + 2, k + 2 - MOE_SS)

    @pl.when(st == nst - 1)
    def _():
        for k in range(MOE_SS):
            wait_all(wsem, k)


def _moe_scatter_call(idx_flat, y, x1):
    rows = y.shape[1]
    grid_spec = pltpu.PrefetchScalarGridSpec(
        num_scalar_prefetch=1,
        grid=(NE, rows // (MOE_SC * MOE_SS)),
        in_specs=[pl.BlockSpec((1, MOE_SC * MOE_SS, D), lambda e, st, idx: (e, st, 0)),
                  pl.BlockSpec(memory_space=pl.ANY)],
        out_specs=pl.BlockSpec(memory_space=pl.ANY),
        scratch_shapes=[pltpu.VMEM((MOE_SS, MOE_SC, D), F32),
                        pltpu.SemaphoreType.DMA((MOE_SS,)),
                        pltpu.SemaphoreType.DMA((MOE_SS,))],
    )
    return pl.pallas_call(
        functools.partial(_moe_scatter_kernel, rows=rows),
        grid_spec=grid_spec,
        out_shape=jax.ShapeDtypeStruct(x1.shape, F32),
        input_output_aliases={2: 0},
        compiler_params=_cparams(("arbitrary", "arbitrary")),
        name="moe_scatter",
    )(idx_flat, y, x1)


def _final_kernel(x_ref, g_ref, o_ref):
    x = x_ref[...]
    ms = jnp.mean(x * x, axis=-1, keepdims=True)
    o_ref[...] = x * lax.rsqrt(ms + EPS) * g_ref[...]


def _final_call(x2d, g):
    m = x2d.shape[0]
    tm = 1024
    return pl.pallas_call(
        _final_kernel,
        grid=(m // tm,),
        in_specs=[pl.BlockSpec((tm, D), lambda i: (i, 0)), pl.BlockSpec((1, D), lambda i: (0, 0))],
        out_specs=pl.BlockSpec((tm, D), lambda i: (i, 0)),
        out_shape=jax.ShapeDtypeStruct(x2d.shape, F32),
        compiler_params=_cparams(("parallel",)),
        name="final_norm",
    )(x2d, g)


def kernel(x, c, ctx, c_ctx, w_mod, b_mod, norm1_g, norm2_g, w_in, conv_w, a_log, dt_bias, gdn_norm_w, w_out,
           w_router, w_gate, w_up, w_down, norm_f):
    b, n, _ = x.shape
    nctx = ctx.shape[1]
    cap = 2 * n // NE
    i = 0

    cc = jnp.concatenate([c, c_ctx[None, :], jnp.zeros((8 - b - 1, D), F32)], axis=0)
    mod = _mod_call(cc, w_mod[i], b_mod[i][None, :])
    sh1, sc1, gt1, sh2, sc2, gt2 = [mod[:b, k * D:(k + 1) * D][:, None, :] for k in range(6)]
    sh1c = mod[b:b + 1, 0:D][:, None, :]
    sc1c = mod[b:b + 1, D:2 * D][:, None, :]

    w_main, w_ab = _wprep_call(w_in, i)
    conv_w8 = jnp.pad(conv_w[i], ((0, 8 - CONV_K), (0, 0)))
    alog_row = jnp.pad(a_log[i].reshape(1, NU), ((0, 0), (0, LANES - NU)))
    dtb_row = jnp.pad(dt_bias[i].reshape(1, NU), ((0, 0), (0, LANES - NU)))
    g1 = norm1_g[i][None, :]

    p_ctx, ab_ctx = _inproj_call(ctx.reshape(1, b * nctx, D), g1, sh1c, sc1c, w_main, w_ab, conv_w8,
                                 b * nctx, 1, 3, nctx)
    _, _, s_ctx = _gdn_call(p_ctx.reshape(b, nctx, 3 * GW), 0, ab_ctx.reshape(b, nctx, LANES), alog_row, dtb_row,
                            jnp.zeros((b, NU, HD, HD), F32))

    p_main, ab = _inproj_call(x, g1, sh1, sc1, w_main, w_ab, conv_w8, 1024, 0, 5, GRID_W)
    o_f, o_b, _ = _gdn_call(p_main, 1, ab, alog_row, dtb_row, s_ctx)
    yf = _fourier_call(p_main)

    w_r = jnp.pad(w_router[i], ((0, 0), (0, LANES - NE)))
    w_r_hi = w_r.astype(BF16)
    w_r = jnp.concatenate([w_r_hi, (w_r - w_r_hi.astype(F32)).astype(BF16)], axis=1)
    x1, hx2, lg_t = _outproj_call(yf, o_f, o_b, p_main, x, w_out[i].astype(BF16), gdn_norm_w[i][None, :], gt1,
                                  norm2_g[i][None, :], sh2, sc2, w_r)

    idx_c, val_c = _topk_call(lg_t, cap)
    idx = jnp.transpose(idx_c[:, :, :NE], (2, 0, 1))
    val = jnp.transpose(val_c[:, :, :NE], (2, 0, 1))
    idx_flat = (idx + (jnp.arange(b, dtype=I32) * n)[None, :, None]).reshape(NE * b * cap)
    vals = val.reshape(NE, 1, b * cap)

    y = _moe_ffn_call(idx_flat, hx2.reshape(b * n, D), vals, gt2, w_gate[i], w_up[i], w_down[i], cap)
    out = _moe_scatter_call(idx_flat, y, x1.reshape(b * n, D))
    return _final_call(out, norm_f[None, :]).reshape(b, n, D)
```

```python
import functools
import math

import numpy as np
import jax
import jax.numpy as jnp
from jax import lax
from jax.experimental import pallas as pl
from jax.experimental.pallas import tpu as pltpu

F32 = jnp.float32
BF16 = jnp.bfloat16
I32 = jnp.int32

D = 2048
SEQ_N = 4096
GRID_W = 64
FW = 1024
FG = 8
FGD = 128
GW = 1024
NH = 8
HD = 128
CONV_K = 5
N_DIR = 2
NE = 16
EFF = 1536
EPS = 1e-6
GC = 128
NU = N_DIR * NH
LANES = 128
VMEM_LIMIT = 56 * 1024 * 1024


def _sigmoid(x):
    return 1.0 / (1.0 + jnp.exp(-x))


def _silu(x):
    return x * _sigmoid(x)


def _softplus(x):
    return jnp.maximum(x, 0.0) + jnp.log(1.0 + jnp.exp(-jnp.abs(x)))


def _dot(a, b):
    return jnp.dot(a, b, preferred_element_type=F32)


def _bmm(a, b):
    return jnp.einsum('uij,ujk->uik', a.astype(BF16), b.astype(BF16), preferred_element_type=F32)


def _bmm_nt(a, b):
    return jnp.einsum('uik,ujk->uij', a, b, preferred_element_type=F32)


def _cparams(sem, vmem=VMEM_LIMIT):
    return pltpu.CompilerParams(dimension_semantics=sem, vmem_limit_bytes=vmem)


def _mod_kernel(c_ref, w_ref, b_ref, o_ref):
    s = _silu(c_ref[...]).astype(BF16)
    o_ref[...] = _dot(s, w_ref[...].astype(BF16)) + b_ref[...]


def _mod_call(cc, w_mod, b_mod):
    tn = 1024
    n = w_mod.shape[1]
    return pl.pallas_call(
        _mod_kernel,
        grid=(n // tn,),
        in_specs=[pl.BlockSpec((8, D), lambda j: (0, 0)),
                  pl.BlockSpec((D, tn), lambda j: (0, j)),
                  pl.BlockSpec((1, tn), lambda j: (0, j))],
        out_specs=pl.BlockSpec((8, tn), lambda j: (0, j)),
        out_shape=jax.ShapeDtypeStruct((8, n), F32),
        compiler_params=_cparams(("parallel",)),
        name="mod",
    )(cc, w_mod, b_mod)


def _wprep_kernel(w_ref, wm_ref, wab_ref):
    nmain = wm_ref.shape[1]
    ngate = w_ref.shape[2] - nmain
    wm_ref[...] = w_ref[0, :, :nmain].astype(BF16)
    wab_ref[...] = jnp.concatenate([w_ref[0, :, nmain:], jnp.zeros((w_ref.shape[1], LANES - ngate), F32)],
                                   axis=1).astype(BF16)


def _wprep_call(w_in, layer):
    _, d, c = w_in.shape
    nmain = FW + 4 * GW
    tr = 256
    return pl.pallas_call(
        _wprep_kernel,
        grid=(d // tr,),
        in_specs=[pl.BlockSpec((1, tr, c), lambda r: (layer, r, 0))],
        out_specs=[pl.BlockSpec((tr, nmain), lambda r: (r, 0)), pl.BlockSpec((tr, LANES), lambda r: (r, 0))],
        out_shape=[jax.ShapeDtypeStruct((d, nmain), BF16), jax.ShapeDtypeStruct((d, LANES), BF16)],
        compiler_params=_cparams(("parallel",)),
        name="wprep",
    )(w_in)


INPROJ_SUB = 256


def _inproj_kernel(x_ref, g_ref, sh_ref, sc_ref, w_ref, wab_ref, cw_ref, o_ref, ab_ref, h_scr, *, col0, row_len):
    c = pl.program_id(2) + col0

    @pl.when(pl.program_id(2) == 0)
    def _():
        x = x_ref[0]
        ms = jnp.mean(x * x, axis=-1, keepdims=True)
        y = x * lax.rsqrt(ms + EPS) * g_ref[...]
        h = (y * (1.0 + sc_ref[0]) + sh_ref[0]).astype(BF16)
        h_scr[...] = h
        ab_ref[0] = _dot(h, wab_ref[...])

    is_qkv = jnp.logical_and(c >= 1, c <= 3)

    @pl.when(jnp.logical_not(is_qkv))
    def _():
        o_ref[0] = _dot(h_scr[...], w_ref[...]).astype(o_ref.dtype)

    @pl.when(is_qkv)
    def _():
        half = CONV_K // 2
        qscale = jnp.where(c == 1, HD ** -0.5, 1.0).astype(F32)

        def shifted(x, off):
            zero = jnp.zeros((abs(off), HD), F32)
            return (jnp.concatenate([x[off:], zero], axis=0) if off > 0
                    else jnp.concatenate([zero, x[:off]], axis=0))

        for rb in range(x_ref.shape[1] // INPROJ_SUB):
            p = _dot(h_scr[rb * INPROJ_SUB:(rb + 1) * INPROJ_SUB, :], w_ref[...])
            for g in range(INPROJ_SUB // row_len):
                rows = slice(g * row_len, (g + 1) * row_len)
                for h in range(NH):
                    cols = slice(h * HD, (h + 1) * HD)
                    x = p[rows, cols]
                    acc = x * cw_ref[half:half + 1, cols]
                    for tap in range(CONV_K):
                        off = tap - half
                        if off != 0:
                            acc = acc + shifted(x, off) * cw_ref[tap:tap + 1, cols]
                    y = _silu(acc)
                    ss = jnp.sum(y * y, axis=-1, keepdims=True)
                    nrm = jnp.where(c < 3, lax.rsqrt(ss + EPS) * qscale, 1.0)
                    o_ref[0, rb * INPROJ_SUB + g * row_len:rb * INPROJ_SUB + (g + 1) * row_len, cols] = (
                        y * nrm).astype(o_ref.dtype)


def _inproj_call(x, g, sh, sc, w_main, w_ab, conv_w8, tm, col0, ncol, row_len):
    b, t, _ = x.shape
    tn = 1024
    assert tm % INPROJ_SUB == 0 and INPROJ_SUB % row_len == 0
    return pl.pallas_call(
        functools.partial(_inproj_kernel, col0=col0, row_len=row_len),
        grid=(b, t // tm, ncol),
        in_specs=[pl.BlockSpec((1, tm, D), lambda bi, i, j: (bi, i, 0)),
                  pl.BlockSpec((1, D), lambda bi, i, j: (0, 0)),
                  pl.BlockSpec((1, 1, D), lambda bi, i, j: (bi, 0, 0)),
                  pl.BlockSpec((1, 1, D), lambda bi, i, j: (bi, 0, 0)),
                  pl.BlockSpec((D, tn), lambda bi, i, j: (0, j + col0)),
                  pl.BlockSpec((D, LANES), lambda bi, i, j: (0, 0)),
                  pl.BlockSpec((8, GW), lambda bi, i, j: (0, jnp.clip(j + col0 - 1, 0, 2)))],
        out_specs=[pl.BlockSpec((1, tm, tn), lambda bi, i, j: (bi, i, j)),
                   pl.BlockSpec((1, tm, LANES), lambda bi, i, j: (bi, i, 0))],
        out_shape=[jax.ShapeDtypeStruct((b, t, ncol * tn), BF16),
                   jax.ShapeDtypeStruct((b, t, LANES), F32)],
        scratch_shapes=[pltpu.VMEM((tm, D), BF16)],
        compiler_params=_cparams(("parallel", "parallel", "arbitrary")),
        name="inproj",
    )(x, g, sh, sc, w_main, w_ab, conv_w8)


def _gdn_kernel(qf_ref, kf_ref, vf_ref, abf_ref, qb_ref, kb_ref, vb_ref, abb_ref, alog_ref, dtb_ref, s0_ref,
                of_ref, ob_ref, sout_ref, s_scr):
    n = pl.program_id(1)

    @pl.when(n == 0)
    def _():
        s_scr[...] = s0_ref[0]

    row = lax.broadcasted_iota(I32, (GC, GC), 0)
    col = lax.broadcasted_iota(I32, (GC, GC), 1)
    eye = (row == col).astype(F32)
    m_parts, rhs_parts, qd_parts, kdt_parts, qk_parts, gt_parts = [], [], [], [], [], []
    for d in range(N_DIR):
        q_ref, k_ref, v_ref, ab_ref = ((qf_ref, kf_ref, vf_ref, abf_ref) if d == 0
                                       else (qb_ref, kb_ref, vb_ref, abb_ref))
        incl = (col <= row) if d == 0 else (col >= row)
        strict = (col < row) if d == 0 else (col > row)
        ab = ab_ref[0]
        g_all = -jnp.exp(alog_ref[...]) * _softplus(ab + dtb_ref[...])
        beta_all = _sigmoid(ab)
        gc_all = jnp.dot(incl.astype(F32), g_all, preferred_element_type=F32,
                         precision=lax.Precision.HIGHEST)
        gc_t = gc_all.T
        last = GC - 1 if d == 0 else 0
        c0 = d * NH
        cb = N_DIR * NH + d * NH
        heads = range(NH)
        gcol = jnp.stack([gc_all[:, c0 + h:c0 + h + 1] for h in heads])
        grow = jnp.stack([gc_t[c0 + h:c0 + h + 1, :] for h in heads])
        glast = jnp.stack([gc_all[last:last + 1, c0 + h:c0 + h + 1] for h in heads])
        beta = jnp.stack([beta_all[:, cb + h:cb + h + 1] for h in heads])
        q = jnp.stack([q_ref[0, :, h * HD:(h + 1) * HD] for h in heads]).astype(F32)
        k = jnp.stack([k_ref[0, :, h * HD:(h + 1) * HD] for h in heads])
        kf = k.astype(F32)
        v = jnp.stack([v_ref[0, :, h * HD:(h + 1) * HD] for h in heads]).astype(F32)
        decay = jnp.where(incl[None], jnp.exp(jnp.where(incl[None], gcol - grow, 0.0)), 0.0)
        egc = jnp.exp(gcol)
        kbeta = kf * beta
        a2 = _bmm_nt(jnp.concatenate([kbeta, q], axis=1).astype(BF16), k)
        m_parts.append(jnp.where(strict[None], a2[:, :GC] * decay, 0.0))
        rhs_parts.append(jnp.concatenate([v * beta, kbeta * egc], axis=2).astype(BF16))
        qd_parts.append((q * egc).astype(BF16))
        kdt_parts.append(jnp.swapaxes(kf * jnp.exp(glast - gcol), 1, 2).astype(BF16))
        qk_parts.append((a2[:, GC:] * decay).astype(BF16))
        gt_parts.append(jnp.exp(glast))

    cat = lambda parts: jnp.concatenate(parts, axis=0)
    m, rhs, qd, kdt, qk, gtot = (cat(t) for t in (m_parts, rhs_parts, qd_parts, kdt_parts, qk_parts, gt_parts))
    blk = lambda s: ((row >> int(math.log2(s))) == (col >> int(math.log2(s))))[None]
    mb = jnp.where(blk(8), m, 0.0)
    p = eye[None] - mb
    m2 = _bmm(mb, mb)
    p = p + _bmm(p, m2)
    p = p + _bmm(p, _bmm(m2, m2))
    s = 8
    while s < GC:
        cpart = jnp.where(jnp.logical_and(blk(2 * s), jnp.logical_not(blk(s))), m, 0.0)
        p = p - _bmm(p, _bmm(cpart, p))
        s *= 2
    uw = _bmm(p, rhs)

    st = s_scr[...]
    r = _bmm(jnp.concatenate([uw[:, :, HD:].astype(BF16), qd], axis=1), st)
    vb = (uw[:, :, :HD] - r[:, :GC]).astype(BF16)
    o = r[:, GC:] + _bmm(qk, vb)
    s_scr[...] = st * gtot + _bmm(kdt, vb)
    for un in range(NU):
        h = un % NH
        if un < NH:
            of_ref[0, :, h * HD:(h + 1) * HD] = o[un].astype(of_ref.dtype)
        else:
            ob_ref[0, :, h * HD:(h + 1) * HD] = o[un].astype(ob_ref.dtype)

    @pl.when(n == pl.num_programs(1) - 1)
    def _():
        sout_ref[0] = s_scr[...]


def _gdn_call(qkv, col0, ab, alog_row, dtb_row, s0):
    b, t, _ = qkv.shape
    nc = t // GC
    blk = lambda c, rev: pl.BlockSpec((1, GC, GW), (lambda bi, n: (bi, nc - 1 - n, c + col0)) if rev
                                      else (lambda bi, n: (bi, n, c + col0)))
    abblk = lambda rev: pl.BlockSpec((1, GC, LANES), (lambda bi, n: (bi, nc - 1 - n, 0)) if rev
                                     else (lambda bi, n: (bi, n, 0)))
    sblk = pl.BlockSpec((1, NU, HD, HD), lambda bi, n: (bi, 0, 0, 0))
    return pl.pallas_call(
        _gdn_kernel,
        grid=(b, nc),
        in_specs=[blk(0, False), blk(1, False), blk(2, False), abblk(False),
                  blk(0, True), blk(1, True), blk(2, True), abblk(True),
                  pl.BlockSpec((1, LANES), lambda bi, n: (0, 0)),
                  pl.BlockSpec((1, LANES), lambda bi, n: (0, 0)),
                  sblk],
        out_specs=[pl.BlockSpec((1, GC, GW), lambda bi, n: (bi, n, 0)),
                   pl.BlockSpec((1, GC, GW), lambda bi, n: (bi, nc - 1 - n, 0)),
                   sblk],
        out_shape=[jax.ShapeDtypeStruct((b, t, GW), BF16),
                   jax.ShapeDtypeStruct((b, t, GW), BF16),
                   jax.ShapeDtypeStruct((b, NU, HD, HD), F32)],
        scratch_shapes=[pltpu.VMEM((NU, HD, HD), F32)],
        compiler_params=_cparams(("parallel", "arbitrary")),
        name="gdn",
    )(qkv, qkv, qkv, ab, qkv, qkv, qkv, ab, alog_row, dtb_row, s0)


def _dft_tables():
    r = GRID_W
    a = 2.0 * np.pi * np.outer(np.arange(r), np.arange(r)) / r
    c64, s64 = np.cos(a), np.sin(a)
    ac = 2.0 * np.pi * np.outer(np.arange(FGD), np.arange(FGD)) / FGD
    cc, sc = np.cos(ac), np.sin(ac)
    at = (2.0 * np.pi * np.outer(np.arange(r), np.arange(r)) / (r * r)).reshape(r * r, 1)
    chan = np.concatenate([cc, -sc], axis=1)
    row_cs = np.concatenate([c64, s64], axis=0)
    col_re = np.concatenate([c64, s64], axis=1)
    as_bf16 = lambda t: jnp.asarray(t, F32).astype(BF16)
    return (as_bf16(chan), as_bf16(row_cs), as_bf16(col_re),
            jnp.asarray(np.cos(at), F32), jnp.asarray(np.sin(at), F32))


def _fourier_kernel(x_ref, chan_ref, rowcs_ref, colre_ref, twc_ref, tws_ref, o_ref, u_scr, v_scr, y_scr):
    r = GRID_W

    def swap(t):
        return jnp.swapaxes(t.reshape(r, r, FGD), 0, 1).reshape(r * r, FGD)

    u = _dot(x_ref[0], chan_ref[...])
    u_scr[0] = swap(u[:, :FGD])
    u_scr[1] = swap(u[:, FGD:])

    def over_n2(n1, carry):
        rows = pl.ds(pl.multiple_of(n1 * r, r), r)
        x = jnp.concatenate([u_scr[0, rows, :], u_scr[1, rows, :]], axis=1).astype(BF16)
        cs = _dot(rowcs_ref[...], x)
        vr = cs[:r, :FGD] + cs[r:, FGD:]
        vi = cs[:r, FGD:] - cs[r:, :FGD]
        tc = twc_ref[rows, :]
        ts = tws_ref[rows, :]
        v_scr[0, rows, :] = vr * tc + vi * ts
        v_scr[1, rows, :] = vi * tc - vr * ts
        return carry

    lax.fori_loop(0, r, over_n2, 0, unroll=16)
    u_scr[0] = swap(v_scr[0])
    u_scr[1] = swap(v_scr[1])

    def over_n1(k2, carry):
        rows = pl.ds(pl.multiple_of(k2 * r, r), r)
        st = jnp.concatenate([u_scr[0, rows, :], u_scr[1, rows, :]], axis=0).astype(BF16)
        y_scr[rows, :] = _dot(colre_ref[...], st) * (1.0 / math.sqrt(SEQ_N * FGD))
        return carry

    lax.fori_loop(0, r, over_n1, 0, unroll=16)
    o_ref[0] = swap(y_scr[...]).astype(o_ref.dtype)


def _fourier_call(p_main):
    b, n, _ = p_main.shape
    r = GRID_W
    chan, rowcs, colre, twc, tws = _dft_tables()
    twc = jnp.broadcast_to(twc, (n, LANES))
    tws = jnp.broadcast_to(tws, (n, LANES))
    const = lambda shape: pl.BlockSpec(shape, lambda bi, g: (0, 0))
    return pl.pallas_call(
        _fourier_kernel,
        grid=(b, FG),
        in_specs=[pl.BlockSpec((1, n, FGD), lambda bi, g: (bi, 0, g)),
                  const((FGD, 2 * FGD)), const((2 * r, r)), const((r, 2 * r)),
                  const((n, LANES)), const((n, LANES))],
        out_specs=pl.BlockSpec((1, n, FGD), lambda bi, g: (bi, 0, g)),
        out_shape=jax.ShapeDtypeStruct((b, n, FW), BF16),
        scratch_shapes=[pltpu.VMEM((2, n, FGD), F32), pltpu.VMEM((2, n, FGD), F32), pltpu.VMEM((n, FGD), F32)],
        compiler_params=_cparams(("parallel", "parallel")),
        name="fourier",
    )(p_main, chan, rowcs, colre, twc, tws)


OUTPROJ_TM = 512
OUTPROJ_SUB = 256


def _outproj_kernel(yf_ref, of_ref, ob_ref, z_ref, x_ref, wout_ref, gnw_ref, gt1_ref, g2_ref, sh2_ref, sc2_ref,
                    wr_ref, x1_ref, hx_ref, lg_ref):
    sub = OUTPROJ_SUB
    for rb in range(x_ref.shape[1] // sub):
        rows = slice(rb * sub, (rb + 1) * sub)
        parts = [yf_ref[0, rows, :]]
        for h in range(NH):
            cols = slice(h * HD, (h + 1) * HD)
            oh = of_ref[0, rows, cols].astype(F32) + ob_ref[0, rows, cols].astype(F32)
            ms = jnp.mean(oh * oh, axis=-1, keepdims=True)
            on = oh * lax.rsqrt(ms + EPS) * gnw_ref[...]
            parts.append((on * _silu(z_ref[0, rows, cols].astype(F32))).astype(BF16))
        mix = jnp.concatenate(parts, axis=1)
        x1 = x_ref[0, rows, :] + gt1_ref[0] * _dot(mix, wout_ref[...])
        x1_ref[0, rows, :] = x1
        ms = jnp.mean(x1 * x1, axis=-1, keepdims=True)
        hx = x1 * lax.rsqrt(ms + EPS) * g2_ref[...] * (1.0 + sc2_ref[0]) + sh2_ref[0]
        hx_ref[0, rows, :] = hx
        hx_hi = hx.astype(BF16)
        hx_lo = (hx - hx_hi.astype(F32)).astype(BF16)
        l2 = _dot(hx_hi, wr_ref[...])
        lg = l2[:, :LANES] + l2[:, LANES:] + _dot(hx_lo, wr_ref[:, :LANES])
        lg_ref[0, :, rows] = lg.T[:NE, :]


def _outproj_call(yf, o_f, o_b, p_main, x, w_out, gnw, gt1, g2, sh2, sc2, w_r):
    b, t, _ = x.shape
    tm = OUTPROJ_TM
    zcol = p_main.shape[2] // GW - 1
    row = lambda c: pl.BlockSpec((1, tm, c), lambda bi, i: (bi, i, 0))
    vec = pl.BlockSpec((1, 1, D), lambda bi, i: (bi, 0, 0))
    once = pl.Buffered(1)
    return pl.pallas_call(
        _outproj_kernel,
        grid=(b, t // tm),
        in_specs=[row(FW), row(GW), row(GW),
                  pl.BlockSpec((1, tm, GW), lambda bi, i: (bi, i, zcol)),
                  row(D),
                  pl.BlockSpec((D, D), lambda bi, i: (0, 0), pipeline_mode=once),
                  pl.BlockSpec((1, HD), lambda bi, i: (0, 0)),
                  vec,
                  pl.BlockSpec((1, D), lambda bi, i: (0, 0)),
                  vec, vec,
                  pl.BlockSpec((D, 2 * LANES), lambda bi, i: (0, 0), pipeline_mode=once)],
        out_specs=[row(D), row(D), pl.BlockSpec((1, NE, tm), lambda bi, i: (bi, 0, i))],
        out_shape=[jax.ShapeDtypeStruct((b, t, D), F32),
                   jax.ShapeDtypeStruct((b, t, D), F32),
                   jax.ShapeDtypeStruct((b, NE, t), F32)],
        compiler_params=_cparams(("parallel", "parallel")),
        name="outproj",
    )(yf, o_f, o_b, p_main, x, w_out, gnw, gt1, g2, sh2, sc2, w_r)


def _lane_cumsum(x):
    n = x.shape[-1]
    lane = lax.broadcasted_iota(I32, x.shape, x.ndim - 1)
    sh = 1
    while sh < n:
        x = x + jnp.where(lane >= sh, pltpu.roll(x, shift=sh, axis=x.ndim - 1), 0.0)
        sh *= 2
    return x


def _topk_kernel(lg_ref, idx_ref, val_ref, tab_scr, rend_scr, *, cap, nt):
    nr = NE * nt
    lg = lg_ref[0]
    lg3 = lg.reshape(NE, nt, LANES)
    ex3 = jnp.exp(lg3 - jnp.max(lg3, axis=0, keepdims=True))
    aff = (ex3 / jnp.sum(ex3, axis=0, keepdims=True)).reshape(nr, LANES)

    rr = lax.broadcasted_iota(I32, (nr, nr), 0)
    rc = lax.broadcasted_iota(I32, (nr, nr), 1)
    ntb = int(math.log2(nt))
    same = (rr >> ntb) == (rc >> ntb)
    grp = same.astype(BF16)
    before = jnp.logical_and(same, rc < rr).astype(BF16)

    def per_expert_sum(mat, mask):
        return jnp.sum(_dot(mat, mask.astype(BF16)), axis=1, keepdims=True)

    def expert_count(mask):
        tot = jnp.sum(mask.reshape(NE, nt, LANES), axis=(1, 2), keepdims=True)
        return jnp.broadcast_to(tot, (NE, nt, 1)).reshape(nr, 1)

    def search(i, cur):
        cand = cur | (1 << (30 - i))
        cnt = expert_count((aff >= pltpu.bitcast(cand, F32)).astype(F32))
        return jnp.where(cnt >= cap, cand, cur)

    thr = pltpu.bitcast(lax.fori_loop(0, 31, search, jnp.zeros((nr, 1), I32)), F32)
    gt = aff > thr
    eq = (aff == thr).astype(F32)

    def rank_of(m):
        return _lane_cumsum(m) + per_expert_sum(before, m)

    n_gt = per_expert_sum(grp, gt.astype(F32))
    sel = jnp.logical_or(gt, jnp.logical_and(eq > 0.0, rank_of(eq) <= cap - n_gt))
    selm = sel.astype(F32)
    rsel = jnp.where(sel, rank_of(selm), 0.0)
    rprev = per_expert_sum(before, selm)
    rend_scr[...] = jnp.concatenate([jnp.broadcast_to(rprev, (nr, LANES)),
                                     jnp.broadcast_to(rprev + jnp.sum(selm, axis=1, keepdims=True), (nr, LANES))],
                                    axis=1)
    hi = jnp.floor(rsel * (1.0 / 32.0))
    a1 = aff.astype(BF16)
    a2 = (aff - a1.astype(F32)).astype(BF16)
    a3 = (aff - a1.astype(F32) - a2.astype(F32)).astype(BF16)
    tab_scr[...] = jnp.concatenate([hi.astype(BF16), (rsel - 32.0 * hi).astype(BF16), a1, a2, a3], axis=1)
    idx_ref[0] = jnp.zeros((cap, LANES), I32)
    val_ref[0] = jnp.zeros((cap, LANES), F32)
    slot = (lax.broadcasted_iota(I32, (cap, 1), 0) + 1).astype(F32)
    lane = lax.broadcasted_iota(I32, (cap, LANES), 1)
    lanef = lane.astype(F32)
    pad = jnp.full((LANES - nt, 2 * LANES), 2.0 * cap, F32)

    def per_expert(e, carry):
        r0 = pl.multiple_of(e * nt, nt)
        rt = jnp.concatenate([rend_scr[pl.ds(r0, nt), :], pad], axis=0)
        prev_l = rt[:, :LANES].T[0:1, :]
        end_l = rt[:, LANES:].T[0:1, :]
        inrow = jnp.logical_and(prev_l < slot, slot <= end_l)
        tab = jnp.concatenate([tab_scr[pl.ds(r0, nt), :], jnp.zeros((LANES - nt, 5 * LANES), BF16)], axis=0)
        g = _dot(inrow.astype(BF16), tab)
        hit = (32.0 * g[:, :LANES] + g[:, LANES:2 * LANES]) == slot
        ag = g[:, 2 * LANES:3 * LANES] + g[:, 3 * LANES:4 * LANES] + g[:, 4 * LANES:]
        pos = (jnp.sum(jnp.where(inrow, lanef, 0.0), axis=1, keepdims=True) * float(LANES)
               + jnp.sum(jnp.where(hit, lanef, 0.0), axis=1, keepdims=True))
        val = jnp.sum(jnp.where(hit, ag, 0.0), axis=1, keepdims=True)
        idx_ref[0] = jnp.where(lane == e, pos.astype(I32), idx_ref[0])
        val_ref[0] = jnp.where(lane == e, val, val_ref[0])
        return carry

    lax.fori_loop(0, NE, per_expert, 0)


def _topk_call(lg_t, cap):
    b, _, t = lg_t.shape
    nt = t // LANES
    nr = NE * nt
    return pl.pallas_call(
        functools.partial(_topk_kernel, cap=cap, nt=nt),
        grid=(b,),
        in_specs=[pl.BlockSpec((1, nr, LANES), lambda bi: (bi, 0, 0))],
        out_specs=[pl.BlockSpec((1, cap, LANES), lambda bi: (bi, 0, 0)),
                   pl.BlockSpec((1, cap, LANES), lambda bi: (bi, 0, 0))],
        out_shape=[jax.ShapeDtypeStruct((b, cap, LANES), I32),
                   jax.ShapeDtypeStruct((b, cap, LANES), F32)],
        scratch_shapes=[pltpu.VMEM((nr, 5 * LANES), BF16), pltpu.VMEM((nr, 2 * LANES), F32)],
        compiler_params=_cparams(("parallel",)),
        name="topk",
    )(lg_t.reshape(b, nr, LANES))


MOE_TF = 256
MOE_TN = 512
MOE_MC = 512
MOE_NF = EFF // MOE_TF
MOE_NN = D // MOE_TN
MOE_GC = 352
MOE_GPAD = 32


def _moe_ffn_kernel(idx_ref, hx_hbm, val_ref, gt2_ref, wg_ref, wu_ref, wd_ref, y_ref,
                    xg_scr, hid_scr, stg_scr, vcol_scr, gsem, *, rows):
    e = pl.program_id(0)
    s = pl.program_id(1)
    cur = e % 2
    nxt = jnp.minimum(e + 1, pl.num_programs(0) - 1)

    def issue(expert, c):
        base = expert * rows + c * MOE_GC
        for r in range(MOE_GC):
            pltpu.make_async_copy(hx_hbm.at[pl.ds(idx_ref[base + r], 1), :],
                                  stg_scr.at[pl.ds(r, 1), :], gsem).start(priority=r % 2)

    def land(c, buf):
        pltpu.make_async_copy(hx_hbm.at[pl.ds(0, MOE_GC), :], stg_scr, gsem).wait()
        r0 = pl.multiple_of(c * MOE_GC, MOE_GPAD)
        xg_scr[buf, pl.ds(r0, MOE_GC), :] = stg_scr[...].astype(BF16)

    @pl.when(jnp.logical_and(e == 0, s == 0))
    def _():
        for c in range(MOE_NF):
            issue(0, c)
            land(c, 0)

    @pl.when(jnp.logical_and(s >= 1, s <= MOE_NF))
    def _():
        land(s - 1, 1 - cur)

    @pl.when(s < MOE_NF)
    def _():
        xs = xg_scr[cur, 0:rows, :]
        hid = (_silu(_dot(xs, wg_ref[0].astype(BF16))) * _dot(xs, wu_ref[0].astype(BF16))).astype(BF16)
        hid_scr[:, pl.ds(pl.multiple_of(s * MOE_TF, MOE_TF), MOE_TF)] = hid
        issue(nxt, s)

    @pl.when(s == MOE_NF)
    def _():
        vcol_scr[...] = jnp.broadcast_to(val_ref[0], (LANES, rows)).T

    @pl.when(s >= MOE_NF)
    def _():
        y = _dot(hid_scr[...], wd_ref[0].astype(BF16))
        for mc in range(rows // MOE_MC):
            sl = slice(mc * MOE_MC, (mc + 1) * MOE_MC)
            y_ref[0, sl, :] = (y[sl] * vcol_scr[sl, 0:1] * gt2_ref[mc, 0]).astype(y_ref.dtype)


def _moe_ffn_call(idx_flat, hx2, vals, gt2, w_gate, w_up, w_down, cap):
    rows = vals.shape[2]
    rows_pad = MOE_NF * MOE_GC
    assert rows // MOE_MC == gt2.shape[0] and cap == MOE_MC and rows_pad >= rows and MOE_GC % MOE_GPAD == 0
    idx_flat = jnp.pad(idx_flat, (0, rows_pad - rows))
    gt2t = gt2.reshape(gt2.shape[0], MOE_NN, 1, MOE_TN)
    ph1 = lambda s: jnp.minimum(s, MOE_NF - 1)
    ph2 = lambda s: jnp.maximum(s - MOE_NF, 0)
    grid_spec = pltpu.PrefetchScalarGridSpec(
        num_scalar_prefetch=1,
        grid=(NE, MOE_NF + MOE_NN),
        in_specs=[pl.BlockSpec(memory_space=pl.ANY),
                  pl.BlockSpec((1, 1, rows), lambda e, s, idx: (e, 0, 0)),
                  pl.BlockSpec((gt2.shape[0], 1, 1, MOE_TN), lambda e, s, idx: (0, ph2(s), 0, 0)),
                  pl.BlockSpec((1, D, MOE_TF), lambda e, s, idx: (e, 0, ph1(s))),
                  pl.BlockSpec((1, D, MOE_TF), lambda e, s, idx: (e, 0, ph1(s))),
                  pl.BlockSpec((1, EFF, MOE_TN), lambda e, s, idx: (e, 0, ph2(s)))],
        out_specs=pl.BlockSpec((1, rows, MOE_TN), lambda e, s, idx: (e, 0, ph2(s))),
        scratch_shapes=[pltpu.VMEM((2, rows_pad, D), BF16),
                        pltpu.VMEM((rows, EFF), BF16),
                        pltpu.VMEM((MOE_GC, D), F32),
                        pltpu.VMEM((rows, LANES), F32),
                        pltpu.SemaphoreType.DMA(())],
    )
    return pl.pallas_call(
        functools.partial(_moe_ffn_kernel, rows=rows),
        grid_spec=grid_spec,
        out_shape=jax.ShapeDtypeStruct((NE, rows, D), BF16),
        compiler_params=_cparams(("arbitrary", "arbitrary")),
        name="moe_ffn",
    )(idx_flat, hx2, vals, gt2t, w_gate, w_up, w_down)


MOE_SC = 256
MOE_SS = 4


def _moe_scatter_kernel(idx_ref, y_ref, res_hbm, out_hbm, buf_scr, rsem, wsem, *, rows):
    del res_hbm
    e = pl.program_id(0)
    st = pl.program_id(1)
    nst = pl.num_programs(1)
    c0 = st * MOE_SS

    def reads(chunk, k):
        base = e * rows + chunk * MOE_SC
        for r in range(MOE_SC):
            pltpu.make_async_copy(out_hbm.at[pl.ds(idx_ref[base + r], 1), :],
                                  buf_scr.at[k, pl.ds(r, 1), :], rsem.at[k]).start(priority=r % 2)

    def writes(chunk, k):
        base = e * rows + chunk * MOE_SC
        for r in range(MOE_SC):
            pltpu.make_async_copy(buf_scr.at[k, pl.ds(r, 1), :],
                                  out_hbm.at[pl.ds(idx_ref[base + r], 1), :], wsem.at[k]).start(priority=r % 2)

    def wait_all(sem, k):
        pltpu.make_async_copy(out_hbm.at[pl.ds(0, MOE_SC), :], buf_scr.at[k], sem.at[k]).wait()

    @pl.when(st == 0)
    def _():
        reads(c0, 0)
        reads(c0 + 1, 1)

    for k in range(MOE_SS):
        wait_all(rsem, k)
        buf_scr[k] = buf_scr[k] + y_ref[0, k * MOE_SC:(k + 1) * MOE_SC, :].astype(F32)
        writes(c0 + k, k)
        if k + 2 < MOE_SS:
            @pl.when(st > 0)
            def _():
                wait_all(wsem, k + 2)
            reads(c0 + k + 2, k + 2)
        else:
            @pl.when(st + 1 < nst)
            def _():
                wait_all(wsem, k + 2 - MOE_SS)
                reads(c0 + k + 2, k + 2 - MOE_SS)

    @pl.when(st == nst - 1)
    def _():
        for k in range(MOE_SS):
            wait_all(wsem, k)


def _moe_scatter_call(idx_flat, y, x1):
    rows = y.shape[1]
    grid_spec = pltpu.PrefetchScalarGridSpec(
        num_scalar_prefetch=1,
        grid=(NE, rows // (MOE_SC * MOE_SS)),
        in_specs=[pl.BlockSpec((1, MOE_SC * MOE_SS, D), lambda e, st, idx: (e, st, 0)),
                  pl.BlockSpec(memory_space=pl.ANY)],
        out_specs=pl.BlockSpec(memory_space=pl.ANY),
        scratch_shapes=[pltpu.VMEM((MOE_SS, MOE_SC, D), F32),
                        pltpu.SemaphoreType.DMA((MOE_SS,)),
                        pltpu.SemaphoreType.DMA((MOE_SS,))],
    )
    return pl.pallas_call(
        functools.partial(_moe_scatter_kernel, rows=rows),
        grid_spec=grid_spec,
        out_shape=jax.ShapeDtypeStruct(x1.shape, F32),
        input_output_aliases={2: 0},
        compiler_params=_cparams(("arbitrary", "arbitrary")),
        name="moe_scatter",
    )(idx_flat, y, x1)


def _final_kernel(x_ref, g_ref, o_ref):
    x = x_ref[...]
    ms = jnp.mean(x * x, axis=-1, keepdims=True)
    o_ref[...] = x * lax.rsqrt(ms + EPS) * g_ref[...]


def _final_call(x2d, g):
    m = x2d.shape[0]
    tm = 1024
    return pl.pallas_call(
        _final_kernel,
        grid=(m // tm,),
        in_specs=[pl.BlockSpec((tm, D), lambda i: (i, 0)), pl.BlockSpec((1, D), lambda i: (0, 0))],
        out_specs=pl.BlockSpec((tm, D), lambda i: (i, 0)),
        out_shape=jax.ShapeDtypeStruct(x2d.shape, F32),
        compiler_params=_cparams(("parallel",)),
        name="final_norm",
    )(x2d, g)


def kernel(x, c, ctx, c_ctx, w_mod, b_mod, norm1_g, norm2_g, w_in, conv_w, a_log, dt_bias, gdn_norm_w, w_out,
           w_router, w_gate, w_up, w_down, norm_f):
    b, n, _ = x.shape
    nctx = ctx.shape[1]
    cap = 2 * n // NE
    i = 0

    cc = jnp.concatenate([c, c_ctx[None, :], jnp.zeros((8 - b - 1, D), F32)], axis=0)
    mod = _mod_call(cc, w_mod[i], b_mod[i][None, :])
    sh1, sc1, gt1, sh2, sc2, gt2 = [mod[:b, k * D:(k + 1) * D][:, None, :] for k in range(6)]
    sh1c = mod[b:b + 1, 0:D][:, None, :]
    sc1c = mod[b:b + 1, D:2 * D][:, None, :]

    w_main, w_ab = _wprep_call(w_in, i)
    conv_w8 = jnp.pad(conv_w[i], ((0, 8 - CONV_K), (0, 0)))
    alog_row = jnp.pad(a_log[i].reshape(1, NU), ((0, 0), (0, LANES - NU)))
    dtb_row = jnp.pad(dt_bias[i].reshape(1, NU), ((0, 0), (0, LANES - NU)))
    g1 = norm1_g[i][None, :]

    p_ctx, ab_ctx = _inproj_call(ctx.reshape(1, b * nctx, D), g1, sh1c, sc1c, w_main, w_ab, conv_w8,
                                 b * nctx, 1, 3, nctx)
    _, _, s_ctx = _gdn_call(p_ctx.reshape(b, nctx, 3 * GW), 0, ab_ctx.reshape(b, nctx, LANES), alog_row, dtb_row,
                            jnp.zeros((b, NU, HD, HD), F32))

    p_main, ab = _inproj_call(x, g1, sh1, sc1, w_main, w_ab, conv_w8, 1024, 0, 5, GRID_W)
    o_f, o_b, _ = _gdn_call(p_main, 1, ab, alog_row, dtb_row, s_ctx)
    yf = _fourier_call(p_main)

    w_r = jnp.pad(w_router[i], ((0, 0), (0, LANES - NE)))
    w_r_hi = w_r.astype(BF16)
    w_r = jnp.concatenate([w_r_hi, (w_r - w_r_hi.astype(F32)).astype(BF16)], axis=1)
    x1, hx2, lg_t = _outproj_call(yf, o_f, o_b, p_main, x, w_out[i].astype(BF16), gdn_norm_w[i][None, :], gt1,
                                  norm2_g[i][None, :], sh2, sc2, w_r)

    idx_c, val_c = _topk_call(lg_t, cap)
    idx = jnp.transpose(idx_c[:, :, :NE], (2, 0, 1))
    val = jnp.transpose(val_c[:, :, :NE], (2, 0, 1))
    idx_flat = (idx + (jnp.arange(b, dtype=I32) * n)[None, :, None]).reshape(NE * b * cap)
    vals = val.reshape(NE, 1, b * cap)

    y = _moe_ffn_call(idx_flat, hx2.reshape(b * n, D), vals, gt2, w_gate[i], w_up[i], w_down[i], cap)
    out = _moe_scatter_call(idx_flat, y, x1.reshape(b * n, D))
    return _final_call(out, norm_f[None, :]).reshape(b, n, D)
```
